```python
import jax, jax.numpy as jnp
from jax import lax
import numpy as np

D_MODEL = 1024
BATCH = 4
SEQ = 8192
DEPTH = 4

CHUNK = 64
N_A = DEPTH // 2
N_B = DEPTH - N_A
RET_HEADS = 4
RET_DK = D_MODEL // RET_HEADS
RET_DV = 2 * RET_DK
RET_V = RET_HEADS * RET_DV
MLA_HEADS = 8
QK_NOPE = 128
QK_ROPE = 64
V_HEAD = 128
Q_LORA = 256
KV_LORA = 128
Q_BLOCK = 128
N_EXPERTS = 16
N_GROUPS = 4
EXPERTS_PER_GROUP = N_EXPERTS // N_GROUPS
TOP_K = 2
D_EXPERT = 512
ROPE_THETA = 10000.0
EPS = 1e-6

kernel_name = "yoco_retention_mla_grouped_moe"


def rms_norm(x, g):
    xf = x.astype(jnp.float32)
    y = xf * lax.rsqrt(jnp.mean(xf * xf, axis=-1, keepdims=True) + EPS)
    return (y * g.astype(jnp.float32)).astype(x.dtype)


def modulate(xn, shift, scale):
    return xn * (1 + scale[:, None, :]) + shift[:, None, :]


def rope(x, pos):
    half = x.shape[-1] // 2
    inv = ROPE_THETA ** (-jnp.arange(half, dtype=jnp.float32) / half)
    ang = pos.astype(jnp.float32)[:, :, None] * inv
    cos = jnp.cos(ang)[:, :, None, :]
    sin = jnp.sin(ang)[:, :, None, :]
    xf = x.astype(jnp.float32)
    x1, x2 = xf[..., :half], xf[..., half:]
    return jnp.concatenate([x1 * cos - x2 * sin, x1 * sin + x2 * cos], axis=-1).astype(x.dtype)


def retention(q, k, v):
    B, S, H, dk = q.shape
    dv = v.shape[-1]
    n = S // CHUNK
    log_g = jnp.log1p(-(2.0 ** (-5.0 - jnp.arange(H, dtype=jnp.float32))))
    t = jnp.arange(CHUNK, dtype=jnp.float32)
    diff = t[:, None] - t[None, :]
    decay_intra = jnp.where(diff >= 0, jnp.exp(log_g[:, None, None] * jnp.maximum(diff, 0.0)), 0.0)
    decay_q = jnp.exp(log_g[:, None] * (t + 1.0))[None, :, :, None]
    decay_k = jnp.exp(log_g[:, None] * (CHUNK - 1.0 - t))[None, :, :, None]
    decay_chunk = jnp.exp(log_g * CHUNK)[None, :, None, None]

    def to_chunks(a):
        return a.reshape(B, n, CHUNK, H, a.shape[-1]).transpose(1, 0, 3, 2, 4)

    def step(state, inp):
        qc, kc, vc = inp
        inner = jnp.einsum('bhnd,bhmd->bhnm', qc, kc) * decay_intra
        out = jnp.einsum('bhnm,bhmv->bhnv', inner, vc) + \
            jnp.einsum('bhnd,bhdv->bhnv', qc, state) * decay_q
        state = state * decay_chunk + jnp.einsum('bhmd,bhmv->bhdv', kc * decay_k, vc)
        return state, out

    state0 = jnp.zeros((B, H, dk, dv), jnp.float32)
    _, o = lax.scan(step, state0, (to_chunks(q), to_chunks(k), to_chunks(v)))
    return o.transpose(1, 0, 3, 2, 4).reshape(B, S, H, dv)


def retention_mixer(xn, pos, w_in, w_out):
    B, S, _ = xn.shape
    proj = xn @ w_in
    q = proj[..., :D_MODEL].reshape(B, S, RET_HEADS, RET_DK)
    k = proj[..., D_MODEL:2 * D_MODEL].reshape(B, S, RET_HEADS, RET_DK)
    v = proj[..., 2 * D_MODEL:2 * D_MODEL + RET_V].reshape(B, S, RET_HEADS, RET_DV)
    g = proj[..., 2 * D_MODEL + RET_V:]
    q = rope(q, pos).astype(jnp.float32)
    k = rope(k, pos).astype(jnp.float32) * (RET_DK ** -0.5)
    o = retention(q, k, v.astype(jnp.float32))
    mu = jnp.mean(o, axis=-1, keepdims=True)
    var = jnp.mean(jnp.square(o - mu), axis=-1, keepdims=True)
    o = (o - mu) * lax.rsqrt(var + EPS)
    y = (jax.nn.silu(g.astype(jnp.float32)) * o.reshape(B, S, RET_V)).astype(xn.dtype)
    return y @ w_out


def mla_shared_kv(hn, pos, w_kv_a, kv_norm, w_kv_b):
    B, S, _ = hn.shape
    a = hn @ w_kv_a
    c_kv = rms_norm(a[..., :KV_LORA], kv_norm)
    k_rope = rope(a[..., KV_LORA:][:, :, None, :], pos)[:, :, 0, :]
    kv = (c_kv @ w_kv_b).reshape(B, S, MLA_HEADS, QK_NOPE + V_HEAD)
    return kv[..., :QK_NOPE], k_rope, kv[..., QK_NOPE:]


def mla_attention(xn, pos, w_q_a, q_norm, w_q_b, w_o, k_nope, k_rope, v):
    B, S, _ = xn.shape
    q = (rms_norm(xn @ w_q_a, q_norm) @ w_q_b).reshape(B, S, MLA_HEADS, QK_NOPE + QK_ROPE)
    q_nope = q[..., :QK_NOPE]
    q_rope = rope(q[..., QK_NOPE:], pos)
    scale = (QK_NOPE + QK_ROPE) ** -0.5
    nb = S // Q_BLOCK
    qn_b = q_nope.reshape(B, nb, Q_BLOCK, MLA_HEADS, QK_NOPE).transpose(1, 0, 2, 3, 4)
    qr_b = q_rope.reshape(B, nb, Q_BLOCK, MLA_HEADS, QK_ROPE).transpose(1, 0, 2, 3, 4)
    key_chunk = jnp.arange(S) // CHUNK

    def block(args):
        qn, qr, i = args
        s = (jnp.einsum('bqhd,bkhd->bhqk', qn, k_nope, preferred_element_type=jnp.float32)
             + jnp.einsum('bqhr,bkr->bhqk', qr, k_rope, preferred_element_type=jnp.float32)) * scale
        q_chunk = (i * Q_BLOCK + jnp.arange(Q_BLOCK)) // CHUNK
        mask = key_chunk[None, :] <= q_chunk[:, None]
        p = jax.nn.softmax(jnp.where(mask, s, -jnp.inf), axis=-1)
        return jnp.einsum('bhqk,bkhv->bqhv', p.astype(v.dtype), v)

    o = lax.map(block, (qn_b, qr_b, jnp.arange(nb)))
    o = o.transpose(1, 0, 2, 3, 4).reshape(B, S, MLA_HEADS * V_HEAD)
    return o @ w_o


def grouped_moe(xn, w_router, b_router, w_gate, w_up, w_down):
    B, S, D = xn.shape
    xt = xn.reshape(B * S, D)
    scores = jax.nn.sigmoid((xt @ w_router).astype(jnp.float32))
    biased = scores + b_router.astype(jnp.float32)
    grouped = biased.reshape(-1, N_GROUPS, EXPERTS_PER_GROUP)
    group_score = lax.top_k(grouped, TOP_K)[0].sum(-1)
    g_sel = jnp.argmax(group_score, axis=-1)
    in_group = jnp.take_along_axis(grouped, g_sel[:, None, None], axis=1)[:, 0]
    _, local = lax.top_k(in_group, TOP_K)
    idx = g_sel[:, None] * EXPERTS_PER_GROUP + local
    w = jnp.take_along_axis(scores, idx, axis=-1)
    w = w / jnp.sum(w, axis=-1, keepdims=True)
    combine = jnp.sum(jax.nn.one_hot(idx, N_EXPERTS, dtype=jnp.float32) * w[..., None], axis=1)
    y = jnp.zeros((B * S, D), jnp.float32)
    for e in range(N_EXPERTS):
        hdn = jax.nn.silu(xt @ w_gate[e]) * (xt @ w_up[e])
        y = y + combine[:, e:e + 1] * (hdn @ w_down[e]).astype(jnp.float32)
    return y.astype(xn.dtype).reshape(B, S, D)


def setup_inputs(seed: int = 0) -> dict:
    key = jax.random.key(seed)
    ks = jax.random.split(key, 32)
    f32 = jnp.float32
    D = D_MODEL

    def nrm(k, shape, fan_in, mult=1.0):
        return jax.random.normal(k, shape, f32) * (mult * fan_in ** -0.5)

    def gain(k, shape):
        return 1.0 + 0.02 * jax.random.normal(k, shape, f32)

    offset = jax.random.randint(ks[2], (BATCH, 1), 0, 4096, dtype=jnp.int32)
    positions = offset + jnp.arange(SEQ, dtype=jnp.int32)[None, :]
    return {
        "x": jax.random.normal(ks[0], (BATCH, SEQ, D), f32),
        "c": jax.random.normal(ks[1], (BATCH, D), f32),
        "positions": positions,
        "w_mod": nrm(ks[3], (DEPTH, D, 6 * D), D, 0.5),
        "b_mod": 0.02 * jax.random.normal(ks[4], (DEPTH, 6 * D), f32),
        "norm_mix": gain(ks[5], (DEPTH, D)),
        "norm_ffn": gain(ks[6], (DEPTH, D)),
        "ret_w_in": nrm(ks[7], (N_A, D, 2 * D + 2 * RET_V), D),
        "ret_w_out": nrm(ks[8], (N_A, RET_V, D), RET_V),
        "w_mod_kv": nrm(ks[9], (D, 2 * D), D, 0.5),
        "b_mod_kv": 0.02 * jax.random.normal(ks[10], (2 * D,), f32),
        "norm_kv": gain(ks[11], (D,)),
        "mla_w_kv_a": nrm(ks[12], (D, KV_LORA + QK_ROPE), D),
        "mla_kv_norm": gain(ks[13], (KV_LORA,)),
        "mla_w_kv_b": nrm(ks[14], (KV_LORA, MLA_HEADS * (QK_NOPE + V_HEAD)), KV_LORA),
        "mla_w_q_a": nrm(ks[15], (N_B, D, Q_LORA), D),
        "mla_q_norm": gain(ks[16], (N_B, Q_LORA)),
        "mla_w_q_b": nrm(ks[17], (N_B, Q_LORA, MLA_HEADS * (QK_NOPE + QK_ROPE)), Q_LORA),
        "mla_w_o": nrm(ks[18], (N_B, MLA_HEADS * V_HEAD, D), MLA_HEADS * V_HEAD),
        "router_w": nrm(ks[19], (D, N_EXPERTS), D),
        "router_b": 0.01 * jax.random.normal(ks[20], (N_EXPERTS,), f32),
        "moe_w_gate": nrm(ks[21], (DEPTH, N_EXPERTS, D, D_EXPERT), D),
        "moe_w_up": nrm(ks[22], (DEPTH, N_EXPERTS, D, D_EXPERT), D),
        "moe_w_down": nrm(ks[23], (DEPTH, N_EXPERTS, D_EXPERT, D), D_EXPERT),
        "final_norm": gain(ks[24], (D,)),
    }


def reference(x, c, positions, w_mod, b_mod, norm_mix, norm_ffn, ret_w_in, ret_w_out,
              w_mod_kv, b_mod_kv, norm_kv, mla_w_kv_a, mla_kv_norm, mla_w_kv_b,
              mla_w_q_a, mla_q_norm, mla_w_q_b, mla_w_o, router_w, router_b,
              moe_w_gate, moe_w_up, moe_w_down, final_norm):
    c_act = jax.nn.silu(c)
    h = x
    k_nope = k_rope = v_shared = None
    for layer in range(DEPTH):
        mod = c_act @ w_mod[layer] + b_mod[layer]
        sh1, sc1, g1, sh2, sc2, g2 = jnp.split(mod, 6, axis=-1)
        if layer < N_A:
            xn = modulate(rms_norm(h, norm_mix[layer]), sh1, sc1)
            mix = retention_mixer(xn, positions, ret_w_in[layer], ret_w_out[layer])
        else:
            if layer == N_A:
                kv_mod = c_act @ w_mod_kv + b_mod_kv
                sh_kv, sc_kv = jnp.split(kv_mod, 2, axis=-1)
                hn = modulate(rms_norm(h, norm_kv), sh_kv, sc_kv)
                k_nope, k_rope, v_shared = mla_shared_kv(hn, positions, mla_w_kv_a, mla_kv_norm, mla_w_kv_b)
            j = layer - N_A
            xn = modulate(rms_norm(h, norm_mix[layer]), sh1, sc1)
            mix = mla_attention(xn, positions, mla_w_q_a[j], mla_q_norm[j], mla_w_q_b[j], mla_w_o[j],
                                k_nope, k_rope, v_shared)
        h = h + g1[:, None, :] * mix
        xn = modulate(rms_norm(h, norm_ffn[layer]), sh2, sc2)
        h = h + g2[:, None, :] * grouped_moe(xn, router_w, router_b, moe_w_gate[layer],
                                             moe_w_up[layer], moe_w_down[layer])
    return rms_norm(h, final_norm)
```

```python
import functools

import jax
import jax.numpy as jnp
import numpy as np
from jax import lax
from jax.experimental import pallas as pl
from jax.experimental.pallas import tpu as pltpu

F32 = jnp.float32
BF16 = jnp.bfloat16

CHUNK = 64
RET_HEADS = 4
MLA_HEADS = 8
QK_NOPE = 128
QK_ROPE = 64
V_HEAD = 128
Q_LORA = 256
KV_LORA = 128
N_EXPERTS = 16
N_GROUPS = 4
EXPERTS_PER_GROUP = N_EXPERTS // N_GROUPS
D_EXPERT = 512
ROPE_THETA = 10000.0
EPS = 1e-6

LANES = 128
VMEM_LIMIT = 56 * 1024 * 1024
NEG_BIG = -1e30

RET_CHUNK = 256
TOK_TILE = 512
ATT_TILE = 1024


def _cparams(*sem):
    return pltpu.CompilerParams(dimension_semantics=sem, vmem_limit_bytes=VMEM_LIMIT)


def _silu(x):
    return x * jax.nn.sigmoid(x)


def _rms(x, g):
    return x * lax.rsqrt(jnp.mean(x * x, axis=-1, keepdims=True) + EPS) * g


def _norm_mod(h, g, shift, scale):
    return _rms(h, g) * (1.0 + scale) + shift


def _dot(a, b):
    return jnp.dot(a, b, preferred_element_type=F32)


def _dot_nt(a, b, **kw):
    return lax.dot_general(a, b, (((1,), (1,)), ((), ())), preferred_element_type=F32, **kw)


def _dot_tn(a, b):
    return lax.dot_general(a, b, (((0,), (0,)), ((), ())), preferred_element_type=F32)


def _mod_body(c_ref, w_ref, b_ref, o_ref):
    ca = _silu(c_ref[...])
    o_ref[0] = jnp.dot(ca, w_ref[0], preferred_element_type=F32,
                       precision=lax.Precision.HIGHEST) + b_ref[0]


def _mod_vectors(c8, w, b):
    L, D, N = w.shape
    tn = D
    assert N % tn == 0
    return pl.pallas_call(
        _mod_body,
        out_shape=jax.ShapeDtypeStruct((L, 8, N), F32),
        grid=(L, N // tn),
        in_specs=[pl.BlockSpec((8, D), lambda l, j: (0, 0)),
                  pl.BlockSpec((1, D, tn), lambda l, j: (l, 0, j)),
                  pl.BlockSpec((1, 1, tn), lambda l, j: (l, 0, j))],
        out_specs=pl.BlockSpec((1, 8, tn), lambda l, j: (l, 0, j)),
        compiler_params=_cparams("parallel", "parallel"),
        name="mod_vectors",
    )(c8, w, b.reshape(L, 1, N))


def _rope_body(pos_ref, inv_ref, cm_ref, sm_ref, cos_ref, sin_ref):
    ang = pos_ref[...].astype(F32) * inv_ref[...]
    cos_ref[...] = jnp.cos(ang) * cm_ref[...]
    sin_ref[...] = jnp.sin(ang) * sm_ref[...]


def _rope_tables(pos_col, inv, cm, sm):
    T = pos_col.shape[0]
    tm = min(T, 1024)
    row = pl.BlockSpec((1, LANES), lambda i: (0, 0))
    return pl.pallas_call(
        _rope_body,
        out_shape=(jax.ShapeDtypeStruct((T, LANES), F32),) * 2,
        grid=(T // tm,),
        in_specs=[pl.BlockSpec((tm, 1), lambda i: (i, 0)), row, row, row],
        out_specs=(pl.BlockSpec((tm, LANES), lambda i: (i, 0)),) * 2,
        compiler_params=_cparams("parallel"),
        name="rope_tables",
    )(pos_col, inv, cm, sm)


def _inproj_body(h_ref, g_ref, sh_ref, sc_ref, w_ref, o_ref, *, tn):
    xn = _norm_mod(h_ref[...], g_ref[...], sh_ref[0], sc_ref[0]).astype(BF16)
    for j in range(w_ref.shape[1] // tn):
        o_ref[:, j * tn:(j + 1) * tn] = _dot(xn, w_ref[:, j * tn:(j + 1) * tn]).astype(BF16)


def _ret_inproj(h, g, sh, sc, w, S):
    T, D = h.shape
    N = w.shape[1]
    tm = min(TOK_TILE, S)
    per_b = S // tm
    vec = pl.BlockSpec((1, 1, D), lambda i: (i // per_b, 0, 0))
    return pl.pallas_call(
        functools.partial(_inproj_body, tn=512),
        out_shape=jax.ShapeDtypeStruct((T, N), BF16),
        grid=(T // tm,),
        in_specs=[pl.BlockSpec((tm, D), lambda i: (i, 0)),
                  pl.BlockSpec((1, D), lambda i: (0, 0)),
                  vec, vec,
                  pl.BlockSpec((D, N), lambda i: (0, 0))],
        out_specs=pl.BlockSpec((tm, N), lambda i: (i, 0)),
        compiler_params=_cparams("parallel"),
        name="ret_inproj",
    )(h, g, sh, sc, w)


def _ret_body(q_ref, k_ref, v_ref, g_ref, cos_ref, sin_ref, di_ref, dq_ref, dk_ref, dc_ref,
              y_ref, state_ref, *, dk_dim):
    @pl.when(pl.program_id(2) == 0)
    def _():
        state_ref[...] = jnp.zeros_like(state_ref)

    cos = cos_ref[...]
    sin = sin_ref[...]
    half = dk_dim // 2

    def rope(x):
        x1, x2 = x[:, :half], x[:, half:]
        return jnp.concatenate([x1 * cos - x2 * sin, x1 * sin + x2 * cos], axis=-1)

    qr = rope(q_ref[...].astype(F32))
    kr = rope(k_ref[...].astype(F32)) * (dk_dim ** -0.5)
    qb = qr.astype(BF16)
    v = v_ref[...]
    inner = (_dot_nt(qb, kr.astype(BF16)) * di_ref[0]).astype(BF16)
    st = state_ref[...]
    out = _dot(inner, v) + _dot(qb, st.astype(BF16)) * dq_ref[0]
    kd = (kr * dk_ref[0]).astype(BF16)
    state_ref[...] = st * dc_ref[0] + _dot_tn(kd, v)

    mu = jnp.mean(out, axis=-1, keepdims=True)
    cen = out - mu
    var = jnp.mean(cen * cen, axis=-1, keepdims=True)
    o = cen * lax.rsqrt(var + EPS)
    y_ref[...] = (_silu(g_ref[...].astype(F32)) * o).astype(BF16)


def _retention(proj, cos, sin, S, D):
    T = proj.shape[0]
    B = T // S
    H = RET_HEADS
    dk = D // H
    dv = 2 * dk
    C = min(RET_CHUNK, S)
    n = S // C
    log_g = jnp.log1p(-(2.0 ** (-5.0 - jnp.arange(H, dtype=F32))))
    t = jnp.arange(C, dtype=F32)
    diff = t[:, None] - t[None, :]
    d_intra = jnp.where(diff >= 0, jnp.exp(log_g[:, None, None] * jnp.maximum(diff, 0.0)), 0.0)
    d_q = jnp.exp(log_g[:, None] * (t + 1.0))[:, :, None]
    d_k = jnp.exp(log_g[:, None] * (C - 1.0 - t))[:, :, None]
    d_c = jnp.exp(log_g * C)[:, None, None]

    row = lambda b, h, i: b * n + i
    kq = D // dk
    kv = 2 * D // dv
    kg = kv + H
    return pl.pallas_call(
        functools.partial(_ret_body, dk_dim=dk),
        out_shape=jax.ShapeDtypeStruct((T, H * dv), BF16),
        grid=(B, H, n),
        in_specs=[pl.BlockSpec((C, dk), lambda b, h, i: (row(b, h, i), h)),
                  pl.BlockSpec((C, dk), lambda b, h, i: (row(b, h, i), kq + h)),
                  pl.BlockSpec((C, dv), lambda b, h, i: (row(b, h, i), kv + h)),
                  pl.BlockSpec((C, dv), lambda b, h, i: (row(b, h, i), kg + h)),
                  pl.BlockSpec((C, dk // 2), lambda b, h, i: (row(b, h, i), 0)),
                  pl.BlockSpec((C, dk // 2), lambda b, h, i: (row(b, h, i), 0)),
                  pl.BlockSpec((1, C, C), lambda b, h, i: (h, 0, 0)),
                  pl.BlockSpec((1, C, 1), lambda b, h, i: (h, 0, 0)),
                  pl.BlockSpec((1, C, 1), lambda b, h, i: (h, 0, 0)),
                  pl.BlockSpec((1, 1, 1), lambda b, h, i: (h, 0, 0))],
        out_specs=pl.BlockSpec((C, dv), lambda b, h, i: (row(b, h, i), h)),
        scratch_shapes=[pltpu.VMEM((dk, dv), F32)],
        compiler_params=_cparams("parallel", "parallel", "arbitrary"),
        name="retention",
    )(proj, proj, proj, proj, cos, sin, d_intra, d_q, d_k, d_c)


def _route(logits_t, bias):
    sc = jax.nn.sigmoid(logits_t)
    bi = sc + bias
    s_rows = [sc[e:e + 1, :] for e in range(N_EXPERTS)]
    b_rows = [bi[e:e + 1, :] for e in range(N_EXPERTS)]

    def top2sum(a, b, c, d):
        p, q = jnp.maximum(a, b), jnp.minimum(a, b)
        r, s = jnp.maximum(c, d), jnp.minimum(c, d)
        return jnp.maximum(p, r) + jnp.maximum(jnp.minimum(p, r), jnp.maximum(q, s))

    n = EXPERTS_PER_GROUP
    gs = [top2sum(*b_rows[n * g:n * g + n]) for g in range(N_GROUPS)]
    best, gi = gs[0], jnp.zeros_like(gs[0], dtype=jnp.int32)
    for g in range(1, N_GROUPS):
        upd = gs[g] > best
        gi = jnp.where(upd, g, gi)
        best = jnp.where(upd, gs[g], best)

    def pick(rows, j):
        out = rows[j]
        for g in range(1, N_GROUPS):
            out = jnp.where(gi == g, rows[n * g + j], out)
        return out

    vb = [pick(b_rows, j) for j in range(n)]
    vs = [pick(s_rows, j) for j in range(n)]

    def argmax_first(vals):
        best, idx = vals[0], jnp.zeros_like(gi)
        for j in range(1, n):
            upd = vals[j] > best
            idx = jnp.where(upd, j, idx)
            best = jnp.where(upd, vals[j], best)
        return idx

    i1 = argmax_first(vb)
    i2 = argmax_first([jnp.where(i1 == j, -jnp.inf, vb[j]) for j in range(n)])

    def take(vals, idx):
        out = vals[0]
        for j in range(1, n):
            out = jnp.where(idx == j, vals[j], out)
        return out

    w1, w2 = take(vs, i1), take(vs, i2)
    tot = w1 + w2
    w1, w2 = w1 / tot, w2 / tot
    e1, e2 = gi * n + i1, gi * n + i2
    rows = [jnp.where(e1 == e, w1, 0.0) + jnp.where(e2 == e, w2, 0.0) for e in range(N_EXPERTS)]
    return jnp.concatenate(rows, axis=0)


def _outproj_body(y_ref, w_ref, h_ref, g1_ref, gn_ref, sh_ref, sc_ref, wr_ref, br_ref,
                  ho_ref, xn_ref, cb_ref):
    hn = h_ref[...] + g1_ref[0] * _dot(y_ref[...], w_ref[...])
    ho_ref[...] = hn
    xn = _norm_mod(hn, gn_ref[...], sh_ref[0], sc_ref[0])
    xn_ref[...] = xn.astype(BF16)
    logits_t = _dot_nt(wr_ref[...], xn, precision=lax.Precision.HIGHEST)
    comb_t = _route(logits_t, br_ref[...])
    tm = comb_t.shape[1]
    padded = jnp.concatenate([comb_t, jnp.zeros((LANES - N_EXPERTS, tm), F32)], axis=0)
    cb_ref[...] = padded.T


def _outproj_route(y, w, h, g1, gn, sh, sc, wr_t, br, S):
    T, D = h.shape
    K = y.shape[1]
    tm = min(TOK_TILE, S)
    per_b = S // tm
    vec = pl.BlockSpec((1, 1, D), lambda i: (i // per_b, 0, 0))
    tok = lambda n: pl.BlockSpec((tm, n), lambda i: (i, 0))
    return pl.pallas_call(
        _outproj_body,
        out_shape=(jax.ShapeDtypeStruct((T, D), F32),
                   jax.ShapeDtypeStruct((T, D), BF16),
                   jax.ShapeDtypeStruct((T, LANES), F32)),
        grid=(T // tm,),
        in_specs=[tok(K),
                  pl.BlockSpec((K, D), lambda i: (0, 0)),
                  tok(D), vec,
                  pl.BlockSpec((1, D), lambda i: (0, 0)),
                  vec, vec,
                  pl.BlockSpec((N_EXPERTS, D), lambda i: (0, 0)),
                  pl.BlockSpec((N_EXPERTS, 1), lambda i: (0, 0))],
        out_specs=(tok(D), tok(D), tok(LANES)),
        compiler_params=_cparams("parallel"),
        name="outproj_route",
    )(y, w, h, g1, gn, sh, sc, wr_t, br)


def _moe_body(x_ref, wgu_ref, wd_ref, cb_ref, h_ref, g2_ref, o_ref, acc_ref):
    e = pl.program_id(1)

    @pl.when(e == 0)
    def _():
        acc_ref[...] = jnp.zeros_like(acc_ref)

    hgu = _dot(x_ref[...], wgu_ref[0])
    f = hgu.shape[1] // 2
    hdn = (_silu(hgu[:, :f]) * hgu[:, f:]).astype(BF16)
    z = _dot(hdn, wd_ref[0])
    cb = cb_ref[...]
    lane = lax.broadcasted_iota(jnp.int32, cb.shape, 1)
    ce = jnp.sum(jnp.where(lane == e, cb, 0.0), axis=-1, keepdims=True)
    acc_ref[...] += ce * z

    @pl.when(e == pl.num_programs(1) - 1)
    def _():
        o_ref[...] = h_ref[...] + g2_ref[0] * acc_ref[...]


def _moe(xn, wgu, wd, comb, h, g2, S):
    T, D = h.shape
    E, _, F2 = wgu.shape
    tm = min(TOK_TILE, S)
    per_b = S // tm
    tok = lambda n: pl.BlockSpec((tm, n), lambda i, e: (i, 0))
    return pl.pallas_call(
        _moe_body,
        out_shape=jax.ShapeDtypeStruct((T, D), F32),
        grid=(T // tm, E),
        in_specs=[tok(D),
                  pl.BlockSpec((1, D, F2), lambda i, e: (e, 0, 0)),
                  pl.BlockSpec((1, F2 // 2, D), lambda i, e: (e, 0, 0)),
                  tok(LANES), tok(D),
                  pl.BlockSpec((1, 1, D), lambda i, e: (i // per_b, 0, 0))],
        out_specs=tok(D),
        scratch_shapes=[pltpu.VMEM((tm, D), F32)],
        compiler_params=_cparams("parallel", "arbitrary"),
        name="moe_dense",
    )(xn, wgu, wd, comb, h, g2)


def _kv_body(h_ref, g_ref, sh_ref, sc_ref, wa_ref, gkv_ref, wb_ref, cos_ref, sin_ref,
             k_ref, v_ref):
    hn = _norm_mod(h_ref[...], g_ref[...], sh_ref[0], sc_ref[0]).astype(BF16)
    a = _dot(hn, wa_ref[...])
    c_kv = _rms(a[:, :KV_LORA], gkv_ref[...]).astype(BF16)
    kr = (a[:, KV_LORA:KV_LORA + LANES] * cos_ref[...]
          + a[:, KV_LORA + LANES:] * sin_ref[...]).astype(BF16)
    kv = _dot(c_kv, wb_ref[...])
    hk = MLA_HEADS * QK_NOPE
    v_ref[...] = kv[:, hk:].astype(BF16)
    w = QK_NOPE + LANES
    for hd in range(MLA_HEADS):
        k_ref[:, hd * w:hd * w + QK_NOPE] = kv[:, hd * QK_NOPE:(hd + 1) * QK_NOPE].astype(BF16)
        k_ref[:, hd * w + QK_NOPE:(hd + 1) * w] = kr


def _mla_kv(h, g, sh, sc, wa, gkv, wb, cos, sin, S):
    T, D = h.shape
    tm = min(TOK_TILE, S)
    per_b = S // tm
    vec = pl.BlockSpec((1, 1, D), lambda i: (i // per_b, 0, 0))
    tok = lambda n: pl.BlockSpec((tm, n), lambda i: (i, 0))
    full = lambda a: pl.BlockSpec(a.shape, lambda i: (0, 0))
    kw = MLA_HEADS * (QK_NOPE + LANES)
    vw = MLA_HEADS * V_HEAD
    return pl.pallas_call(
        _kv_body,
        out_shape=(jax.ShapeDtypeStruct((T, kw), BF16), jax.ShapeDtypeStruct((T, vw), BF16)),
        grid=(T // tm,),
        in_specs=[tok(D), full(g), vec, vec, full(wa), full(gkv), full(wb), tok(LANES), tok(LANES)],
        out_specs=(tok(kw), tok(vw)),
        compiler_params=_cparams("parallel"),
        name="mla_kv",
    )(h, g, sh, sc, wa, gkv, wb, cos, sin)


def _q_body(h_ref, g_ref, sh_ref, sc_ref, wa_ref, gq_ref, wb_ref, cos_ref, sin_ref, q_ref, *, scale):
    xn = _norm_mod(h_ref[...], g_ref[...], sh_ref[0], sc_ref[0]).astype(BF16)
    qa = _rms(_dot(xn, wa_ref[...]), gq_ref[...]).astype(BF16)
    cos = cos_ref[...]
    sin = sin_ref[...]
    wi = QK_NOPE + 2 * LANES
    wo = QK_NOPE + LANES
    for hd in range(MLA_HEADS):
        qb = _dot(qa, wb_ref[:, hd * wi:(hd + 1) * wi])
        q_ref[:, hd * wo:hd * wo + QK_NOPE] = (qb[:, :QK_NOPE] * scale).astype(BF16)
        rp = qb[:, QK_NOPE:QK_NOPE + LANES] * cos + qb[:, QK_NOPE + LANES:] * sin
        q_ref[:, hd * wo + QK_NOPE:(hd + 1) * wo] = (rp * scale).astype(BF16)


def _mla_q(h, g, sh, sc, wa, gq, wb, cos, sin, S):
    T, D = h.shape
    tm = min(TOK_TILE, S)
    per_b = S // tm
    vec = pl.BlockSpec((1, 1, D), lambda i: (i // per_b, 0, 0))
    tok = lambda n: pl.BlockSpec((tm, n), lambda i: (i, 0))
    full = lambda a: pl.BlockSpec(a.shape, lambda i: (0, 0))
    qw = MLA_HEADS * (QK_NOPE + LANES)
    return pl.pallas_call(
        functools.partial(_q_body, scale=(QK_NOPE + QK_ROPE) ** -0.5),
        out_shape=jax.ShapeDtypeStruct((T, qw), BF16),
        grid=(T // tm,),
        in_specs=[tok(D), full(g), vec, vec, full(wa), full(gq), full(wb), tok(LANES), tok(LANES)],
        out_specs=tok(qw),
        compiler_params=_cparams("parallel"),
        name="mla_q",
    )(h, g, sh, sc, wa, gq, wb, cos, sin)


def _flash_body(qi_ref, kj_ref, q_ref, k_ref, v_ref, o_ref, m_ref, l_ref, acc_ref, *, tq, tk):
    p_id = pl.program_id(2)
    i = qi_ref[p_id]
    j = kj_ref[p_id]

    @pl.when(j == 0)
    def _():
        m_ref[...] = jnp.full_like(m_ref, NEG_BIG)
        l_ref[...] = jnp.zeros_like(l_ref)
        acc_ref[...] = jnp.zeros_like(acc_ref)

    def step(masked):
        s = _dot_nt(q_ref[...], k_ref[...])
        if masked:
            qc = (i * tq + lax.broadcasted_iota(jnp.int32, s.shape, 0)) // CHUNK
            kc = (j * tk + lax.broadcasted_iota(jnp.int32, s.shape, 1)) // CHUNK
            s = jnp.where(kc <= qc, s, NEG_BIG)
        m_prev = m_ref[...]
        m_next = jnp.maximum(m_prev, jnp.max(s, axis=-1, keepdims=True))
        alpha = jnp.exp(m_prev - m_next)
        p = jnp.exp(s - m_next)
        l_ref[...] = alpha * l_ref[...] + jnp.sum(p, axis=-1, keepdims=True)
        acc_ref[...] = alpha * acc_ref[...] + _dot(p.astype(BF16), v_ref[...])
        m_ref[...] = m_next

    last = (i * tq + tq - 1) // tk

    @pl.when(j < last)
    def _():
        step(False)

    @pl.when(j == last)
    def _():
        step(True)
        o_ref[...] = (acc_ref[...] / l_ref[...]).astype(o_ref.dtype)


def _flash(q, k, v, S):
    T = q.shape[0]
    B = T // S
    H = MLA_HEADS
    tq = tk = min(ATT_TILE, S)
    nq = S // tq
    pairs = [(i, j) for i in range(nq) for j in range((i * tq + tq - 1) // tk + 1)]
    qi = jnp.asarray(np.array([p[0] for p in pairs], np.int32))
    kj = jnp.asarray(np.array([p[1] for p in pairs], np.int32))
    dq = QK_NOPE + LANES
    grid_spec = pltpu.PrefetchScalarGridSpec(
        num_scalar_prefetch=2,
        grid=(B, H, len(pairs)),
        in_specs=[pl.BlockSpec((tq, dq), lambda b, h, p, qi, kj: (b * nq + qi[p], h)),
                  pl.BlockSpec((tk, dq), lambda b, h, p, qi, kj: (b * (S // tk) + kj[p], h)),
                  pl.BlockSpec((tk, V_HEAD), lambda b, h, p, qi, kj: (b * (S // tk) + kj[p], h))],
        out_specs=pl.BlockSpec((tq, V_HEAD), lambda b, h, p, qi, kj: (b * nq + qi[p], h)),
        scratch_shapes=[pltpu.VMEM((tq, 1), F32), pltpu.VMEM((tq, 1), F32),
                        pltpu.VMEM((tq, V_HEAD), F32)],
    )
    return pl.pallas_call(
        functools.partial(_flash_body, tq=tq, tk=tk),
        out_shape=jax.ShapeDtypeStruct((T, H * V_HEAD), BF16),
        grid_spec=grid_spec,
        compiler_params=_cparams("parallel", "parallel", "arbitrary"),
        name="flash_attn",
    )(qi, kj, q, k, v)


def _final_body(h_ref, g_ref, o_ref):
    o_ref[...] = _rms(h_ref[...], g_ref[...])


def _final_norm(h, g):
    T, D = h.shape
    tm = min(T, 1024)
    return pl.pallas_call(
        _final_body,
        out_shape=jax.ShapeDtypeStruct((T, D), F32),
        grid=(T // tm,),
        in_specs=[pl.BlockSpec((tm, D), lambda i: (i, 0)), pl.BlockSpec((1, D), lambda i: (0, 0))],
        out_specs=pl.BlockSpec((tm, D), lambda i: (i, 0)),
        compiler_params=_cparams("parallel"),
        name="final_norm",
    )(h, g)


def _rope_swap(w):
    half = w.shape[-1] // 2
    return jnp.concatenate([w[..., half:], w[..., :half]], axis=-1)


def _pad_lanes(w):
    return jnp.pad(w, [(0, 0)] * (w.ndim - 1) + [(0, LANES - w.shape[-1])])


def kernel(x, c, positions, w_mod, b_mod, norm_mix, norm_ffn, ret_w_in, ret_w_out, w_mod_kv, b_mod_kv, norm_kv, mla_w_kv_a, mla_kv_norm, mla_w_kv_b, mla_w_q_a, mla_q_norm, mla_w_q_b, mla_w_o, router_w, router_b, moe_w_gate, moe_w_up, moe_w_down, final_norm):
    B, S, D = x.shape
    T = B * S
    depth = w_mod.shape[0]
    n_a = ret_w_in.shape[0]

    c8 = jnp.pad(c, ((0, 8 - B), (0, 0)))
    mod = _mod_vectors(c8, w_mod, b_mod)[:, :B]
    mod = mod.reshape(depth, B, 6, 1, D)
    kv_mod = _mod_vectors(c8, w_mod_kv[None], b_mod_kv[None])[0, :B].reshape(B, 2, 1, D)

    pos_col = positions.reshape(T, 1)
    ones = jnp.ones((1, LANES), F32)
    inv_ret = (ROPE_THETA ** (-jnp.arange(LANES, dtype=F32) / LANES)).reshape(1, LANES)
    cos_r, sin_r = _rope_tables(pos_col, inv_ret, ones, ones)
    hr = QK_ROPE // 2
    inv_m = ROPE_THETA ** (-jnp.arange(hr, dtype=F32) / hr)
    inv_m = _pad_lanes(jnp.concatenate([inv_m, inv_m])[None])
    cm = _pad_lanes(jnp.ones((1, QK_ROPE), F32))
    sm = _pad_lanes(jnp.concatenate([-jnp.ones((1, hr), F32), jnp.ones((1, hr), F32)], axis=-1))
    cos_m, sin_m = _rope_tables(pos_col, inv_m, cm, sm)

    wr_t = router_w.T
    br = router_b.reshape(N_EXPERTS, 1)
    wgu = jnp.concatenate([moe_w_gate, moe_w_up], axis=-1).astype(BF16)
    wdn = moe_w_down.astype(BF16)

    h = x.reshape(T, D)
    k_full = v_full = None
    for layer in range(depth):
        sh1, sc1, g1, sh2, sc2, g2 = (mod[layer, :, i] for i in range(6))
        gmix = norm_mix[layer].reshape(1, D)
        if layer < n_a:
            proj = _ret_inproj(h, gmix, sh1, sc1, ret_w_in[layer].astype(BF16), S)
            mix = _retention(proj, cos_r, sin_r, S, D)
            w_o = ret_w_out[layer].astype(BF16)
        else:
            if layer == n_a:
                wa = mla_w_kv_a
                wa_r = wa[:, KV_LORA:]
                wa_p = jnp.concatenate([wa[:, :KV_LORA], _pad_lanes(wa_r), _pad_lanes(_rope_swap(wa_r))],
                                       axis=-1).astype(BF16)
                wb = mla_w_kv_b.reshape(KV_LORA, MLA_HEADS, QK_NOPE + V_HEAD)
                wb_p = jnp.concatenate([wb[..., :QK_NOPE].reshape(KV_LORA, -1),
                                        wb[..., QK_NOPE:].reshape(KV_LORA, -1)], axis=-1).astype(BF16)
                k_full, v_full = _mla_kv(h, norm_kv.reshape(1, D), kv_mod[:, 0], kv_mod[:, 1], wa_p,
                                         mla_kv_norm.reshape(1, KV_LORA), wb_p, cos_m, sin_m, S)
            j = layer - n_a
            wq = mla_w_q_b[j].reshape(Q_LORA, MLA_HEADS, QK_NOPE + QK_ROPE)
            wq_r = wq[..., QK_NOPE:]
            wq_p = jnp.concatenate([wq[..., :QK_NOPE], _pad_lanes(wq_r), _pad_lanes(_rope_swap(wq_r))],
                                   axis=-1).reshape(Q_LORA, -1).astype(BF16)
            q_full = _mla_q(h, gmix, sh1, sc1, mla_w_q_a[j].astype(BF16),
                            mla_q_norm[j].reshape(1, Q_LORA), wq_p, cos_m, sin_m, S)
            mix = _flash(q_full, k_full, v_full, S)
            w_o = mla_w_o[j].astype(BF16)
        h, xn, comb = _outproj_route(mix, w_o, h, g1, norm_ffn[layer].reshape(1, D), sh2, sc2,
                                     wr_t, br, S)
        h = _moe(xn, wgu[layer], wdn[layer], comb, h, g2, S)
    return _final_norm(h, final_norm.reshape(1, D)).reshape(B, S, D)
```

```python
import functools

import jax
import jax.numpy as jnp
import numpy as np
from jax import lax
from jax.experimental import pallas as pl
from jax.experimental.pallas import tpu as pltpu

F32 = jnp.float32
BF16 = jnp.bfloat16

CHUNK = 64
RET_HEADS = 4
MLA_HEADS = 8
QK_NOPE = 128
QK_ROPE = 64
V_HEAD = 128
Q_LORA = 256
KV_LORA = 128
N_EXPERTS = 16
N_GROUPS = 4
EXPERTS_PER_GROUP = N_EXPERTS // N_GROUPS
D_EXPERT = 512
ROPE_THETA = 10000.0
EPS = 1e-6

LANES = 128
VMEM_LIMIT = 56 * 1024 * 1024
NEG_BIG = -1e30
LOG2E = 1.4426950408889634

RET_CHUNK = 256
TOK_TILE = 512
ATT_TILE = 512
ATT_KEYS = 512
ATT_SLAB = 32
SUBLANES = 8


def _cparams(*sem):
    return pltpu.CompilerParams(dimension_semantics=sem, vmem_limit_bytes=VMEM_LIMIT)


def _silu(x):
    return x * jax.nn.sigmoid(x)


def _rms(x, g):
    return x * lax.rsqrt(jnp.mean(x * x, axis=-1, keepdims=True) + EPS) * g


def _norm_mod(h, g, shift, scale):
    return _rms(h, g) * (1.0 + scale) + shift


def _dot(a, b):
    return jnp.dot(a, b, preferred_element_type=F32)


def _dot_nt(a, b, **kw):
    return lax.dot_general(a, b, (((1,), (1,)), ((), ())), preferred_element_type=F32, **kw)


def _dot_tn(a, b):
    return lax.dot_general(a, b, (((0,), (0,)), ((), ())), preferred_element_type=F32)


def _mod_body(c_ref, w_ref, b_ref, o_ref):
    ca = _silu(c_ref[...])
    o_ref[0] = jnp.dot(ca, w_ref[0], preferred_element_type=F32,
                       precision=lax.Precision.HIGHEST) + b_ref[0]


def _mod_vectors(c8, w, b):
    L, D, N = w.shape
    tn = D
    assert N % tn == 0
    return pl.pallas_call(
        _mod_body,
        out_shape=jax.ShapeDtypeStruct((L, 8, N), F32),
        grid=(L, N // tn),
        in_specs=[pl.BlockSpec((8, D), lambda l, j: (0, 0)),
                  pl.BlockSpec((1, D, tn), lambda l, j: (l, 0, j)),
                  pl.BlockSpec((1, 1, tn), lambda l, j: (l, 0, j))],
        out_specs=pl.BlockSpec((1, 8, tn), lambda l, j: (l, 0, j)),
        compiler_params=_cparams("parallel", "parallel"),
        name="mod_vectors",
    )(c8, w, b.reshape(L, 1, N))


def _rope_body(pos_ref, inv_ref, cm_ref, sm_ref, cos_ref, sin_ref):
    ang = pos_ref[...].astype(F32) * inv_ref[...]
    cos_ref[...] = jnp.cos(ang) * cm_ref[...]
    sin_ref[...] = jnp.sin(ang) * sm_ref[...]


def _rope_tables(pos_col, inv, cm, sm):
    T = pos_col.shape[0]
    tm = min(T, 1024)
    row = pl.BlockSpec((1, LANES), lambda i: (0, 0))
    return pl.pallas_call(
        _rope_body,
        out_shape=(jax.ShapeDtypeStruct((T, LANES), F32),) * 2,
        grid=(T // tm,),
        in_specs=[pl.BlockSpec((tm, 1), lambda i: (i, 0)), row, row, row],
        out_specs=(pl.BlockSpec((tm, LANES), lambda i: (i, 0)),) * 2,
        compiler_params=_cparams("parallel"),
        name="rope_tables",
    )(pos_col, inv, cm, sm)


def _inproj_body(h_ref, g_ref, sh_ref, sc_ref, w_ref, o_ref, *, tn):
    xn = _norm_mod(h_ref[...], g_ref[...], sh_ref[0], sc_ref[0]).astype(BF16)
    for j in range(w_ref.shape[1] // tn):
        o_ref[:, j * tn:(j + 1) * tn] = _dot(xn, w_ref[:, j * tn:(j + 1) * tn]).astype(BF16)


def _ret_inproj(h, g, sh, sc, w, S):
    T, D = h.shape
    N = w.shape[1]
    tm = min(TOK_TILE, S)
    per_b = S // tm
    vec = pl.BlockSpec((1, 1, D), lambda i: (i // per_b, 0, 0))
    return pl.pallas_call(
        functools.partial(_inproj_body, tn=512),
        out_shape=jax.ShapeDtypeStruct((T, N), BF16),
        grid=(T // tm,),
        in_specs=[pl.BlockSpec((tm, D), lambda i: (i, 0)),
                  pl.BlockSpec((1, D), lambda i: (0, 0)),
                  vec, vec,
                  pl.BlockSpec((D, N), lambda i: (0, 0))],
        out_specs=pl.BlockSpec((tm, N), lambda i: (i, 0)),
        compiler_params=_cparams("parallel"),
        name="ret_inproj",
    )(h, g, sh, sc, w)


def _ret_body(q_ref, k_ref, v_ref, g_ref, cos_ref, sin_ref, di_ref, dq_ref, dk_ref, dc_ref,
              y_ref, state_ref, *, dk_dim):
    @pl.when(pl.program_id(2) == 0)
    def _():
        state_ref[...] = jnp.zeros_like(state_ref)

    cos = cos_ref[...]
    sin = sin_ref[...]
    half = dk_dim // 2

    def rope(x):
        x1, x2 = x[:, :half], x[:, half:]
        return jnp.concatenate([x1 * cos - x2 * sin, x1 * sin + x2 * cos], axis=-1)

    qr = rope(q_ref[...].astype(F32))
    kr = rope(k_ref[...].astype(F32)) * (dk_dim ** -0.5)
    qb = qr.astype(BF16)
    v = v_ref[...]
    inner = (_dot_nt(qb, kr.astype(BF16)) * di_ref[0]).astype(BF16)
    st = state_ref[...]
    out = _dot(inner, v) + _dot(qb, st.astype(BF16)) * dq_ref[0]
    kd = (kr * dk_ref[0]).astype(BF16)
    state_ref[...] = st * dc_ref[0] + _dot_tn(kd, v)

    mu = jnp.mean(out, axis=-1, keepdims=True)
    cen = out - mu
    var = jnp.mean(cen * cen, axis=-1, keepdims=True)
    o = cen * lax.rsqrt(var + EPS)
    y_ref[...] = (_silu(g_ref[...].astype(F32)) * o).astype(BF16)


def _retention(proj, cos, sin, S, D):
    T = proj.shape[0]
    B = T // S
    H = RET_HEADS
    dk = D // H
    dv = 2 * dk
    C = min(RET_CHUNK, S)
    n = S // C
    log_g = jnp.log1p(-(2.0 ** (-5.0 - jnp.arange(H, dtype=F32))))
    t = jnp.arange(C, dtype=F32)
    diff = t[:, None] - t[None, :]
    d_intra = jnp.where(diff >= 0, jnp.exp(log_g[:, None, None] * jnp.maximum(diff, 0.0)), 0.0)
    d_q = jnp.exp(log_g[:, None] * (t + 1.0))[:, :, None]
    d_k = jnp.exp(log_g[:, None] * (C - 1.0 - t))[:, :, None]
    d_c = jnp.exp(log_g * C)[:, None, None]

    row = lambda b, h, i: b * n + i
    kq = D // dk
    kv = 2 * D // dv
    kg = kv + H
    return pl.pallas_call(
        functools.partial(_ret_body, dk_dim=dk),
        out_shape=jax.ShapeDtypeStruct((T, H * dv), BF16),
        grid=(B, H, n),
        in_specs=[pl.BlockSpec((C, dk), lambda b, h, i: (row(b, h, i), h)),
                  pl.BlockSpec((C, dk), lambda b, h, i: (row(b, h, i), kq + h)),
                  pl.BlockSpec((C, dv), lambda b, h, i: (row(b, h, i), kv + h)),
                  pl.BlockSpec((C, dv), lambda b, h, i: (row(b, h, i), kg + h)),
                  pl.BlockSpec((C, dk // 2), lambda b, h, i: (row(b, h, i), 0)),
                  pl.BlockSpec((C, dk // 2), lambda b, h, i: (row(b, h, i), 0)),
                  pl.BlockSpec((1, C, C), lambda b, h, i: (h, 0, 0)),
                  pl.BlockSpec((1, C, 1), lambda b, h, i: (h, 0, 0)),
                  pl.BlockSpec((1, C, 1), lambda b, h, i: (h, 0, 0)),
                  pl.BlockSpec((1, 1, 1), lambda b, h, i: (h, 0, 0))],
        out_specs=pl.BlockSpec((C, dv), lambda b, h, i: (row(b, h, i), h)),
        scratch_shapes=[pltpu.VMEM((dk, dv), F32)],
        compiler_params=_cparams("parallel", "parallel", "arbitrary"),
        name="retention",
    )(proj, proj, proj, proj, cos, sin, d_intra, d_q, d_k, d_c)


def _route(logits_t, bias):
    sc = jax.nn.sigmoid(logits_t)
    bi = sc + bias
    s_rows = [sc[e:e + 1, :] for e in range(N_EXPERTS)]
    b_rows = [bi[e:e + 1, :] for e in range(N_EXPERTS)]

    def top2sum(a, b, c, d):
        p, q = jnp.maximum(a, b), jnp.minimum(a, b)
        r, s = jnp.maximum(c, d), jnp.minimum(c, d)
        return jnp.maximum(p, r) + jnp.maximum(jnp.minimum(p, r), jnp.maximum(q, s))

    n = EXPERTS_PER_GROUP
    gs = [top2sum(*b_rows[n * g:n * g + n]) for g in range(N_GROUPS)]
    best, gi = gs[0], jnp.zeros_like(gs[0], dtype=jnp.int32)
    for g in range(1, N_GROUPS):
        upd = gs[g] > best
        gi = jnp.where(upd, g, gi)
        best = jnp.where(upd, gs[g], best)

    def pick(rows, j):
        out = rows[j]
        for g in range(1, N_GROUPS):
            out = jnp.where(gi == g, rows[n * g + j], out)
        return out

    vb = [pick(b_rows, j) for j in range(n)]
    vs = [pick(s_rows, j) for j in range(n)]

    def argmax_first(vals):
        best, idx = vals[0], jnp.zeros_like(gi)
        for j in range(1, n):
            upd = vals[j] > best
            idx = jnp.where(upd, j, idx)
            best = jnp.where(upd, vals[j], best)
        return idx

    i1 = argmax_first(vb)
    i2 = argmax_first([jnp.where(i1 == j, -jnp.inf, vb[j]) for j in range(n)])

    def take(vals, idx):
        out = vals[0]
        for j in range(1, n):
            out = jnp.where(idx == j, vals[j], out)
        return out

    w1, w2 = take(vs, i1), take(vs, i2)
    tot = w1 + w2
    w1, w2 = w1 / tot, w2 / tot
    e1, e2 = gi * n + i1, gi * n + i2
    rows = [jnp.where(e1 == e, w1, 0.0) + jnp.where(e2 == e, w2, 0.0) for e in range(N_EXPERTS)]
    return jnp.concatenate(rows, axis=0)


def _outproj_body(y_ref, w_ref, h_ref, g1_ref, gn_ref, sh_ref, sc_ref, wr_ref, br_ref,
                  ho_ref, xn_ref, cb_ref):
    hn = h_ref[...] + g1_ref[0] * _dot(y_ref[...], w_ref[...])
    ho_ref[...] = hn
    xn = _norm_mod(hn, gn_ref[...], sh_ref[0], sc_ref[0])
    xn_ref[...] = xn.astype(BF16)
    logits_t = _dot_nt(wr_ref[...], xn, precision=lax.Precision.HIGHEST)
    comb_t = _route(logits_t, br_ref[...])
    tm = comb_t.shape[1]
    padded = jnp.concatenate([comb_t, jnp.zeros((LANES - N_EXPERTS, tm), F32)], axis=0)
    cb_ref[...] = padded.T


def _outproj_route(y, w, h, g1, gn, sh, sc, wr_t, br, S):
    T, D = h.shape
    K = y.shape[1]
    tm = min(TOK_TILE, S)
    per_b = S // tm
    vec = pl.BlockSpec((1, 1, D), lambda i: (i // per_b, 0, 0))
    tok = lambda n: pl.BlockSpec((tm, n), lambda i: (i, 0))
    return pl.pallas_call(
        _outproj_body,
        out_shape=(jax.ShapeDtypeStruct((T, D), F32),
                   jax.ShapeDtypeStruct((T, D), BF16),
                   jax.ShapeDtypeStruct((T, LANES), F32)),
        grid=(T // tm,),
        in_specs=[tok(K),
                  pl.BlockSpec((K, D), lambda i: (0, 0)),
                  tok(D), vec,
                  pl.BlockSpec((1, D), lambda i: (0, 0)),
                  vec, vec,
                  pl.BlockSpec((N_EXPERTS, D), lambda i: (0, 0)),
                  pl.BlockSpec((N_EXPERTS, 1), lambda i: (0, 0))],
        out_specs=(tok(D), tok(D), tok(LANES)),
        compiler_params=_cparams("parallel"),
        name="outproj_route",
    )(y, w, h, g1, gn, sh, sc, wr_t, br)


def _moe_body(x_ref, wgu_ref, wd_ref, cb_ref, h_ref, g2_ref, o_ref, acc_ref):
    e = pl.program_id(1)

    @pl.when(e == 0)
    def _():
        acc_ref[...] = jnp.zeros_like(acc_ref)

    hgu = _dot(x_ref[...], wgu_ref[0])
    f = hgu.shape[1] // 2
    hdn = (_silu(hgu[:, :f]) * hgu[:, f:]).astype(BF16)
    z = _dot(hdn, wd_ref[0])
    cb = cb_ref[...]
    lane = lax.broadcasted_iota(jnp.int32, cb.shape, 1)
    ce = jnp.sum(jnp.where(lane == e, cb, 0.0), axis=-1, keepdims=True)
    acc_ref[...] += ce * z

    @pl.when(e == pl.num_programs(1) - 1)
    def _():
        o_ref[...] = h_ref[...] + g2_ref[0] * acc_ref[...]


def _moe(xn, wgu, wd, comb, h, g2, S):
    T, D = h.shape
    E, _, F2 = wgu.shape
    tm = min(TOK_TILE, S)
    per_b = S // tm
    tok = lambda n: pl.BlockSpec((tm, n), lambda i, e: (i, 0))
    return pl.pallas_call(
        _moe_body,
        out_shape=jax.ShapeDtypeStruct((T, D), F32),
        grid=(T // tm, E),
        in_specs=[tok(D),
                  pl.BlockSpec((1, D, F2), lambda i, e: (e, 0, 0)),
                  pl.BlockSpec((1, F2 // 2, D), lambda i, e: (e, 0, 0)),
                  tok(LANES), tok(D),
                  pl.BlockSpec((1, 1, D), lambda i, e: (i // per_b, 0, 0))],
        out_specs=tok(D),
        scratch_shapes=[pltpu.VMEM((tm, D), F32)],
        compiler_params=_cparams("parallel", "arbitrary"),
        name="moe_dense",
    )(xn, wgu, wd, comb, h, g2)


def _kv_body(h_ref, g_ref, sh_ref, sc_ref, wa_ref, gkv_ref, wb_ref, cos_ref, sin_ref,
             k_ref, vt_ref):
    hn = _norm_mod(h_ref[...], g_ref[...], sh_ref[0], sc_ref[0]).astype(BF16)
    a = _dot(hn, wa_ref[...])
    c_kv = _rms(a[:, :KV_LORA], gkv_ref[...]).astype(BF16)
    kr = (a[:, KV_LORA:KV_LORA + LANES] * cos_ref[...]
          + a[:, KV_LORA + LANES:] * sin_ref[...]).astype(BF16)
    kv = _dot(c_kv, wb_ref[...])
    hk = MLA_HEADS * QK_NOPE
    w = QK_NOPE + LANES
    for hd in range(MLA_HEADS):
        k_ref[:, hd * w:hd * w + QK_NOPE] = kv[:, hd * QK_NOPE:(hd + 1) * QK_NOPE].astype(BF16)
        k_ref[:, hd * w + QK_NOPE:(hd + 1) * w] = kr
        vh = kv[:, hk + hd * V_HEAD:hk + (hd + 1) * V_HEAD]
        for g in range(vt_ref.shape[1]):
            vt_ref[hd, g] = vh[g * ATT_KEYS:(g + 1) * ATT_KEYS, :].T.astype(BF16)


def _mla_kv(h, g, sh, sc, wa, gkv, wb, cos, sin, S):
    T, D = h.shape
    tm = min(TOK_TILE, S)
    per_b = S // tm
    vec = pl.BlockSpec((1, 1, D), lambda i: (i // per_b, 0, 0))
    tok = lambda n: pl.BlockSpec((tm, n), lambda i: (i, 0))
    full = lambda a: pl.BlockSpec(a.shape, lambda i: (0, 0))
    kw = MLA_HEADS * (QK_NOPE + LANES)
    gk = tm // ATT_KEYS
    vt_shape = (MLA_HEADS, T // ATT_KEYS, V_HEAD, ATT_KEYS)
    return pl.pallas_call(
        _kv_body,
        out_shape=(jax.ShapeDtypeStruct((T, kw), BF16), jax.ShapeDtypeStruct(vt_shape, BF16)),
        grid=(T // tm,),
        in_specs=[tok(D), full(g), vec, vec, full(wa), full(gkv), full(wb), tok(LANES), tok(LANES)],
        out_specs=(tok(kw),
                   pl.BlockSpec((MLA_HEADS, gk, V_HEAD, ATT_KEYS), lambda i: (0, i, 0, 0))),
        compiler_params=_cparams("parallel"),
        name="mla_kv",
    )(h, g, sh, sc, wa, gkv, wb, cos, sin)


def _q_body(h_ref, g_ref, sh_ref, sc_ref, wa_ref, gq_ref, wb_ref, cos_ref, sin_ref, q_ref, *, scale):
    xn = _norm_mod(h_ref[...], g_ref[...], sh_ref[0], sc_ref[0]).astype(BF16)
    qa = _rms(_dot(xn, wa_ref[...]), gq_ref[...]).astype(BF16)
    cos = cos_ref[...]
    sin = sin_ref[...]
    wi = QK_NOPE + 2 * LANES
    wo = QK_NOPE + LANES
    for hd in range(MLA_HEADS):
        qb = _dot(qa, wb_ref[:, hd * wi:(hd + 1) * wi])
        q_ref[:, hd * wo:hd * wo + QK_NOPE] = (qb[:, :QK_NOPE] * scale).astype(BF16)
        rp = qb[:, QK_NOPE:QK_NOPE + LANES] * cos + qb[:, QK_NOPE + LANES:] * sin
        q_ref[:, hd * wo + QK_NOPE:(hd + 1) * wo] = (rp * scale).astype(BF16)


def _mla_q(h, g, sh, sc, wa, gq, wb, cos, sin, S):
    T, D = h.shape
    tm = min(TOK_TILE, S)
    per_b = S // tm
    vec = pl.BlockSpec((1, 1, D), lambda i: (i // per_b, 0, 0))
    tok = lambda n: pl.BlockSpec((tm, n), lambda i: (i, 0))
    full = lambda a: pl.BlockSpec(a.shape, lambda i: (0, 0))
    qw = MLA_HEADS * (QK_NOPE + LANES)
    return pl.pallas_call(
        functools.partial(_q_body, scale=(QK_NOPE + QK_ROPE) ** -0.5 * LOG2E),
        out_shape=jax.ShapeDtypeStruct((T, qw), BF16),
        grid=(T // tm,),
        in_specs=[tok(D), full(g), vec, vec, full(wa), full(gq), full(wb), tok(LANES), tok(LANES)],
        out_specs=tok(qw),
        compiler_params=_cparams("parallel"),
        name="mla_q",
    )(h, g, sh, sc, wa, gq, wb, cos, sin)


def _flash_body(q_ref, k_ref, vt_ref, o_ref, *, tq, ks):
    i = pl.program_id(2)
    q = q_ref[...]
    last = (i * tq + tq - 1) // ks

    def scores(c):
        kc = k_ref[pl.ds(pl.multiple_of(c * ks, ks), ks), :]
        s = _dot_nt(kc, q)
        return s, jnp.max(s, axis=0, keepdims=True)

    def update(c, s_and_max, carry):
        s, smax = s_and_max
        m, l, acc = carry
        m_new = jnp.maximum(m, smax)
        alpha = jnp.exp2(m - m_new)
        part = jnp.zeros((SUBLANES, tq), F32)
        ps = []
        for r in range(ks // ATT_SLAB):
            p = jnp.exp2(s[r * ATT_SLAB:(r + 1) * ATT_SLAB, :] - m_new)
            for r8 in range(ATT_SLAB // SUBLANES):
                part = part + p[r8 * SUBLANES:(r8 + 1) * SUBLANES, :]
            ps.append(p.astype(BF16))
        l = alpha * l + jnp.sum(part, axis=0, keepdims=True)
        acc = alpha * acc + _dot(vt_ref[0, c], jnp.concatenate(ps, axis=0))
        return m_new, l, acc

    def body(c, state):
        s, carry = state
        return scores(c + 1), update(c, s, carry)

    carry = (jnp.full((1, tq), NEG_BIG, F32), jnp.zeros((1, tq), F32), jnp.zeros((V_HEAD, tq), F32))
    (s, _), carry = lax.fori_loop(0, last, body, (scores(0), carry))
    kch = (last * ks + lax.broadcasted_iota(jnp.int32, s.shape, 0)) // CHUNK
    qch = (i * tq + lax.broadcasted_iota(jnp.int32, s.shape, 1)) // CHUNK
    s = jnp.where(kch <= qch, s, NEG_BIG)
    _, l, acc = update(last, (s, jnp.max(s, axis=0, keepdims=True)), carry)
    o_ref[...] = (acc / l).T.astype(o_ref.dtype)


def _flash(q, k, vt, S):
    T = q.shape[0]
    B = T // S
    H = MLA_HEADS
    tq = min(ATT_TILE, S)
    ks = ATT_KEYS
    nq = S // tq
    dq = QK_NOPE + LANES
    return pl.pallas_call(
        functools.partial(_flash_body, tq=tq, ks=ks),
        out_shape=jax.ShapeDtypeStruct((T, H * V_HEAD), BF16),
        grid=(B, H, nq),
        in_specs=[pl.BlockSpec((tq, dq), lambda b, h, i: (b * nq + i, h)),
                  pl.BlockSpec((S, dq), lambda b, h, i: (b, h)),
                  pl.BlockSpec((1, S // ks, V_HEAD, ks), lambda b, h, i: (h, b, 0, 0))],
        out_specs=pl.BlockSpec((tq, V_HEAD), lambda b, h, i: (b * nq + i, h)),
        compiler_params=_cparams("parallel", "parallel", "arbitrary"),
        name="flash_attn",
    )(q, k, vt)


def _final_body(h_ref, g_ref, o_ref):
    o_ref[...] = _rms(h_ref[...], g_ref[...])


def _final_norm(h, g):
    T, D = h.shape
    tm = min(T, 1024)
    return pl.pallas_call(
        _final_body,
        out_shape=jax.ShapeDtypeStruct((T, D), F32),
        grid=(T // tm,),
        in_specs=[pl.BlockSpec((tm, D), lambda i: (i, 0)), pl.BlockSpec((1, D), lambda i: (0, 0))],
        out_specs=pl.BlockSpec((tm, D), lambda i: (i, 0)),
        compiler_params=_cparams("parallel"),
        name="final_norm",
    )(h, g)


def _rope_swap(w):
    half = w.shape[-1] // 2
    return jnp.concatenate([w[..., half:], w[..., :half]], axis=-1)


def _pad_lanes(w):
    return jnp.pad(w, [(0, 0)] * (w.ndim - 1) + [(0, LANES - w.shape[-1])])


def kernel(x, c, positions, w_mod, b_mod, norm_mix, norm_ffn, ret_w_in, ret_w_out, w_mod_kv, b_mod_kv, norm_kv, mla_w_kv_a, mla_kv_norm, mla_w_kv_b, mla_w_q_a, mla_q_norm, mla_w_q_b, mla_w_o, router_w, router_b, moe_w_gate, moe_w_up, moe_w_down, final_norm):
    B, S, D = x.shape
    T = B * S
    depth = w_mod.shape[0]
    n_a = ret_w_in.shape[0]

    c8 = jnp.pad(c, ((0, 8 - B), (0, 0)))
    mod = _mod_vectors(c8, w_mod, b_mod)[:, :B]
    mod = mod.reshape(depth, B, 6, 1, D)
    kv_mod = _mod_vectors(c8, w_mod_kv[None], b_mod_kv[None])[0, :B].reshape(B, 2, 1, D)

    pos_col = positions.reshape(T, 1)
    ones = jnp.ones((1, LANES), F32)
    inv_ret = (ROPE_THETA ** (-jnp.arange(LANES, dtype=F32) / LANES)).reshape(1, LANES)
    cos_r, sin_r = _rope_tables(pos_col, inv_ret, ones, ones)
    hr = QK_ROPE // 2
    inv_m = ROPE_THETA ** (-jnp.arange(hr, dtype=F32) / hr)
    inv_m = _pad_lanes(jnp.concatenate([inv_m, inv_m])[None])
    cm = _pad_lanes(jnp.ones((1, QK_ROPE), F32))
    sm = _pad_lanes(jnp.concatenate([-jnp.ones((1, hr), F32), jnp.ones((1, hr), F32)], axis=-1))
    cos_m, sin_m = _rope_tables(pos_col, inv_m, cm, sm)

    wr_t = router_w.T
    br = router_b.reshape(N_EXPERTS, 1)
    wgu = jnp.concatenate([moe_w_gate, moe_w_up], axis=-1).astype(BF16)
    wdn = moe_w_down.astype(BF16)

    h = x.reshape(T, D)
    k_full = v_full = None
    for layer in range(depth):
        sh1, sc1, g1, sh2, sc2, g2 = (mod[layer, :, i] for i in range(6))
        gmix = norm_mix[layer].reshape(1, D)
        if layer < n_a:
            proj = _ret_inproj(h, gmix, sh1, sc1, ret_w_in[layer].astype(BF16), S)
            mix = _retention(proj, cos_r, sin_r, S, D)
            w_o = ret_w_out[layer].astype(BF16)
        else:
            if layer == n_a:
                wa = mla_w_kv_a
                wa_r = wa[:, KV_LORA:]
                wa_p = jnp.concatenate([wa[:, :KV_LORA], _pad_lanes(wa_r), _pad_lanes(_rope_swap(wa_r))],
                                       axis=-1).astype(BF16)
                wb = mla_w_kv_b.reshape(KV_LORA, MLA_HEADS, QK_NOPE + V_HEAD)
                wb_p = jnp.concatenate([wb[..., :QK_NOPE].reshape(KV_LORA, -1),
                                        wb[..., QK_NOPE:].reshape(KV_LORA, -1)], axis=-1).astype(BF16)
                k_full, v_full = _mla_kv(h, norm_kv.reshape(1, D), kv_mod[:, 0], kv_mod[:, 1], wa_p,
                                         mla_kv_norm.reshape(1, KV_LORA), wb_p, cos_m, sin_m, S)
            j = layer - n_a
            wq = mla_w_q_b[j].reshape(Q_LORA, MLA_HEADS, QK_NOPE + QK_ROPE)
            wq_r = wq[..., QK_NOPE:]
            wq_p = jnp.concatenate([wq[..., :QK_NOPE], _pad_lanes(wq_r), _pad_lanes(_rope_swap(wq_r))],
                                   axis=-1).reshape(Q_LORA, -1).astype(BF16)
            q_full = _mla_q(h, gmix, sh1, sc1, mla_w_q_a[j].astype(BF16),
                            mla_q_norm[j].reshape(1, Q_LORA), wq_p, cos_m, sin_m, S)
            mix = _flash(q_full, k_full, v_full, S)
            w_o = mla_w_o[j].astype(BF16)
        h, xn, comb = _outproj_route(mix, w_o, h, g1, norm_ffn[layer].reshape(1, D), sh2, sc2,
                                     wr_t, br, S)
        h = _moe(xn, wgu[layer], wdn[layer], comb, h, g2, S)
    return _final_norm(h, final_norm.reshape(1, D)).reshape(B, S, D)
```

```python
import functools

import jax
import jax.numpy as jnp
import numpy as np
from jax import lax
from jax.experimental import pallas as pl
from jax.experimental.pallas import tpu as pltpu

F32 = jnp.float32
BF16 = jnp.bfloat16

CHUNK = 64
RET_HEADS = 4
MLA_HEADS = 8
QK_NOPE = 128
QK_ROPE = 64
V_HEAD = 128
Q_LORA = 256
KV_LORA = 128
N_EXPERTS = 16
N_GROUPS = 4
EXPERTS_PER_GROUP = N_EXPERTS // N_GROUPS
D_EXPERT = 512
ROPE_THETA = 10000.0
EPS = 1e-6

LANES = 128
VMEM_LIMIT = 56 * 1024 * 1024
NEG_BIG = -1e30
LOG2E = 1.4426950408889634

RET_CHUNK = 256
TOK_TILE = 512
ATT_TILE = 512
ATT_KEYS = 512
ATT_SLAB = 32
SUBLANES = 8
MXU_DIM = 256
MOE_GROUP = 16
MOE_ROW_TILE = 512


def _cparams(*sem):
    return pltpu.CompilerParams(dimension_semantics=sem, vmem_limit_bytes=VMEM_LIMIT)


def _silu(x):
    return x * jax.nn.sigmoid(x)


def _rms(x, g):
    return x * lax.rsqrt(jnp.mean(x * x, axis=-1, keepdims=True) + EPS) * g


def _norm_mod(h, g, shift, scale):
    return _rms(h, g) * (1.0 + scale) + shift


def _dot(a, b):
    return jnp.dot(a, b, preferred_element_type=F32)


def _dot_nt(a, b, **kw):
    return lax.dot_general(a, b, (((1,), (1,)), ((), ())), preferred_element_type=F32, **kw)


def _dot_tn(a, b):
    return lax.dot_general(a, b, (((0,), (0,)), ((), ())), preferred_element_type=F32)


def _mod_body(c_ref, w_ref, b_ref, o_ref):
    ca = _silu(c_ref[...])
    o_ref[0] = jnp.dot(ca, w_ref[0], preferred_element_type=F32,
                       precision=lax.Precision.HIGHEST) + b_ref[0]


def _mod_vectors(c8, w, b):
    L, D, N = w.shape
    tn = D
    assert N % tn == 0
    return pl.pallas_call(
        _mod_body,
        out_shape=jax.ShapeDtypeStruct((L, 8, N), F32),
        grid=(L, N // tn),
        in_specs=[pl.BlockSpec((8, D), lambda l, j: (0, 0)),
                  pl.BlockSpec((1, D, tn), lambda l, j: (l, 0, j)),
                  pl.BlockSpec((1, 1, tn), lambda l, j: (l, 0, j))],
        out_specs=pl.BlockSpec((1, 8, tn), lambda l, j: (l, 0, j)),
        compiler_params=_cparams("parallel", "parallel"),
        name="mod_vectors",
    )(c8, w, b.reshape(L, 1, N))


def _rope_body(pos_ref, inv_ref, cm_ref, sm_ref, cos_ref, sin_ref):
    ang = pos_ref[...].astype(F32) * inv_ref[...]
    cos_ref[...] = jnp.cos(ang) * cm_ref[...]
    sin_ref[...] = jnp.sin(ang) * sm_ref[...]


def _rope_tables(pos_col, inv, cm, sm):
    T = pos_col.shape[0]
    tm = min(T, 1024)
    row = pl.BlockSpec((1, LANES), lambda i: (0, 0))
    return pl.pallas_call(
        _rope_body,
        out_shape=(jax.ShapeDtypeStruct((T, LANES), F32),) * 2,
        grid=(T // tm,),
        in_specs=[pl.BlockSpec((tm, 1), lambda i: (i, 0)), row, row, row],
        out_specs=(pl.BlockSpec((tm, LANES), lambda i: (i, 0)),) * 2,
        compiler_params=_cparams("parallel"),
        name="rope_tables",
    )(pos_col, inv, cm, sm)


def _inproj_body(h_ref, g_ref, sh_ref, sc_ref, w_ref, o_ref, *, tn):
    xn = _norm_mod(h_ref[...], g_ref[...], sh_ref[0], sc_ref[0]).astype(BF16)
    for j in range(w_ref.shape[1] // tn):
        o_ref[:, j * tn:(j + 1) * tn] = _dot(xn, w_ref[:, j * tn:(j + 1) * tn]).astype(BF16)


def _ret_inproj(h, g, sh, sc, w, S):
    T, D = h.shape
    N = w.shape[1]
    tm = min(TOK_TILE, S)
    per_b = S // tm
    vec = pl.BlockSpec((1, 1, D), lambda i: (i // per_b, 0, 0))
    return pl.pallas_call(
        functools.partial(_inproj_body, tn=512),
        out_shape=jax.ShapeDtypeStruct((T, N), BF16),
        grid=(T // tm,),
        in_specs=[pl.BlockSpec((tm, D), lambda i: (i, 0)),
                  pl.BlockSpec((1, D), lambda i: (0, 0)),
                  vec, vec,
                  pl.BlockSpec((D, N), lambda i: (0, 0))],
        out_specs=pl.BlockSpec((tm, N), lambda i: (i, 0)),
        compiler_params=_cparams("parallel"),
        name="ret_inproj",
    )(h, g, sh, sc, w)


def _ret_body(q_ref, k_ref, v_ref, g_ref, cos_ref, sin_ref, di_ref, dq_ref, dk_ref, dc_ref,
              y_ref, state_ref, *, dk_dim):
    @pl.when(pl.program_id(2) == 0)
    def _():
        state_ref[...] = jnp.zeros_like(state_ref)

    cos = cos_ref[...]
    sin = sin_ref[...]
    half = dk_dim // 2

    def rope(x):
        x1, x2 = x[:, :half], x[:, half:]
        return jnp.concatenate([x1 * cos - x2 * sin, x1 * sin + x2 * cos], axis=-1)

    qr = rope(q_ref[...].astype(F32))
    kr = rope(k_ref[...].astype(F32)) * (dk_dim ** -0.5)
    qb = qr.astype(BF16)
    v = v_ref[...]
    inner = (_dot_nt(qb, kr.astype(BF16)) * di_ref[0]).astype(BF16)
    st = state_ref[...]
    out = _dot(inner, v) + _dot(qb, st.astype(BF16)) * dq_ref[0]
    kd = (kr * dk_ref[0]).astype(BF16)
    state_ref[...] = st * dc_ref[0] + _dot_tn(kd, v)

    mu = jnp.mean(out, axis=-1, keepdims=True)
    cen = out - mu
    var = jnp.mean(cen * cen, axis=-1, keepdims=True)
    o = cen * lax.rsqrt(var + EPS)
    y_ref[...] = (_silu(g_ref[...].astype(F32)) * o).astype(BF16)


def _retention(proj, cos, sin, S, D):
    T = proj.shape[0]
    B = T // S
    H = RET_HEADS
    dk = D // H
    dv = 2 * dk
    C = min(RET_CHUNK, S)
    n = S // C
    log_g = jnp.log1p(-(2.0 ** (-5.0 - jnp.arange(H, dtype=F32))))
    t = jnp.arange(C, dtype=F32)
    diff = t[:, None] - t[None, :]
    d_intra = jnp.where(diff >= 0, jnp.exp(log_g[:, None, None] * jnp.maximum(diff, 0.0)), 0.0)
    d_q = jnp.exp(log_g[:, None] * (t + 1.0))[:, :, None]
    d_k = jnp.exp(log_g[:, None] * (C - 1.0 - t))[:, :, None]
    d_c = jnp.exp(log_g * C)[:, None, None]

    row = lambda b, h, i: b * n + i
    kq = D // dk
    kv = 2 * D // dv
    kg = kv + H
    return pl.pallas_call(
        functools.partial(_ret_body, dk_dim=dk),
        out_shape=jax.ShapeDtypeStruct((T, H * dv), BF16),
        grid=(B, H, n),
        in_specs=[pl.BlockSpec((C, dk), lambda b, h, i: (row(b, h, i), h)),
                  pl.BlockSpec((C, dk), lambda b, h, i: (row(b, h, i), kq + h)),
                  pl.BlockSpec((C, dv), lambda b, h, i: (row(b, h, i), kv + h)),
                  pl.BlockSpec((C, dv), lambda b, h, i: (row(b, h, i), kg + h)),
                  pl.BlockSpec((C, dk // 2), lambda b, h, i: (row(b, h, i), 0)),
                  pl.BlockSpec((C, dk // 2), lambda b, h, i: (row(b, h, i), 0)),
                  pl.BlockSpec((1, C, C), lambda b, h, i: (h, 0, 0)),
                  pl.BlockSpec((1, C, 1), lambda b, h, i: (h, 0, 0)),
                  pl.BlockSpec((1, C, 1), lambda b, h, i: (h, 0, 0)),
                  pl.BlockSpec((1, 1, 1), lambda b, h, i: (h, 0, 0))],
        out_specs=pl.BlockSpec((C, dv), lambda b, h, i: (row(b, h, i), h)),
        scratch_shapes=[pltpu.VMEM((dk, dv), F32)],
        compiler_params=_cparams("parallel", "parallel", "arbitrary"),
        name="retention",
    )(proj, proj, proj, proj, cos, sin, d_intra, d_q, d_k, d_c)


def _route(logits_t, bias):
    sc = jax.nn.sigmoid(logits_t)
    bi = sc + bias
    s_rows = [sc[e:e + 1, :] for e in range(N_EXPERTS)]
    b_rows = [bi[e:e + 1, :] for e in range(N_EXPERTS)]

    def top2sum(a, b, c, d):
        p, q = jnp.maximum(a, b), jnp.minimum(a, b)
        r, s = jnp.maximum(c, d), jnp.minimum(c, d)
        return jnp.maximum(p, r) + jnp.maximum(jnp.minimum(p, r), jnp.maximum(q, s))

    n = EXPERTS_PER_GROUP
    gs = [top2sum(*b_rows[n * g:n * g + n]) for g in range(N_GROUPS)]
    best, gi = gs[0], jnp.zeros_like(gs[0], dtype=jnp.int32)
    for g in range(1, N_GROUPS):
        upd = gs[g] > best
        gi = jnp.where(upd, g, gi)
        best = jnp.where(upd, gs[g], best)

    def pick(rows, j):
        out = rows[j]
        for g in range(1, N_GROUPS):
            out = jnp.where(gi == g, rows[n * g + j], out)
        return out

    vb = [pick(b_rows, j) for j in range(n)]
    vs = [pick(s_rows, j) for j in range(n)]

    def argmax_first(vals):
        best, idx = vals[0], jnp.zeros_like(gi)
        for j in range(1, n):
            upd = vals[j] > best
            idx = jnp.where(upd, j, idx)
            best = jnp.where(upd, vals[j], best)
        return idx

    i1 = argmax_first(vb)
    i2 = argmax_first([jnp.where(i1 == j, -jnp.inf, vb[j]) for j in range(n)])

    def take(vals, idx):
        out = vals[0]
        for j in range(1, n):
            out = jnp.where(idx == j, vals[j], out)
        return out

    w1, w2 = take(vs, i1), take(vs, i2)
    tot = w1 + w2
    w1, w2 = w1 / tot, w2 / tot
    return gi * n + i1, gi * n + i2, w1, w2


class _MoeLayout:
    def __init__(self, T, tm):
        self.tm = tm
        self.n_tiles = T // tm
        self.group = MOE_GROUP
        self.rt = -(-(2 * tm + N_EXPERTS * (MOE_GROUP - 1)) // MXU_DIM) * MXU_DIM
        self.ng = self.rt // MOE_GROUP
        assert self.ng < LANES
        self.tmx = MOE_ROW_TILE
        self.cap = -(-(T + self.n_tiles * MOE_GROUP) // self.tmx) * self.tmx
        self.dump = N_EXPERTS * self.cap
        self.rows = self.dump + self.rt
        pad = self.n_tiles * N_EXPERTS * (MOE_GROUP - 1)
        self.nt = (2 * T + pad) // self.tmx + N_EXPERTS


def _dispatch_meta(e1, e2, cum, tri, lay):
    E, G = N_EXPERTS, lay.group
    tm = e1.shape[1]
    eid = lax.broadcasted_iota(jnp.int32, (E, tm), 0)
    oh1, oh2 = eid == e1, eid == e2
    cnt = jnp.where(oh1 | oh2, 1.0, 0.0)
    pre = _dot(cnt.astype(BF16), tri)
    tot = jnp.sum(cnt, axis=1, keepdims=True)
    ptot = jnp.broadcast_to(jnp.ceil(tot * (1.0 / G)) * G, (E, LANES))
    below = jnp.where(lax.broadcasted_iota(jnp.int32, (E, E), 0) > lax.broadcasted_iota(jnp.int32, (E, E), 1),
                      1.0, 0.0)
    loff = jnp.dot(below, ptot, preferred_element_type=F32, precision=lax.Precision.HIGHEST)
    pos_e = loff[:, :1] + pre
    pos1 = jnp.sum(jnp.where(oh1, pos_e, 0.0), axis=0, keepdims=True)
    pos2 = jnp.sum(jnp.where(oh2, pos_e, 0.0), axis=0, keepdims=True)

    lane = lax.broadcasted_iota(jnp.int32, (E, LANES), 1)
    g_row = (lane * G).astype(F32)
    eg = jnp.sum(jnp.where(loff + ptot <= g_row, 1, 0), axis=0, keepdims=True)
    erow = lax.broadcasted_iota(jnp.int32, (E, LANES), 0)
    base = erow.astype(F32) * float(lay.cap) + cum - loff
    sel = jnp.sum(jnp.where(erow == eg, base, 0.0), axis=0, keepdims=True)
    dst = jnp.where(eg < E, g_row[:1] + sel, float(lay.dump) + g_row[:1])
    n_used = jnp.sum(ptot[:, :1], axis=0, keepdims=True) * (1.0 / G)
    table = jnp.where(lane[:1] == LANES - 1, n_used, dst).astype(jnp.int32)
    return pos1, pos2, table, cum + ptot


def _outproj_body(y_ref, w_ref, h_ref, g1_ref, gn_ref, sh_ref, sc_ref, wr_ref, br_ref, tri_ref,
                  ho_ref, xn_ref, rows_ref, cols_ref, tbl_ref, cum_ref, *, lay):
    @pl.when(pl.program_id(0) == 0)
    def _():
        cum_ref[...] = jnp.zeros_like(cum_ref)

    hn = h_ref[...] + g1_ref[0] * _dot(y_ref[...], w_ref[...])
    ho_ref[...] = hn
    xn = _norm_mod(hn, gn_ref[...], sh_ref[0], sc_ref[0])
    xn_ref[...] = xn.astype(BF16)
    logits_t = _dot_nt(wr_ref[...], xn, precision=lax.Precision.HIGHEST)
    e1, e2, w1, w2 = _route(logits_t, br_ref[...])
    pos1, pos2, table, cum = _dispatch_meta(e1, e2, cum_ref[...], tri_ref[...], lay)
    cum_ref[...] = cum
    tbl_ref[0] = table
    tm = e1.shape[1]
    rows = jnp.concatenate([pos1, pos2, w1, w2, jnp.zeros((SUBLANES - 4, tm), F32)], axis=0)
    rows_ref[...] = rows
    cols_ref[...] = jnp.concatenate([rows, jnp.zeros((LANES - SUBLANES, tm), F32)], axis=0).T


def _outproj_route(y, w, h, g1, gn, sh, sc, wr_t, br, S, lay):
    T, D = h.shape
    K = y.shape[1]
    tm = lay.tm
    per_b = S // tm
    vec = pl.BlockSpec((1, 1, D), lambda i: (i // per_b, 0, 0))
    tok = lambda n: pl.BlockSpec((tm, n), lambda i: (i, 0))
    tri = jnp.triu(jnp.ones((tm, tm), BF16), k=1)
    return pl.pallas_call(
        functools.partial(_outproj_body, lay=lay),
        out_shape=(jax.ShapeDtypeStruct((T, D), F32),
                   jax.ShapeDtypeStruct((T, D), BF16),
                   jax.ShapeDtypeStruct((SUBLANES, T), F32),
                   jax.ShapeDtypeStruct((T, LANES), F32),
                   jax.ShapeDtypeStruct((lay.n_tiles, 1, LANES), jnp.int32),
                   jax.ShapeDtypeStruct((N_EXPERTS, LANES), F32)),
        grid=(T // tm,),
        in_specs=[tok(K),
                  pl.BlockSpec((K, D), lambda i: (0, 0)),
                  tok(D), vec,
                  pl.BlockSpec((1, D), lambda i: (0, 0)),
                  vec, vec,
                  pl.BlockSpec((N_EXPERTS, D), lambda i: (0, 0)),
                  pl.BlockSpec((N_EXPERTS, 1), lambda i: (0, 0)),
                  pl.BlockSpec((tm, tm), lambda i: (0, 0))],
        out_specs=(tok(D), tok(D),
                   pl.BlockSpec((SUBLANES, tm), lambda i: (0, i)),
                   tok(LANES),
                   pl.BlockSpec((1, 1, LANES), lambda i: (i, 0, 0)),
                   pl.BlockSpec((N_EXPERTS, LANES), lambda i: (0, 0))),
        compiler_params=_cparams("arbitrary"),
        name="outproj_route",
    )(y, w, h, g1, gn, sh, sc, wr_t, br, tri)


def _group_copy(hbm_ref, buf_ref, tbl_ref, i, g, sem, lay, to_hbm):
    G = lay.group
    hbm = hbm_ref.at[pl.ds(pl.multiple_of(tbl_ref[i * LANES + g], G), G), :]
    vmem = buf_ref.at[pl.ds(pl.multiple_of(g * G, G), G), :]
    return pltpu.make_async_copy(vmem, hbm, sem) if to_hbm else pltpu.make_async_copy(hbm, vmem, sem)


def _dispatch_body(tbl_ref, x_ref, rows_ref, xs_ref, buf_ref, sem, *, lay):
    i = pl.program_id(0)
    pos1 = rows_ref[0:1, :].astype(jnp.int32)
    pos2 = rows_ref[1:2, :].astype(jnp.int32)
    r = lax.broadcasted_iota(jnp.int32, (lay.rt, lay.tm), 0)
    perm = jnp.where((r == pos1) | (r == pos2), 1.0, 0.0).astype(BF16)
    buf_ref[...] = _dot(perm, x_ref[...]).astype(BF16)
    for g in range(lay.ng):
        _group_copy(xs_ref, buf_ref, tbl_ref, i, g, sem, lay, True).start()
    for g in range(lay.ng):
        _group_copy(xs_ref, buf_ref, tbl_ref, i, g, sem, lay, True).wait()


def _moe_dispatch(xn, rows, tbl, lay):
    T, D = xn.shape
    tm = lay.tm
    grid_spec = pltpu.PrefetchScalarGridSpec(
        num_scalar_prefetch=1,
        grid=(lay.n_tiles,),
        in_specs=[pl.BlockSpec((tm, D), lambda i, tbl: (i, 0)),
                  pl.BlockSpec((SUBLANES, tm), lambda i, tbl: (0, i))],
        out_specs=pl.BlockSpec(memory_space=pl.ANY),
        scratch_shapes=[pltpu.VMEM((lay.rt, D), BF16), pltpu.SemaphoreType.DMA(())],
    )
    return pl.pallas_call(
        functools.partial(_dispatch_body, lay=lay),
        out_shape=jax.ShapeDtypeStruct((lay.rows, D), BF16),
        grid_spec=grid_spec,
        compiler_params=_cparams("arbitrary"),
        name="moe_dispatch",
    )(tbl, xn, rows)


def _expert_body(te_ref, tb_ref, nv_ref, x_ref, wgu_ref, wd_ref, y_ref):
    @pl.when(pl.program_id(0) < nv_ref[0])
    def _():
        hgu = _dot(x_ref[...], wgu_ref[0])
        f = hgu.shape[1] // 2
        hdn = (_silu(hgu[:, :f]) * hgu[:, f:]).astype(BF16)
        y_ref[...] = _dot(hdn, wd_ref[0]).astype(BF16)


def _moe_experts(xs, wgu, wd, tile_e, tile_blk, n_valid, lay):
    R, D = xs.shape
    E, _, F2 = wgu.shape
    tmx = lay.tmx
    grid_spec = pltpu.PrefetchScalarGridSpec(
        num_scalar_prefetch=3,
        grid=(lay.nt,),
        in_specs=[pl.BlockSpec((tmx, D), lambda n, te, tb, nv: (tb[n], 0)),
                  pl.BlockSpec((1, D, F2), lambda n, te, tb, nv: (te[n], 0, 0)),
                  pl.BlockSpec((1, F2 // 2, D), lambda n, te, tb, nv: (te[n], 0, 0))],
        out_specs=pl.BlockSpec((tmx, D), lambda n, te, tb, nv: (tb[n], 0)),
    )
    return pl.pallas_call(
        _expert_body,
        out_shape=jax.ShapeDtypeStruct((R, D), BF16),
        grid_spec=grid_spec,
        compiler_params=_cparams("arbitrary"),
        name="moe_experts",
    )(tile_e, tile_blk, n_valid, xs, wgu, wd)


def _combine_body(tbl_ref, cols_ref, h_ref, g2_ref, ys_ref, o_ref, buf_ref, sem, *, lay):
    i = pl.program_id(0)

    @pl.when(i == 0)
    def _():
        buf_ref[...] = jnp.zeros_like(buf_ref)

    n_used = tbl_ref[i * LANES + LANES - 1]

    def start(g, _):
        _group_copy(ys_ref, buf_ref, tbl_ref, i, g, sem, lay, False).start()
        return 0

    def wait(g, _):
        _group_copy(ys_ref, buf_ref, tbl_ref, i, g, sem, lay, False).wait()
        return 0

    lax.fori_loop(0, n_used, start, 0)
    lax.fori_loop(0, n_used, wait, 0)

    cols = cols_ref[...]
    r = lax.broadcasted_iota(jnp.int32, (lay.tm, lay.rt), 1)
    ys = buf_ref[...]
    y1 = _dot(jnp.where(r == cols[:, 0:1].astype(jnp.int32), 1.0, 0.0).astype(BF16), ys)
    y2 = _dot(jnp.where(r == cols[:, 1:2].astype(jnp.int32), 1.0, 0.0).astype(BF16), ys)
    o_ref[...] = h_ref[...] + g2_ref[0] * (cols[:, 2:3] * y1 + cols[:, 3:4] * y2)


def _moe_combine(ys, cols, tbl, h, g2, S, lay):
    T, D = h.shape
    tm = lay.tm
    per_b = S // tm
    grid_spec = pltpu.PrefetchScalarGridSpec(
        num_scalar_prefetch=1,
        grid=(lay.n_tiles,),
        in_specs=[pl.BlockSpec((tm, LANES), lambda i, tbl: (i, 0)),
                  pl.BlockSpec((tm, D), lambda i, tbl: (i, 0)),
                  pl.BlockSpec((1, 1, D), lambda i, tbl: (i // per_b, 0, 0)),
                  pl.BlockSpec(memory_space=pl.ANY)],
        out_specs=pl.BlockSpec((tm, D), lambda i, tbl: (i, 0)),
        scratch_shapes=[pltpu.VMEM((lay.rt, D), BF16), pltpu.SemaphoreType.DMA(())],
    )
    return pl.pallas_call(
        functools.partial(_combine_body, lay=lay),
        out_shape=jax.ShapeDtypeStruct((T, D), F32),
        grid_spec=grid_spec,
        compiler_params=_cparams("arbitrary"),
        name="moe_combine",
    )(tbl, cols, h, g2, ys)


def _expert_tiles(used, lay):
    tiles = jnp.ceil(used / lay.tmx).astype(jnp.int32)
    ends = jnp.cumsum(tiles)
    n_valid = ends[-1]
    n = jnp.minimum(jnp.arange(lay.nt, dtype=jnp.int32), n_valid - 1)
    e = jnp.searchsorted(ends, n, side="right").astype(jnp.int32)
    blk = e * (lay.cap // lay.tmx) + n - (ends - tiles)[e]
    return e, blk, n_valid.reshape(1)


def _moe(xn, rows, cols, tbl, used, wgu, wd, h, g2, S, lay):
    tbl = tbl.reshape(-1)
    xs = _moe_dispatch(xn, rows, tbl, lay)
    ys = _moe_experts(xs, wgu, wd, *_expert_tiles(used, lay), lay)
    return _moe_combine(ys, cols, tbl, h, g2, S, lay)


def _kv_body(h_ref, g_ref, sh_ref, sc_ref, wa_ref, gkv_ref, wb_ref, cos_ref, sin_ref,
             k_ref, vt_ref):
    hn = _norm_mod(h_ref[...], g_ref[...], sh_ref[0], sc_ref[0]).astype(BF16)
    a = _dot(hn, wa_ref[...])
    c_kv = _rms(a[:, :KV_LORA], gkv_ref[...]).astype(BF16)
    kr = (a[:, KV_LORA:KV_LORA + LANES] * cos_ref[...]
          + a[:, KV_LORA + LANES:] * sin_ref[...]).astype(BF16)
    kv = _dot(c_kv, wb_ref[...])
    hk = MLA_HEADS * QK_NOPE
    w = QK_NOPE + LANES
    for hd in range(MLA_HEADS):
        k_ref[:, hd * w:hd * w + QK_NOPE] = kv[:, hd * QK_NOPE:(hd + 1) * QK_NOPE].astype(BF16)
        k_ref[:, hd * w + QK_NOPE:(hd + 1) * w] = kr
        vh = kv[:, hk + hd * V_HEAD:hk + (hd + 1) * V_HEAD]
        for g in range(vt_ref.shape[1]):
            vt_ref[hd, g] = vh[g * ATT_KEYS:(g + 1) * ATT_KEYS, :].T.astype(BF16)


def _mla_kv(h, g, sh, sc, wa, gkv, wb, cos, sin, S):
    T, D = h.shape
    tm = min(TOK_TILE, S)
    per_b = S // tm
    vec = pl.BlockSpec((1, 1, D), lambda i: (i // per_b, 0, 0))
    tok = lambda n: pl.BlockSpec((tm, n), lambda i: (i, 0))
    full = lambda a: pl.BlockSpec(a.shape, lambda i: (0, 0))
    kw = MLA_HEADS * (QK_NOPE + LANES)
    gk = tm // ATT_KEYS
    vt_shape = (MLA_HEADS, T // ATT_KEYS, V_HEAD, ATT_KEYS)
    return pl.pallas_call(
        _kv_body,
        out_shape=(jax.ShapeDtypeStruct((T, kw), BF16), jax.ShapeDtypeStruct(vt_shape, BF16)),
        grid=(T // tm,),
        in_specs=[tok(D), full(g), vec, vec, full(wa), full(gkv), full(wb), tok(LANES), tok(LANES)],
        out_specs=(tok(kw),
                   pl.BlockSpec((MLA_HEADS, gk, V_HEAD, ATT_KEYS), lambda i: (0, i, 0, 0))),
        compiler_params=_cparams("parallel"),
        name="mla_kv",
    )(h, g, sh, sc, wa, gkv, wb, cos, sin)


def _q_body(h_ref, g_ref, sh_ref, sc_ref, wa_ref, gq_ref, wb_ref, cos_ref, sin_ref, q_ref, *, scale):
    xn = _norm_mod(h_ref[...], g_ref[...], sh_ref[0], sc_ref[0]).astype(BF16)
    qa = _rms(_dot(xn, wa_ref[...]), gq_ref[...]).astype(BF16)
    cos = cos_ref[...]
    sin = sin_ref[...]
    wi = QK_NOPE + 2 * LANES
    wo = QK_NOPE + LANES
    for hd in range(MLA_HEADS):
        qb = _dot(qa, wb_ref[:, hd * wi:(hd + 1) * wi])
        q_ref[:, hd * wo:hd * wo + QK_NOPE] = (qb[:, :QK_NOPE] * scale).astype(BF16)
        rp = qb[:, QK_NOPE:QK_NOPE + LANES] * cos + qb[:, QK_NOPE + LANES:] * sin
        q_ref[:, hd * wo + QK_NOPE:(hd + 1) * wo] = (rp * scale).astype(BF16)


def _mla_q(h, g, sh, sc, wa, gq, wb, cos, sin, S):
    T, D = h.shape
    tm = min(TOK_TILE, S)
    per_b = S // tm
    vec = pl.BlockSpec((1, 1, D), lambda i: (i // per_b, 0, 0))
    tok = lambda n: pl.BlockSpec((tm, n), lambda i: (i, 0))
    full = lambda a: pl.BlockSpec(a.shape, lambda i: (0, 0))
    qw = MLA_HEADS * (QK_NOPE + LANES)
    return pl.pallas_call(
        functools.partial(_q_body, scale=(QK_NOPE + QK_ROPE) ** -0.5 * LOG2E),
        out_shape=jax.ShapeDtypeStruct((T, qw), BF16),
        grid=(T // tm,),
        in_specs=[tok(D), full(g), vec, vec, full(wa), full(gq), full(wb), tok(LANES), tok(LANES)],
        out_specs=tok(qw),
        compiler_params=_cparams("parallel"),
        name="mla_q",
    )(h, g, sh, sc, wa, gq, wb, cos, sin)


def _flash_body(q_ref, k_ref, vt_ref, o_ref, *, tq, ks):
    i = pl.program_id(2)
    q = q_ref[...]
    last = (i * tq + tq - 1) // ks

    def scores(c):
        kc = k_ref[pl.ds(pl.multiple_of(c * ks, ks), ks), :]
        s = _dot_nt(kc, q)
        return s, jnp.max(s, axis=0, keepdims=True)

    def update(c, s_and_max, carry):
        s, smax = s_and_max
        m, l, acc = carry
        m_new = jnp.maximum(m, smax)
        alpha = jnp.exp2(m - m_new)
        part = jnp.zeros((SUBLANES, tq), F32)
        ps = []
        for r in range(ks // ATT_SLAB):
            p = jnp.exp2(s[r * ATT_SLAB:(r + 1) * ATT_SLAB, :] - m_new)
            for r8 in range(ATT_SLAB // SUBLANES):
                part = part + p[r8 * SUBLANES:(r8 + 1) * SUBLANES, :]
            ps.append(p.astype(BF16))
        l = alpha * l + jnp.sum(part, axis=0, keepdims=True)
        acc = alpha * acc + _dot(vt_ref[0, c], jnp.concatenate(ps, axis=0))
        return m_new, l, acc

    def body(c, state):
        s, carry = state
        return scores(c + 1), update(c, s, carry)

    carry = (jnp.full((1, tq), NEG_BIG, F32), jnp.zeros((1, tq), F32), jnp.zeros((V_HEAD, tq), F32))
    (s, _), carry = lax.fori_loop(0, last, body, (scores(0), carry))
    kch = (last * ks + lax.broadcasted_iota(jnp.int32, s.shape, 0)) // CHUNK
    qch = (i * tq + lax.broadcasted_iota(jnp.int32, s.shape, 1)) // CHUNK
    s = jnp.where(kch <= qch, s, NEG_BIG)
    _, l, acc = update(last, (s, jnp.max(s, axis=0, keepdims=True)), carry)
    o_ref[...] = (acc / l).T.astype(o_ref.dtype)


def _flash(q, k, vt, S):
    T = q.shape[0]
    B = T // S
    H = MLA_HEADS
    tq = min(ATT_TILE, S)
    ks = ATT_KEYS
    nq = S // tq
    dq = QK_NOPE + LANES
    return pl.pallas_call(
        functools.partial(_flash_body, tq=tq, ks=ks),
        out_shape=jax.ShapeDtypeStruct((T, H * V_HEAD), BF16),
        grid=(B, H, nq),
        in_specs=[pl.BlockSpec((tq, dq), lambda b, h, i: (b * nq + i, h)),
                  pl.BlockSpec((S, dq), lambda b, h, i: (b, h)),
                  pl.BlockSpec((1, S // ks, V_HEAD, ks), lambda b, h, i: (h, b, 0, 0))],
        out_specs=pl.BlockSpec((tq, V_HEAD), lambda b, h, i: (b * nq + i, h)),
        compiler_params=_cparams("parallel", "parallel", "arbitrary"),
        name="flash_attn",
    )(q, k, vt)


def _final_body(h_ref, g_ref, o_ref):
    o_ref[...] = _rms(h_ref[...], g_ref[...])


def _final_norm(h, g):
    T, D = h.shape
    tm = min(T, 1024)
    return pl.pallas_call(
        _final_body,
        out_shape=jax.ShapeDtypeStruct((T, D), F32),
        grid=(T // tm,),
        in_specs=[pl.BlockSpec((tm, D), lambda i: (i, 0)), pl.BlockSpec((1, D), lambda i: (0, 0))],
        out_specs=pl.BlockSpec((tm, D), lambda i: (i, 0)),
        compiler_params=_cparams("parallel"),
        name="final_norm",
    )(h, g)


def _rope_swap(w):
    half = w.shape[-1] // 2
    return jnp.concatenate([w[..., half:], w[..., :half]], axis=-1)


def _pad_lanes(w):
    return jnp.pad(w, [(0, 0)] * (w.ndim - 1) + [(0, LANES - w.shape[-1])])


def kernel(x, c, positions, w_mod, b_mod, norm_mix, norm_ffn, ret_w_in, ret_w_out, w_mod_kv, b_mod_kv, norm_kv, mla_w_kv_a, mla_kv_norm, mla_w_kv_b, mla_w_q_a, mla_q_norm, mla_w_q_b, mla_w_o, router_w, router_b, moe_w_gate, moe_w_up, moe_w_down, final_norm):
    B, S, D = x.shape
    T = B * S
    depth = w_mod.shape[0]
    n_a = ret_w_in.shape[0]

    c8 = jnp.pad(c, ((0, 8 - B), (0, 0)))
    mod = _mod_vectors(c8, w_mod, b_mod)[:, :B]
    mod = mod.reshape(depth, B, 6, 1, D)
    kv_mod = _mod_vectors(c8, w_mod_kv[None], b_mod_kv[None])[0, :B].reshape(B, 2, 1, D)

    pos_col = positions.reshape(T, 1)
    ones = jnp.ones((1, LANES), F32)
    inv_ret = (ROPE_THETA ** (-jnp.arange(LANES, dtype=F32) / LANES)).reshape(1, LANES)
    cos_r, sin_r = _rope_tables(pos_col, inv_ret, ones, ones)
    hr = QK_ROPE // 2
    inv_m = ROPE_THETA ** (-jnp.arange(hr, dtype=F32) / hr)
    inv_m = _pad_lanes(jnp.concatenate([inv_m, inv_m])[None])
    cm = _pad_lanes(jnp.ones((1, QK_ROPE), F32))
    sm = _pad_lanes(jnp.concatenate([-jnp.ones((1, hr), F32), jnp.ones((1, hr), F32)], axis=-1))
    cos_m, sin_m = _rope_tables(pos_col, inv_m, cm, sm)

    wr_t = router_w.T
    br = router_b.reshape(N_EXPERTS, 1)
    wgu = jnp.concatenate([moe_w_gate, moe_w_up], axis=-1).astype(BF16)
    wdn = moe_w_down.astype(BF16)

    h = x.reshape(T, D)
    lay = _MoeLayout(T, min(TOK_TILE, S))
    k_full = v_full = None
    for layer in range(depth):
        sh1, sc1, g1, sh2, sc2, g2 = (mod[layer, :, i] for i in range(6))
        gmix = norm_mix[layer].reshape(1, D)
        if layer < n_a:
            proj = _ret_inproj(h, gmix, sh1, sc1, ret_w_in[layer].astype(BF16), S)
            mix = _retention(proj, cos_r, sin_r, S, D)
            w_o = ret_w_out[layer].astype(BF16)
        else:
            if layer == n_a:
                wa = mla_w_kv_a
                wa_r = wa[:, KV_LORA:]
                wa_p = jnp.concatenate([wa[:, :KV_LORA], _pad_lanes(wa_r), _pad_lanes(_rope_swap(wa_r))],
                                       axis=-1).astype(BF16)
                wb = mla_w_kv_b.reshape(KV_LORA, MLA_HEADS, QK_NOPE + V_HEAD)
                wb_p = jnp.concatenate([wb[..., :QK_NOPE].reshape(KV_LORA, -1),
                                        wb[..., QK_NOPE:].reshape(KV_LORA, -1)], axis=-1).astype(BF16)
                k_full, v_full = _mla_kv(h, norm_kv.reshape(1, D), kv_mod[:, 0], kv_mod[:, 1], wa_p,
                                         mla_kv_norm.reshape(1, KV_LORA), wb_p, cos_m, sin_m, S)
            j = layer - n_a
            wq = mla_w_q_b[j].reshape(Q_LORA, MLA_HEADS, QK_NOPE + QK_ROPE)
            wq_r = wq[..., QK_NOPE:]
            wq_p = jnp.concatenate([wq[..., :QK_NOPE], _pad_lanes(wq_r), _pad_lanes(_rope_swap(wq_r))],
                                   axis=-1).reshape(Q_LORA, -1).astype(BF16)
            q_full = _mla_q(h, gmix, sh1, sc1, mla_w_q_a[j].astype(BF16),
                            mla_q_norm[j].reshape(1, Q_LORA), wq_p, cos_m, sin_m, S)
            mix = _flash(q_full, k_full, v_full, S)
            w_o = mla_w_o[j].astype(BF16)
        h, xn, rows, cols, tbl, used = _outproj_route(mix, w_o, h, g1, norm_ffn[layer].reshape(1, D),
                                                      sh2, sc2, wr_t, br, S, lay)
        h = _moe(xn, rows, cols, tbl, used[:, 0], wgu[layer], wdn[layer], h, g2, S, lay)
    return _final_norm(h, final_norm.reshape(1, D)).reshape(B, S, D)
```

```python
import functools

import jax
import jax.numpy as jnp
import numpy as np
from jax import lax
from jax.experimental import pallas as pl
from jax.experimental.pallas import tpu as pltpu

F32 = jnp.float32
BF16 = jnp.bfloat16

CHUNK = 64
RET_HEADS = 4
MLA_HEADS = 8
QK_NOPE = 128
QK_ROPE = 64
V_HEAD = 128
Q_LORA = 256
KV_LORA = 128
N_EXPERTS = 16
N_GROUPS = 4
EXPERTS_PER_GROUP = N_EXPERTS // N_GROUPS
D_EXPERT = 512
ROPE_THETA = 10000.0
EPS = 1e-6

LANES = 128
VMEM_LIMIT = 56 * 1024 * 1024
NEG_BIG = -1e30
LOG2E = 1.4426950408889634

RET_CHUNK = 256
TOK_TILE = 512
ATT_TILE = 512
ATT_KEYS = 512
ATT_SLAB = 32
SUBLANES = 8
MXU_DIM = 256
MOE_GROUP = 16
MOE_ROW_TILE = 512


def _cparams(*sem):
    return pltpu.CompilerParams(dimension_semantics=sem, vmem_limit_bytes=VMEM_LIMIT)


def _silu(x):
    return x * jax.nn.sigmoid(x)


def _rms(x, g):
    return x * lax.rsqrt(jnp.mean(x * x, axis=-1, keepdims=True) + EPS) * g


def _norm_mod(h, g, shift, scale):
    return _rms(h, g) * (1.0 + scale) + shift


def _dot(a, b):
    return jnp.dot(a, b, preferred_element_type=F32)


def _dot_nt(a, b, **kw):
    return lax.dot_general(a, b, (((1,), (1,)), ((), ())), preferred_element_type=F32, **kw)


def _dot_tn(a, b):
    return lax.dot_general(a, b, (((0,), (0,)), ((), ())), preferred_element_type=F32)


def _mod_body(c_ref, w_ref, b_ref, o_ref):
    ca = _silu(c_ref[...])
    o_ref[0] = jnp.dot(ca, w_ref[0], preferred_element_type=F32,
                       precision=lax.Precision.HIGHEST) + b_ref[0]


def _mod_vectors(c8, w, b):
    L, D, N = w.shape
    tn = D
    assert N % tn == 0
    return pl.pallas_call(
        _mod_body,
        out_shape=jax.ShapeDtypeStruct((L, 8, N), F32),
        grid=(L, N // tn),
        in_specs=[pl.BlockSpec((8, D), lambda l, j: (0, 0)),
                  pl.BlockSpec((1, D, tn), lambda l, j: (l, 0, j)),
                  pl.BlockSpec((1, 1, tn), lambda l, j: (l, 0, j))],
        out_specs=pl.BlockSpec((1, 8, tn), lambda l, j: (l, 0, j)),
        compiler_params=_cparams("parallel", "parallel"),
        name="mod_vectors",
    )(c8, w, b.reshape(L, 1, N))


def _rope_body(pos_ref, inv_ref, cm_ref, sm_ref, cos_ref, sin_ref):
    ang = pos_ref[...].astype(F32) * inv_ref[...]
    cos_ref[...] = jnp.cos(ang) * cm_ref[...]
    sin_ref[...] = jnp.sin(ang) * sm_ref[...]


def _rope_tables(pos_col, inv, cm, sm):
    T = pos_col.shape[0]
    tm = min(T, 1024)
    row = pl.BlockSpec((1, LANES), lambda i: (0, 0))
    return pl.pallas_call(
        _rope_body,
        out_shape=(jax.ShapeDtypeStruct((T, LANES), F32),) * 2,
        grid=(T // tm,),
        in_specs=[pl.BlockSpec((tm, 1), lambda i: (i, 0)), row, row, row],
        out_specs=(pl.BlockSpec((tm, LANES), lambda i: (i, 0)),) * 2,
        compiler_params=_cparams("parallel"),
        name="rope_tables",
    )(pos_col, inv, cm, sm)


def _inproj_body(h_ref, g_ref, sh_ref, sc_ref, w_ref, o_ref, *, tn):
    xn = _norm_mod(h_ref[...], g_ref[...], sh_ref[0], sc_ref[0]).astype(BF16)
    for j in range(w_ref.shape[1] // tn):
        o_ref[:, j * tn:(j + 1) * tn] = _dot(xn, w_ref[:, j * tn:(j + 1) * tn]).astype(BF16)


def _ret_inproj(h, g, sh, sc, w, S):
    T, D = h.shape
    N = w.shape[1]
    tm = min(TOK_TILE, S)
    per_b = S // tm
    vec = pl.BlockSpec((1, 1, D), lambda i: (i // per_b, 0, 0))
    return pl.pallas_call(
        functools.partial(_inproj_body, tn=512),
        out_shape=jax.ShapeDtypeStruct((T, N), BF16),
        grid=(T // tm,),
        in_specs=[pl.BlockSpec((tm, D), lambda i: (i, 0)),
                  pl.BlockSpec((1, D), lambda i: (0, 0)),
                  vec, vec,
                  pl.BlockSpec((D, N), lambda i: (0, 0))],
        out_specs=pl.BlockSpec((tm, N), lambda i: (i, 0)),
        compiler_params=_cparams("parallel"),
        name="ret_inproj",
    )(h, g, sh, sc, w)


def _ret_body(q_ref, k_ref, v_ref, g_ref, cos_ref, sin_ref, di_ref, dq_ref, dk_ref, dc_ref,
              y_ref, state_ref, *, dk_dim):
    @pl.when(pl.program_id(2) == 0)
    def _():
        state_ref[...] = jnp.zeros_like(state_ref)

    cos = cos_ref[...]
    sin = sin_ref[...]
    half = dk_dim // 2

    def rope(x):
        x1, x2 = x[:, :half], x[:, half:]
        return jnp.concatenate([x1 * cos - x2 * sin, x1 * sin + x2 * cos], axis=-1)

    qr = rope(q_ref[...].astype(F32))
    kr = rope(k_ref[...].astype(F32)) * (dk_dim ** -0.5)
    qb = qr.astype(BF16)
    v = v_ref[...]
    inner = (_dot_nt(qb, kr.astype(BF16)) * di_ref[0]).astype(BF16)
    st = state_ref[...]
    out = _dot(inner, v) + _dot(qb, st.astype(BF16)) * dq_ref[0]
    kd = (kr * dk_ref[0]).astype(BF16)
    state_ref[...] = st * dc_ref[0] + _dot_tn(kd, v)

    mu = jnp.mean(out, axis=-1, keepdims=True)
    cen = out - mu
    var = jnp.mean(cen * cen, axis=-1, keepdims=True)
    o = cen * lax.rsqrt(var + EPS)
    y_ref[...] = (_silu(g_ref[...].astype(F32)) * o).astype(BF16)


def _retention(proj, cos, sin, S, D):
    T = proj.shape[0]
    B = T // S
    H = RET_HEADS
    dk = D // H
    dv = 2 * dk
    C = min(RET_CHUNK, S)
    n = S // C
    log_g = jnp.log1p(-(2.0 ** (-5.0 - jnp.arange(H, dtype=F32))))
    t = jnp.arange(C, dtype=F32)
    diff = t[:, None] - t[None, :]
    d_intra = jnp.where(diff >= 0, jnp.exp(log_g[:, None, None] * jnp.maximum(diff, 0.0)), 0.0)
    d_q = jnp.exp(log_g[:, None] * (t + 1.0))[:, :, None]
    d_k = jnp.exp(log_g[:, None] * (C - 1.0 - t))[:, :, None]
    d_c = jnp.exp(log_g * C)[:, None, None]

    row = lambda b, h, i: b * n + i
    kq = D // dk
    kv = 2 * D // dv
    kg = kv + H
    return pl.pallas_call(
        functools.partial(_ret_body, dk_dim=dk),
        out_shape=jax.ShapeDtypeStruct((T, H * dv), BF16),
        grid=(B, H, n),
        in_specs=[pl.BlockSpec((C, dk), lambda b, h, i: (row(b, h, i), h)),
                  pl.BlockSpec((C, dk), lambda b, h, i: (row(b, h, i), kq + h)),
                  pl.BlockSpec((C, dv), lambda b, h, i: (row(b, h, i), kv + h)),
                  pl.BlockSpec((C, dv), lambda b, h, i: (row(b, h, i), kg + h)),
                  pl.BlockSpec((C, dk // 2), lambda b, h, i: (row(b, h, i), 0)),
                  pl.BlockSpec((C, dk // 2), lambda b, h, i: (row(b, h, i), 0)),
                  pl.BlockSpec((1, C, C), lambda b, h, i: (h, 0, 0)),
                  pl.BlockSpec((1, C, 1), lambda b, h, i: (h, 0, 0)),
                  pl.BlockSpec((1, C, 1), lambda b, h, i: (h, 0, 0)),
                  pl.BlockSpec((1, 1, 1), lambda b, h, i: (h, 0, 0))],
        out_specs=pl.BlockSpec((C, dv), lambda b, h, i: (row(b, h, i), h)),
        scratch_shapes=[pltpu.VMEM((dk, dv), F32)],
        compiler_params=_cparams("parallel", "parallel", "arbitrary"),
        name="retention",
    )(proj, proj, proj, proj, cos, sin, d_intra, d_q, d_k, d_c)


def _route(logits_t, bias):
    sc = jax.nn.sigmoid(logits_t)
    bi = sc + bias
    s_rows = [sc[e:e + 1, :] for e in range(N_EXPERTS)]
    b_rows = [bi[e:e + 1, :] for e in range(N_EXPERTS)]

    def top2sum(a, b, c, d):
        p, q = jnp.maximum(a, b), jnp.minimum(a, b)
        r, s = jnp.maximum(c, d), jnp.minimum(c, d)
        return jnp.maximum(p, r) + jnp.maximum(jnp.minimum(p, r), jnp.maximum(q, s))

    n = EXPERTS_PER_GROUP
    gs = [top2sum(*b_rows[n * g:n * g + n]) for g in range(N_GROUPS)]
    best, gi = gs[0], jnp.zeros_like(gs[0], dtype=jnp.int32)
    for g in range(1, N_GROUPS):
        upd = gs[g] > best
        gi = jnp.where(upd, g, gi)
        best = jnp.where(upd, gs[g], best)

    def pick(rows, j):
        out = rows[j]
        for g in range(1, N_GROUPS):
            out = jnp.where(gi == g, rows[n * g + j], out)
        return out

    vb = [pick(b_rows, j) for j in range(n)]
    vs = [pick(s_rows, j) for j in range(n)]

    def argmax_first(vals):
        best, idx = vals[0], jnp.zeros_like(gi)
        for j in range(1, n):
            upd = vals[j] > best
            idx = jnp.where(upd, j, idx)
            best = jnp.where(upd, vals[j], best)
        return idx

    i1 = argmax_first(vb)
    i2 = argmax_first([jnp.where(i1 == j, -jnp.inf, vb[j]) for j in range(n)])

    def take(vals, idx):
        out = vals[0]
        for j in range(1, n):
            out = jnp.where(idx == j, vals[j], out)
        return out

    w1, w2 = take(vs, i1), take(vs, i2)
    tot = w1 + w2
    w1, w2 = w1 / tot, w2 / tot
    return gi * n + i1, gi * n + i2, w1, w2


class _MoeLayout:
    def __init__(self, T, tm):
        self.tm = tm
        self.n_tiles = T // tm
        self.group = MOE_GROUP
        self.rt = -(-(2 * tm + N_EXPERTS * (MOE_GROUP - 1)) // MXU_DIM) * MXU_DIM
        self.ng = self.rt // MOE_GROUP
        assert self.ng < LANES
        self.tmx = MOE_ROW_TILE
        self.cap = -(-(T + self.n_tiles * MOE_GROUP) // self.tmx) * self.tmx
        self.dump = N_EXPERTS * self.cap
        self.rows = self.dump + self.rt
        pad = self.n_tiles * N_EXPERTS * (MOE_GROUP - 1)
        self.nt = (2 * T + pad) // self.tmx + N_EXPERTS


def _dispatch_meta(e1, e2, cum, tri, lay):
    E, G = N_EXPERTS, lay.group
    tm = e1.shape[1]
    eid = lax.broadcasted_iota(jnp.int32, (E, tm), 0)
    oh1, oh2 = eid == e1, eid == e2
    cnt = jnp.where(oh1 | oh2, 1.0, 0.0)
    pre = _dot(cnt.astype(BF16), tri)
    tot = jnp.sum(cnt, axis=1, keepdims=True)
    ptot = jnp.broadcast_to(jnp.ceil(tot * (1.0 / G)) * G, (E, LANES))
    below = jnp.where(lax.broadcasted_iota(jnp.int32, (E, E), 0) > lax.broadcasted_iota(jnp.int32, (E, E), 1),
                      1.0, 0.0)
    loff = jnp.dot(below, ptot, preferred_element_type=F32, precision=lax.Precision.HIGHEST)
    pos_e = loff[:, :1] + pre
    pos1 = jnp.sum(jnp.where(oh1, pos_e, 0.0), axis=0, keepdims=True)
    pos2 = jnp.sum(jnp.where(oh2, pos_e, 0.0), axis=0, keepdims=True)

    lane = lax.broadcasted_iota(jnp.int32, (E, LANES), 1)
    g_row = (lane * G).astype(F32)
    eg = jnp.sum(jnp.where(loff + ptot <= g_row, 1, 0), axis=0, keepdims=True)
    erow = lax.broadcasted_iota(jnp.int32, (E, LANES), 0)
    base = erow.astype(F32) * float(lay.cap) + cum - loff
    sel = jnp.sum(jnp.where(erow == eg, base, 0.0), axis=0, keepdims=True)
    dst = jnp.where(eg < E, g_row[:1] + sel, float(lay.dump) + g_row[:1])
    n_used = jnp.sum(ptot[:, :1], axis=0, keepdims=True) * (1.0 / G)
    table = jnp.where(lane[:1] == LANES - 1, n_used, dst).astype(jnp.int32)
    return pos1, pos2, table, cum + ptot


def _outproj_body(y_ref, w_ref, h_ref, g1_ref, gn_ref, sh_ref, sc_ref, wr_ref, br_ref, tri_ref,
                  ho_ref, xn_ref, rows_ref, cols_ref, tbl_ref, cum_ref, *, lay):
    @pl.when(pl.program_id(0) == 0)
    def _():
        cum_ref[...] = jnp.zeros_like(cum_ref)

    hn = h_ref[...] + g1_ref[0] * _dot(y_ref[...], w_ref[...])
    ho_ref[...] = hn
    xn = _norm_mod(hn, gn_ref[...], sh_ref[0], sc_ref[0])
    xn_ref[...] = xn.astype(BF16)
    logits_t = _dot_nt(wr_ref[...], xn, precision=lax.Precision.HIGHEST)
    e1, e2, w1, w2 = _route(logits_t, br_ref[...])
    pos1, pos2, table, cum = _dispatch_meta(e1, e2, cum_ref[...], tri_ref[...], lay)
    cum_ref[...] = cum
    tbl_ref[0] = table
    tm = e1.shape[1]
    rows = jnp.concatenate([pos1, pos2, w1, w2, jnp.zeros((SUBLANES - 4, tm), F32)], axis=0)
    rows_ref[...] = rows
    cols_ref[...] = jnp.concatenate([rows, jnp.zeros((LANES - SUBLANES, tm), F32)], axis=0).T


def _outproj_route(y, w, h, g1, gn, sh, sc, wr_t, br, S, lay):
    T, D = h.shape
    K = y.shape[1]
    tm = lay.tm
    per_b = S // tm
    vec = pl.BlockSpec((1, 1, D), lambda i: (i // per_b, 0, 0))
    tok = lambda n: pl.BlockSpec((tm, n), lambda i: (i, 0))
    tri = jnp.triu(jnp.ones((tm, tm), BF16), k=1)
    return pl.pallas_call(
        functools.partial(_outproj_body, lay=lay),
        out_shape=(jax.ShapeDtypeStruct((T, D), F32),
                   jax.ShapeDtypeStruct((T, D), BF16),
                   jax.ShapeDtypeStruct((SUBLANES, T), F32),
                   jax.ShapeDtypeStruct((T, LANES), F32),
                   jax.ShapeDtypeStruct((lay.n_tiles, 1, LANES), jnp.int32),
                   jax.ShapeDtypeStruct((N_EXPERTS, LANES), F32)),
        grid=(T // tm,),
        in_specs=[tok(K),
                  pl.BlockSpec((K, D), lambda i: (0, 0)),
                  tok(D), vec,
                  pl.BlockSpec((1, D), lambda i: (0, 0)),
                  vec, vec,
                  pl.BlockSpec((N_EXPERTS, D), lambda i: (0, 0)),
                  pl.BlockSpec((N_EXPERTS, 1), lambda i: (0, 0)),
                  pl.BlockSpec((tm, tm), lambda i: (0, 0))],
        out_specs=(tok(D), tok(D),
                   pl.BlockSpec((SUBLANES, tm), lambda i: (0, i)),
                   tok(LANES),
                   pl.BlockSpec((1, 1, LANES), lambda i: (i, 0, 0)),
                   pl.BlockSpec((N_EXPERTS, LANES), lambda i: (0, 0))),
        compiler_params=_cparams("arbitrary"),
        name="outproj_route",
    )(y, w, h, g1, gn, sh, sc, wr_t, br, tri)


def _group_copy(hbm_ref, buf_ref, tbl_ref, i, g, sem, lay, to_hbm):
    G = lay.group
    hbm = hbm_ref.at[pl.ds(pl.multiple_of(tbl_ref[i * LANES + g], G), G), :]
    vmem = buf_ref.at[pl.ds(pl.multiple_of(g * G, G), G), :]
    return pltpu.make_async_copy(vmem, hbm, sem) if to_hbm else pltpu.make_async_copy(hbm, vmem, sem)


def _dispatch_body(tbl_ref, x_ref, rows_ref, xs_ref, buf_ref, sem, *, lay):
    i = pl.program_id(0)
    pos1 = rows_ref[0:1, :].astype(jnp.int32)
    pos2 = rows_ref[1:2, :].astype(jnp.int32)
    r = lax.broadcasted_iota(jnp.int32, (lay.rt, lay.tm), 0)
    perm = jnp.where((r == pos1) | (r == pos2), 1.0, 0.0).astype(BF16)
    buf_ref[...] = _dot(perm, x_ref[...]).astype(BF16)
    for g in range(lay.ng):
        _group_copy(xs_ref, buf_ref, tbl_ref, i, g, sem, lay, True).start()
    for g in range(lay.ng):
        _group_copy(xs_ref, buf_ref, tbl_ref, i, g, sem, lay, True).wait()


def _moe_dispatch(xn, rows, tbl, lay):
    T, D = xn.shape
    tm = lay.tm
    grid_spec = pltpu.PrefetchScalarGridSpec(
        num_scalar_prefetch=1,
        grid=(lay.n_tiles,),
        in_specs=[pl.BlockSpec((tm, D), lambda i, tbl: (i, 0)),
                  pl.BlockSpec((SUBLANES, tm), lambda i, tbl: (0, i))],
        out_specs=pl.BlockSpec(memory_space=pl.ANY),
        scratch_shapes=[pltpu.VMEM((lay.rt, D), BF16), pltpu.SemaphoreType.DMA(())],
    )
    return pl.pallas_call(
        functools.partial(_dispatch_body, lay=lay),
        out_shape=jax.ShapeDtypeStruct((lay.rows, D), BF16),
        grid_spec=grid_spec,
        compiler_params=_cparams("arbitrary"),
        name="moe_dispatch",
    )(tbl, xn, rows)


def _expert_body(te_ref, tb_ref, nv_ref, x_ref, wgu_ref, wd_ref, y_ref):
    @pl.when(pl.program_id(0) < nv_ref[0])
    def _():
        hgu = _dot(x_ref[...], wgu_ref[0])
        f = hgu.shape[1] // 2
        hdn = (_silu(hgu[:, :f]) * hgu[:, f:]).astype(BF16)
        y_ref[...] = _dot(hdn, wd_ref[0]).astype(BF16)


def _moe_experts(xs, wgu, wd, tile_e, tile_blk, n_valid, lay):
    R, D = xs.shape
    E, _, F2 = wgu.shape
    tmx = lay.tmx
    grid_spec = pltpu.PrefetchScalarGridSpec(
        num_scalar_prefetch=3,
        grid=(lay.nt,),
        in_specs=[pl.BlockSpec((tmx, D), lambda n, te, tb, nv: (tb[n], 0)),
                  pl.BlockSpec((1, D, F2), lambda n, te, tb, nv: (te[n], 0, 0)),
                  pl.BlockSpec((1, F2 // 2, D), lambda n, te, tb, nv: (te[n], 0, 0))],
        out_specs=pl.BlockSpec((tmx, D), lambda n, te, tb, nv: (tb[n], 0)),
    )
    return pl.pallas_call(
        _expert_body,
        out_shape=jax.ShapeDtypeStruct((R, D), BF16),
        grid_spec=grid_spec,
        compiler_params=_cparams("arbitrary"),
        name="moe_experts",
    )(tile_e, tile_blk, n_valid, xs, wgu, wd)


def _combine_body(tbl_ref, cols_ref, h_ref, g2_ref, ys_ref, o_ref, buf_ref, sem, *, lay):
    i = pl.program_id(0)

    @pl.when(i == 0)
    def _():
        buf_ref[...] = jnp.zeros_like(buf_ref)

    n_used = tbl_ref[i * LANES + LANES - 1]

    def start(g, _):
        _group_copy(ys_ref, buf_ref, tbl_ref, i, g, sem, lay, False).start()
        return 0

    def wait(g, _):
        _group_copy(ys_ref, buf_ref, tbl_ref, i, g, sem, lay, False).wait()
        return 0

    lax.fori_loop(0, n_used, start, 0)
    lax.fori_loop(0, n_used, wait, 0)

    cols = cols_ref[...]
    r = lax.broadcasted_iota(jnp.int32, (lay.tm, lay.rt), 1)
    ys = buf_ref[...]
    y1 = _dot(jnp.where(r == cols[:, 0:1].astype(jnp.int32), 1.0, 0.0).astype(BF16), ys)
    y2 = _dot(jnp.where(r == cols[:, 1:2].astype(jnp.int32), 1.0, 0.0).astype(BF16), ys)
    o_ref[...] = h_ref[...] + g2_ref[0] * (cols[:, 2:3] * y1 + cols[:, 3:4] * y2)


def _moe_combine(ys, cols, tbl, h, g2, S, lay):
    T, D = h.shape
    tm = lay.tm
    per_b = S // tm
    grid_spec = pltpu.PrefetchScalarGridSpec(
        num_scalar_prefetch=1,
        grid=(lay.n_tiles,),
        in_specs=[pl.BlockSpec((tm, LANES), lambda i, tbl: (i, 0)),
                  pl.BlockSpec((tm, D), lambda i, tbl: (i, 0)),
                  pl.BlockSpec((1, 1, D), lambda i, tbl: (i // per_b, 0, 0)),
                  pl.BlockSpec(memory_space=pl.ANY)],
        out_specs=pl.BlockSpec((tm, D), lambda i, tbl: (i, 0)),
        scratch_shapes=[pltpu.VMEM((lay.rt, D), BF16), pltpu.SemaphoreType.DMA(())],
    )
    return pl.pallas_call(
        functools.partial(_combine_body, lay=lay),
        out_shape=jax.ShapeDtypeStruct((T, D), F32),
        grid_spec=grid_spec,
        compiler_params=_cparams("arbitrary"),
        name="moe_combine",
    )(tbl, cols, h, g2, ys)


def _expert_tiles(used, lay):
    tiles = jnp.ceil(used / lay.tmx).astype(jnp.int32)
    ends = jnp.cumsum(tiles)
    n_valid = ends[-1]
    n = jnp.minimum(jnp.arange(lay.nt, dtype=jnp.int32), n_valid - 1)
    e = jnp.sum((ends[None, :] <= n[:, None]).astype(jnp.int32), axis=1)
    blk = e * (lay.cap // lay.tmx) + n - (ends - tiles)[e]
    return e, blk, n_valid.reshape(1)


def _moe(xn, rows, cols, tbl, used, wgu, wd, h, g2, S, lay):
    tbl = tbl.reshape(-1)
    xs = _moe_dispatch(xn, rows, tbl, lay)
    ys = _moe_experts(xs, wgu, wd, *_expert_tiles(used, lay), lay)
    return _moe_combine(ys, cols, tbl, h, g2, S, lay)


def _kv_body(h_ref, g_ref, sh_ref, sc_ref, wa_ref, gkv_ref, wb_ref, cos_ref, sin_ref,
             k_ref, vt_ref):
    hn = _norm_mod(h_ref[...], g_ref[...], sh_ref[0], sc_ref[0]).astype(BF16)
    a = _dot(hn, wa_ref[...])
    c_kv = _rms(a[:, :KV_LORA], gkv_ref[...]).astype(BF16)
    kr = (a[:, KV_LORA:KV_LORA + LANES] * cos_ref[...]
          + a[:, KV_LORA + LANES:] * sin_ref[...]).astype(BF16)
    kv = _dot(c_kv, wb_ref[...])
    hk = MLA_HEADS * QK_NOPE
    w = QK_NOPE + LANES
    for hd in range(MLA_HEADS):
        k_ref[:, hd * w:hd * w + QK_NOPE] = kv[:, hd * QK_NOPE:(hd + 1) * QK_NOPE].astype(BF16)
        k_ref[:, hd * w + QK_NOPE:(hd + 1) * w] = kr
        vh = kv[:, hk + hd * V_HEAD:hk + (hd + 1) * V_HEAD]
        for g in range(vt_ref.shape[1]):
            vt_ref[hd, g] = vh[g * ATT_KEYS:(g + 1) * ATT_KEYS, :].T.astype(BF16)


def _mla_kv(h, g, sh, sc, wa, gkv, wb, cos, sin, S):
    T, D = h.shape
    tm = min(TOK_TILE, S)
    per_b = S // tm
    vec = pl.BlockSpec((1, 1, D), lambda i: (i // per_b, 0, 0))
    tok = lambda n: pl.BlockSpec((tm, n), lambda i: (i, 0))
    full = lambda a: pl.BlockSpec(a.shape, lambda i: (0, 0))
    kw = MLA_HEADS * (QK_NOPE + LANES)
    gk = tm // ATT_KEYS
    vt_shape = (MLA_HEADS, T // ATT_KEYS, V_HEAD, ATT_KEYS)
    return pl.pallas_call(
        _kv_body,
        out_shape=(jax.ShapeDtypeStruct((T, kw), BF16), jax.ShapeDtypeStruct(vt_shape, BF16)),
        grid=(T // tm,),
        in_specs=[tok(D), full(g), vec, vec, full(wa), full(gkv), full(wb), tok(LANES), tok(LANES)],
        out_specs=(tok(kw),
                   pl.BlockSpec((MLA_HEADS, gk, V_HEAD, ATT_KEYS), lambda i: (0, i, 0, 0))),
        compiler_params=_cparams("parallel"),
        name="mla_kv",
    )(h, g, sh, sc, wa, gkv, wb, cos, sin)


def _q_body(h_ref, g_ref, sh_ref, sc_ref, wa_ref, gq_ref, wb_ref, cos_ref, sin_ref, q_ref, *, scale):
    xn = _norm_mod(h_ref[...], g_ref[...], sh_ref[0], sc_ref[0]).astype(BF16)
    qa = _rms(_dot(xn, wa_ref[...]), gq_ref[...]).astype(BF16)
    cos = cos_ref[...]
    sin = sin_ref[...]
    wi = QK_NOPE + 2 * LANES
    wo = QK_NOPE + LANES
    for hd in range(MLA_HEADS):
        qb = _dot(qa, wb_ref[:, hd * wi:(hd + 1) * wi])
        q_ref[:, hd * wo:hd * wo + QK_NOPE] = (qb[:, :QK_NOPE] * scale).astype(BF16)
        rp = qb[:, QK_NOPE:QK_NOPE + LANES] * cos + qb[:, QK_NOPE + LANES:] * sin
        q_ref[:, hd * wo + QK_NOPE:(hd + 1) * wo] = (rp * scale).astype(BF16)


def _mla_q(h, g, sh, sc, wa, gq, wb, cos, sin, S):
    T, D = h.shape
    tm = min(TOK_TILE, S)
    per_b = S // tm
    vec = pl.BlockSpec((1, 1, D), lambda i: (i // per_b, 0, 0))
    tok = lambda n: pl.BlockSpec((tm, n), lambda i: (i, 0))
    full = lambda a: pl.BlockSpec(a.shape, lambda i: (0, 0))
    qw = MLA_HEADS * (QK_NOPE + LANES)
    return pl.pallas_call(
        functools.partial(_q_body, scale=(QK_NOPE + QK_ROPE) ** -0.5 * LOG2E),
        out_shape=jax.ShapeDtypeStruct((T, qw), BF16),
        grid=(T // tm,),
        in_specs=[tok(D), full(g), vec, vec, full(wa), full(gq), full(wb), tok(LANES), tok(LANES)],
        out_specs=tok(qw),
        compiler_params=_cparams("parallel"),
        name="mla_q",
    )(h, g, sh, sc, wa, gq, wb, cos, sin)


def _flash_body(q_ref, k_ref, vt_ref, o_ref, s_ref, smax_ref, m_ref, l_ref, acc_ref, *, tq, ks):
    i = pl.program_id(2)
    last = (i * tq + tq - 1) // ks

    def scores(slot, c):
        kc = k_ref[pl.ds(pl.multiple_of(c * ks, ks), ks), :]
        s = _dot_nt(kc, q_ref[...])
        s_ref[slot] = s
        smax_ref[slot] = jnp.max(s, axis=0, keepdims=True)

    def update(slot, c, masked):
        if masked:
            s = s_ref[slot]
            kch = (c * ks + lax.broadcasted_iota(jnp.int32, s.shape, 0)) // CHUNK
            qch = (i * tq + lax.broadcasted_iota(jnp.int32, s.shape, 1)) // CHUNK
            s_ref[slot] = jnp.where(kch <= qch, s, NEG_BIG)
            smax = jnp.max(s_ref[slot], axis=0, keepdims=True)
        else:
            smax = smax_ref[slot]
        m = m_ref[...]
        m_new = jnp.maximum(m, smax)
        alpha = jnp.exp2(m - m_new)
        part = jnp.zeros((SUBLANES, tq), F32)
        ps = []
        for r in range(ks // ATT_SLAB):
            p = jnp.exp2(s_ref[slot, r * ATT_SLAB:(r + 1) * ATT_SLAB, :] - m_new)
            for r8 in range(ATT_SLAB // SUBLANES):
                part = part + p[r8 * SUBLANES:(r8 + 1) * SUBLANES, :]
            ps.append(p.astype(BF16))
        m_ref[...] = m_new
        l_ref[...] = alpha * l_ref[...] + jnp.sum(part, axis=0, keepdims=True)
        acc_ref[...] = alpha * acc_ref[...] + _dot(vt_ref[0, c], jnp.concatenate(ps, axis=0))

    m_ref[...] = jnp.full_like(m_ref, NEG_BIG)
    l_ref[...] = jnp.zeros_like(l_ref)
    acc_ref[...] = jnp.zeros_like(acc_ref)
    scores(0, 0)

    def pair(g, _):
        scores(1, 2 * g + 1)
        update(0, 2 * g, False)
        scores(0, 2 * g + 2)
        update(1, 2 * g + 1, False)
        return 0

    lax.fori_loop(0, last // 2, pair, 0)

    @pl.when(last % 2 == 1)
    def _():
        scores(1, last)
        update(0, last - 1, False)
        update(1, last, True)

    @pl.when(last % 2 == 0)
    def _():
        update(0, last, True)

    o_ref[...] = (acc_ref[...] / l_ref[...]).T.astype(o_ref.dtype)


def _flash(q, k, vt, S):
    T = q.shape[0]
    B = T // S
    H = MLA_HEADS
    tq = min(ATT_TILE, S)
    ks = ATT_KEYS
    nq = S // tq
    dq = QK_NOPE + LANES
    return pl.pallas_call(
        functools.partial(_flash_body, tq=tq, ks=ks),
        out_shape=jax.ShapeDtypeStruct((T, H * V_HEAD), BF16),
        grid=(B, H, nq),
        in_specs=[pl.BlockSpec((tq, dq), lambda b, h, i: (b * nq + i, h)),
                  pl.BlockSpec((S, dq), lambda b, h, i: (b, h)),
                  pl.BlockSpec((1, S // ks, V_HEAD, ks), lambda b, h, i: (h, b, 0, 0))],
        out_specs=pl.BlockSpec((tq, V_HEAD), lambda b, h, i: (b * nq + i, h)),
        scratch_shapes=[pltpu.VMEM((2, ks, tq), F32), pltpu.VMEM((2, 1, tq), F32),
                        pltpu.VMEM((1, tq), F32), pltpu.VMEM((1, tq), F32),
                        pltpu.VMEM((V_HEAD, tq), F32)],
        compiler_params=_cparams("parallel", "parallel", "arbitrary"),
        name="flash_attn",
    )(q, k, vt)


def _final_body(h_ref, g_ref, o_ref):
    o_ref[...] = _rms(h_ref[...], g_ref[...])


def _final_norm(h, g):
    T, D = h.shape
    tm = min(T, 1024)
    return pl.pallas_call(
        _final_body,
        out_shape=jax.ShapeDtypeStruct((T, D), F32),
        grid=(T // tm,),
        in_specs=[pl.BlockSpec((tm, D), lambda i: (i, 0)), pl.BlockSpec((1, D), lambda i: (0, 0))],
        out_specs=pl.BlockSpec((tm, D), lambda i: (i, 0)),
        compiler_params=_cparams("parallel"),
        name="final_norm",
    )(h, g)


def _rope_swap(w):
    half = w.shape[-1] // 2
    return jnp.concatenate([w[..., half:], w[..., :half]], axis=-1)


def _pad_lanes(w):
    return jnp.pad(w, [(0, 0)] * (w.ndim - 1) + [(0, LANES - w.shape[-1])])


def kernel(x, c, positions, w_mod, b_mod, norm_mix, norm_ffn, ret_w_in, ret_w_out, w_mod_kv, b_mod_kv, norm_kv, mla_w_kv_a, mla_kv_norm, mla_w_kv_b, mla_w_q_a, mla_q_norm, mla_w_q_b, mla_w_o, router_w, router_b, moe_w_gate, moe_w_up, moe_w_down, final_norm):
    B, S, D = x.shape
    T = B * S
    depth = w_mod.shape[0]
    n_a = ret_w_in.shape[0]

    c8 = jnp.pad(c, ((0, 8 - B), (0, 0)))
    mod = _mod_vectors(c8, w_mod, b_mod)[:, :B]
    mod = mod.reshape(depth, B, 6, 1, D)
    kv_mod = _mod_vectors(c8, w_mod_kv[None], b_mod_kv[None])[0, :B].reshape(B, 2, 1, D)

    pos_col = positions.reshape(T, 1)
    ones = jnp.ones((1, LANES), F32)
    inv_ret = (ROPE_THETA ** (-jnp.arange(LANES, dtype=F32) / LANES)).reshape(1, LANES)
    cos_r, sin_r = _rope_tables(pos_col, inv_ret, ones, ones)
    hr = QK_ROPE // 2
    inv_m = ROPE_THETA ** (-jnp.arange(hr, dtype=F32) / hr)
    inv_m = _pad_lanes(jnp.concatenate([inv_m, inv_m])[None])
    cm = _pad_lanes(jnp.ones((1, QK_ROPE), F32))
    sm = _pad_lanes(jnp.concatenate([-jnp.ones((1, hr), F32), jnp.ones((1, hr), F32)], axis=-1))
    cos_m, sin_m = _rope_tables(pos_col, inv_m, cm, sm)

    wr_t = router_w.T
    br = router_b.reshape(N_EXPERTS, 1)
    wgu = jnp.concatenate([moe_w_gate, moe_w_up], axis=-1).astype(BF16)
    wdn = moe_w_down.astype(BF16)

    h = x.reshape(T, D)
    lay = _MoeLayout(T, min(TOK_TILE, S))
    k_full = v_full = None
    for layer in range(depth):
        sh1, sc1, g1, sh2, sc2, g2 = (mod[layer, :, i] for i in range(6))
        gmix = norm_mix[layer].reshape(1, D)
        if layer < n_a:
            proj = _ret_inproj(h, gmix, sh1, sc1, ret_w_in[layer].astype(BF16), S)
            mix = _retention(proj, cos_r, sin_r, S, D)
            w_o = ret_w_out[layer].astype(BF16)
        else:
            if layer == n_a:
                wa = mla_w_kv_a
                wa_r = wa[:, KV_LORA:]
                wa_p = jnp.concatenate([wa[:, :KV_LORA], _pad_lanes(wa_r), _pad_lanes(_rope_swap(wa_r))],
                                       axis=-1).astype(BF16)
                wb = mla_w_kv_b.reshape(KV_LORA, MLA_HEADS, QK_NOPE + V_HEAD)
                wb_p = jnp.concatenate([wb[..., :QK_NOPE].reshape(KV_LORA, -1),
                                        wb[..., QK_NOPE:].reshape(KV_LORA, -1)], axis=-1).astype(BF16)
                k_full, v_full = _mla_kv(h, norm_kv.reshape(1, D), kv_mod[:, 0], kv_mod[:, 1], wa_p,
                                         mla_kv_norm.reshape(1, KV_LORA), wb_p, cos_m, sin_m, S)
            j = layer - n_a
            wq = mla_w_q_b[j].reshape(Q_LORA, MLA_HEADS, QK_NOPE + QK_ROPE)
            wq_r = wq[..., QK_NOPE:]
            wq_p = jnp.concatenate([wq[..., :QK_NOPE], _pad_lanes(wq_r), _pad_lanes(_rope_swap(wq_r))],
                                   axis=-1).reshape(Q_LORA, -1).astype(BF16)
            q_full = _mla_q(h, gmix, sh1, sc1, mla_w_q_a[j].astype(BF16),
                            mla_q_norm[j].reshape(1, Q_LORA), wq_p, cos_m, sin_m, S)
            mix = _flash(q_full, k_full, v_full, S)
            w_o = mla_w_o[j].astype(BF16)
        h, xn, rows, cols, tbl, used = _outproj_route(mix, w_o, h, g1, norm_ffn[layer].reshape(1, D),
                                                      sh2, sc2, wr_t, br, S, lay)
        h = _moe(xn, rows, cols, tbl, used[:, 0], wgu[layer], wdn[layer], h, g2, S, lay)
    return _final_norm(h, final_norm.reshape(1, D)).reshape(B, S, D)
```

```python
import functools

import jax
import jax.numpy as jnp
import numpy as np
from jax import lax
from jax.experimental import pallas as pl
from jax.experimental.pallas import tpu as pltpu

F32 = jnp.float32
BF16 = jnp.bfloat16

CHUNK = 64
RET_HEADS = 4
MLA_HEADS = 8
QK_NOPE = 128
QK_ROPE = 64
V_HEAD = 128
Q_LORA = 256
KV_LORA = 128
N_EXPERTS = 16
N_GROUPS = 4
EXPERTS_PER_GROUP = N_EXPERTS // N_GROUPS
D_EXPERT = 512
ROPE_THETA = 10000.0
EPS = 1e-6

LANES = 128
VMEM_LIMIT = 56 * 1024 * 1024
NEG_BIG = -1e30
LOG2E = 1.4426950408889634

RET_CHUNK = 256
TOK_TILE = 512
ATT_TILE = 512
ATT_KEYS = 512
ATT_SLAB = 32
SUBLANES = 8
MXU_DIM = 256
MOE_GROUP = 16
MOE_ROW_TILE = 512


def _cparams(*sem):
    return pltpu.CompilerParams(dimension_semantics=sem, vmem_limit_bytes=VMEM_LIMIT)


def _silu(x):
    return x * jax.nn.sigmoid(x)


def _rms(x, g):
    return x * lax.rsqrt(jnp.mean(x * x, axis=-1, keepdims=True) + EPS) * g


def _norm_mod(h, g, shift, scale):
    return _rms(h, g) * (1.0 + scale) + shift


def _dot(a, b):
    return jnp.dot(a, b, preferred_element_type=F32)


def _dot_nt(a, b, **kw):
    return lax.dot_general(a, b, (((1,), (1,)), ((), ())), preferred_element_type=F32, **kw)


def _dot_tn(a, b):
    return lax.dot_general(a, b, (((0,), (0,)), ((), ())), preferred_element_type=F32)


def _mod_body(c_ref, w_ref, b_ref, o_ref):
    ca = _silu(c_ref[...])
    o_ref[0] = jnp.dot(ca, w_ref[0], preferred_element_type=F32,
                       precision=lax.Precision.HIGHEST) + b_ref[0]


def _mod_vectors(c8, w, b):
    L, D, N = w.shape
    tn = D
    assert N % tn == 0
    return pl.pallas_call(
        _mod_body,
        out_shape=jax.ShapeDtypeStruct((L, 8, N), F32),
        grid=(L, N // tn),
        in_specs=[pl.BlockSpec((8, D), lambda l, j: (0, 0)),
                  pl.BlockSpec((1, D, tn), lambda l, j: (l, 0, j)),
                  pl.BlockSpec((1, 1, tn), lambda l, j: (l, 0, j))],
        out_specs=pl.BlockSpec((1, 8, tn), lambda l, j: (l, 0, j)),
        compiler_params=_cparams("parallel", "parallel"),
        name="mod_vectors",
    )(c8, w, b.reshape(L, 1, N))


def _rope_body(pos_ref, inv_ref, cm_ref, sm_ref, cos_ref, sin_ref):
    ang = pos_ref[...].astype(F32) * inv_ref[...]
    cos_ref[...] = jnp.cos(ang) * cm_ref[...]
    sin_ref[...] = jnp.sin(ang) * sm_ref[...]


def _rope_tables(pos_col, inv, cm, sm):
    T = pos_col.shape[0]
    tm = min(T, 1024)
    row = pl.BlockSpec((1, LANES), lambda i: (0, 0))
    return pl.pallas_call(
        _rope_body,
        out_shape=(jax.ShapeDtypeStruct((T, LANES), F32),) * 2,
        grid=(T // tm,),
        in_specs=[pl.BlockSpec((tm, 1), lambda i: (i, 0)), row, row, row],
        out_specs=(pl.BlockSpec((tm, LANES), lambda i: (i, 0)),) * 2,
        compiler_params=_cparams("parallel"),
        name="rope_tables",
    )(pos_col, inv, cm, sm)


def _inproj_body(h_ref, g_ref, sh_ref, sc_ref, w_ref, o_ref, *, tn):
    xn = _norm_mod(h_ref[...], g_ref[...], sh_ref[0], sc_ref[0]).astype(BF16)
    for j in range(w_ref.shape[1] // tn):
        o_ref[:, j * tn:(j + 1) * tn] = _dot(xn, w_ref[:, j * tn:(j + 1) * tn]).astype(BF16)


def _ret_inproj(h, g, sh, sc, w, S):
    T, D = h.shape
    N = w.shape[1]
    tm = min(TOK_TILE, S)
    per_b = S // tm
    vec = pl.BlockSpec((1, 1, D), lambda i: (i // per_b, 0, 0))
    return pl.pallas_call(
        functools.partial(_inproj_body, tn=512),
        out_shape=jax.ShapeDtypeStruct((T, N), BF16),
        grid=(T // tm,),
        in_specs=[pl.BlockSpec((tm, D), lambda i: (i, 0)),
                  pl.BlockSpec((1, D), lambda i: (0, 0)),
                  vec, vec,
                  pl.BlockSpec((D, N), lambda i: (0, 0))],
        out_specs=pl.BlockSpec((tm, N), lambda i: (i, 0)),
        compiler_params=_cparams("parallel"),
        name="ret_inproj",
    )(h, g, sh, sc, w)


def _ret_body(q_ref, k_ref, v_ref, g_ref, cos_ref, sin_ref, di_ref, dq_ref, dk_ref, dc_ref,
              y_ref, state_ref, *, dk_dim):
    @pl.when(pl.program_id(1) == 0)
    def _():
        state_ref[...] = jnp.zeros_like(state_ref)

    cos = cos_ref[...]
    sin = sin_ref[...]
    half = dk_dim // 2
    dv_dim = 2 * dk_dim

    def rope(x):
        x1, x2 = x[:, :half], x[:, half:]
        return jnp.concatenate([x1 * cos - x2 * sin, x1 * sin + x2 * cos], axis=-1)

    for hd in range(RET_HEADS):
        qk = slice(hd * dk_dim, (hd + 1) * dk_dim)
        vg = slice(hd * dv_dim, (hd + 1) * dv_dim)
        qr = rope(q_ref[:, qk].astype(F32))
        kr = rope(k_ref[:, qk].astype(F32)) * (dk_dim ** -0.5)
        qb = qr.astype(BF16)
        v = v_ref[:, vg]
        inner = (_dot_nt(qb, kr.astype(BF16)) * di_ref[hd]).astype(BF16)
        st = state_ref[hd]
        out = _dot(inner, v) + _dot(qb, st.astype(BF16)) * dq_ref[hd]
        kd = (kr * dk_ref[hd]).astype(BF16)
        state_ref[hd] = st * dc_ref[hd] + _dot_tn(kd, v)

        mu = jnp.mean(out, axis=-1, keepdims=True)
        cen = out - mu
        var = jnp.mean(cen * cen, axis=-1, keepdims=True)
        o = cen * lax.rsqrt(var + EPS)
        y_ref[:, vg] = (_silu(g_ref[:, vg].astype(F32)) * o).astype(BF16)


def _retention(proj, cos, sin, S, D):
    T = proj.shape[0]
    B = T // S
    H = RET_HEADS
    dk = D // H
    dv = 2 * dk
    C = min(RET_CHUNK, S)
    n = S // C
    log_g = jnp.log1p(-(2.0 ** (-5.0 - jnp.arange(H, dtype=F32))))
    t = jnp.arange(C, dtype=F32)
    diff = t[:, None] - t[None, :]
    d_intra = jnp.where(diff >= 0, jnp.exp(log_g[:, None, None] * jnp.maximum(diff, 0.0)), 0.0)
    d_q = jnp.exp(log_g[:, None] * (t + 1.0))[:, :, None]
    d_k = jnp.exp(log_g[:, None] * (C - 1.0 - t))[:, :, None]
    d_c = jnp.exp(log_g * C)[:, None, None]

    assert 2 * D == H * dv
    row = lambda b, i: b * n + i
    const = lambda a: pl.BlockSpec(a.shape, lambda b, i: (0, 0, 0))
    return pl.pallas_call(
        functools.partial(_ret_body, dk_dim=dk),
        out_shape=jax.ShapeDtypeStruct((T, H * dv), BF16),
        grid=(B, n),
        in_specs=[pl.BlockSpec((C, D), lambda b, i: (row(b, i), 0)),
                  pl.BlockSpec((C, D), lambda b, i: (row(b, i), 1)),
                  pl.BlockSpec((C, H * dv), lambda b, i: (row(b, i), 1)),
                  pl.BlockSpec((C, H * dv), lambda b, i: (row(b, i), 2)),
                  pl.BlockSpec((C, dk // 2), lambda b, i: (row(b, i), 0)),
                  pl.BlockSpec((C, dk // 2), lambda b, i: (row(b, i), 0)),
                  const(d_intra), const(d_q), const(d_k), const(d_c)],
        out_specs=pl.BlockSpec((C, H * dv), lambda b, i: (row(b, i), 0)),
        scratch_shapes=[pltpu.VMEM((H, dk, dv), F32)],
        compiler_params=_cparams("parallel", "arbitrary"),
        name="retention",
    )(proj, proj, proj, proj, cos, sin, d_intra, d_q, d_k, d_c)


def _route(logits_t, bias):
    sc = jax.nn.sigmoid(logits_t)
    bi = sc + bias
    s_rows = [sc[e:e + 1, :] for e in range(N_EXPERTS)]
    b_rows = [bi[e:e + 1, :] for e in range(N_EXPERTS)]

    def top2sum(a, b, c, d):
        p, q = jnp.maximum(a, b), jnp.minimum(a, b)
        r, s = jnp.maximum(c, d), jnp.minimum(c, d)
        return jnp.maximum(p, r) + jnp.maximum(jnp.minimum(p, r), jnp.maximum(q, s))

    n = EXPERTS_PER_GROUP
    gs = [top2sum(*b_rows[n * g:n * g + n]) for g in range(N_GROUPS)]
    best, gi = gs[0], jnp.zeros_like(gs[0], dtype=jnp.int32)
    for g in range(1, N_GROUPS):
        upd = gs[g] > best
        gi = jnp.where(upd, g, gi)
        best = jnp.where(upd, gs[g], best)

    def pick(rows, j):
        out = rows[j]
        for g in range(1, N_GROUPS):
            out = jnp.where(gi == g, rows[n * g + j], out)
        return out

    vb = [pick(b_rows, j) for j in range(n)]
    vs = [pick(s_rows, j) for j in range(n)]

    def argmax_first(vals):
        best, idx = vals[0], jnp.zeros_like(gi)
        for j in range(1, n):
            upd = vals[j] > best
            idx = jnp.where(upd, j, idx)
            best = jnp.where(upd, vals[j], best)
        return idx

    i1 = argmax_first(vb)
    i2 = argmax_first([jnp.where(i1 == j, -jnp.inf, vb[j]) for j in range(n)])

    def take(vals, idx):
        out = vals[0]
        for j in range(1, n):
            out = jnp.where(idx == j, vals[j], out)
        return out

    w1, w2 = take(vs, i1), take(vs, i2)
    tot = w1 + w2
    w1, w2 = w1 / tot, w2 / tot
    return gi * n + i1, gi * n + i2, w1, w2


class _MoeLayout:
    def __init__(self, T, tm):
        self.tm = tm
        self.n_tiles = T // tm
        self.group = MOE_GROUP
        self.rt = -(-(2 * tm + N_EXPERTS * (MOE_GROUP - 1)) // MXU_DIM) * MXU_DIM
        self.ng = self.rt // MOE_GROUP
        assert self.ng < LANES
        self.tmx = MOE_ROW_TILE
        self.cap = -(-(T + self.n_tiles * MOE_GROUP) // self.tmx) * self.tmx
        self.dump = N_EXPERTS * self.cap
        self.rows = self.dump + self.rt
        pad = self.n_tiles * N_EXPERTS * (MOE_GROUP - 1)
        self.nt = (2 * T + pad) // self.tmx + N_EXPERTS


def _dispatch_meta(e1, e2, cum, tri, lay):
    E, G = N_EXPERTS, lay.group
    tm = e1.shape[1]
    eid = lax.broadcasted_iota(jnp.int32, (E, tm), 0)
    oh1, oh2 = eid == e1, eid == e2
    cnt = jnp.where(oh1 | oh2, 1.0, 0.0)
    pre = _dot(cnt.astype(BF16), tri)
    tot = jnp.sum(cnt, axis=1, keepdims=True)
    ptot = jnp.broadcast_to(jnp.ceil(tot * (1.0 / G)) * G, (E, LANES))
    below = jnp.where(lax.broadcasted_iota(jnp.int32, (E, E), 0) > lax.broadcasted_iota(jnp.int32, (E, E), 1),
                      1.0, 0.0)
    loff = jnp.dot(below, ptot, preferred_element_type=F32, precision=lax.Precision.HIGHEST)
    pos_e = loff[:, :1] + pre
    pos1 = jnp.sum(jnp.where(oh1, pos_e, 0.0), axis=0, keepdims=True)
    pos2 = jnp.sum(jnp.where(oh2, pos_e, 0.0), axis=0, keepdims=True)

    lane = lax.broadcasted_iota(jnp.int32, (E, LANES), 1)
    g_row = (lane * G).astype(F32)
    eg = jnp.sum(jnp.where(loff + ptot <= g_row, 1, 0), axis=0, keepdims=True)
    erow = lax.broadcasted_iota(jnp.int32, (E, LANES), 0)
    base = erow.astype(F32) * float(lay.cap) + cum - loff
    sel = jnp.sum(jnp.where(erow == eg, base, 0.0), axis=0, keepdims=True)
    dst = jnp.where(eg < E, g_row[:1] + sel, float(lay.dump) + g_row[:1])
    n_used = jnp.sum(ptot[:, :1], axis=0, keepdims=True) * (1.0 / G)
    table = jnp.where(lane[:1] == LANES - 1, n_used, dst).astype(jnp.int32)
    return pos1, pos2, table, cum + ptot


def _outproj_body(y_ref, w_ref, h_ref, g1_ref, gn_ref, sh_ref, sc_ref, wr_ref, br_ref, tri_ref,
                  ho_ref, xn_ref, rows_ref, cols_ref, tbl_ref, cum_ref, *, lay):
    @pl.when(pl.program_id(0) == 0)
    def _():
        cum_ref[...] = jnp.zeros_like(cum_ref)

    hn = h_ref[...] + g1_ref[0] * _dot(y_ref[...], w_ref[...])
    ho_ref[...] = hn
    xn = _norm_mod(hn, gn_ref[...], sh_ref[0], sc_ref[0])
    xb = xn.astype(BF16)
    xn_ref[...] = xb
    xl = (xn - xb.astype(F32)).astype(BF16)
    hl = _dot_nt(wr_ref[...], xb)
    logits_t = hl[:N_EXPERTS] + hl[N_EXPERTS:] + _dot_nt(wr_ref[:N_EXPERTS, :], xl)
    e1, e2, w1, w2 = _route(logits_t, br_ref[...])
    pos1, pos2, table, cum = _dispatch_meta(e1, e2, cum_ref[...], tri_ref[...], lay)
    cum_ref[...] = cum
    tbl_ref[0] = table
    tm = e1.shape[1]
    rows = jnp.concatenate([pos1, pos2, w1, w2, jnp.zeros((SUBLANES - 4, tm), F32)], axis=0)
    rows_ref[...] = rows
    cols_ref[...] = jnp.concatenate([rows, jnp.zeros((LANES - SUBLANES, tm), F32)], axis=0).T


def _outproj_route(y, w, h, g1, gn, sh, sc, wr_t, br, S, lay):
    T, D = h.shape
    K = y.shape[1]
    tm = lay.tm
    per_b = S // tm
    vec = pl.BlockSpec((1, 1, D), lambda i: (i // per_b, 0, 0))
    tok = lambda n: pl.BlockSpec((tm, n), lambda i: (i, 0))
    tri = jnp.triu(jnp.ones((tm, tm), BF16), k=1)
    return pl.pallas_call(
        functools.partial(_outproj_body, lay=lay),
        out_shape=(jax.ShapeDtypeStruct((T, D), F32),
                   jax.ShapeDtypeStruct((T, D), BF16),
                   jax.ShapeDtypeStruct((SUBLANES, T), F32),
                   jax.ShapeDtypeStruct((T, LANES), F32),
                   jax.ShapeDtypeStruct((lay.n_tiles, 1, LANES), jnp.int32),
                   jax.ShapeDtypeStruct((N_EXPERTS, LANES), F32)),
        grid=(T // tm,),
        in_specs=[tok(K),
                  pl.BlockSpec((K, D), lambda i: (0, 0)),
                  tok(D), vec,
                  pl.BlockSpec((1, D), lambda i: (0, 0)),
                  vec, vec,
                  pl.BlockSpec((2 * N_EXPERTS, D), lambda i: (0, 0)),
                  pl.BlockSpec((N_EXPERTS, 1), lambda i: (0, 0)),
                  pl.BlockSpec((tm, tm), lambda i: (0, 0))],
        out_specs=(tok(D), tok(D),
                   pl.BlockSpec((SUBLANES, tm), lambda i: (0, i)),
                   tok(LANES),
                   pl.BlockSpec((1, 1, LANES), lambda i: (i, 0, 0)),
                   pl.BlockSpec((N_EXPERTS, LANES), lambda i: (0, 0))),
        compiler_params=_cparams("arbitrary"),
        name="outproj_route",
    )(y, w, h, g1, gn, sh, sc, wr_t, br, tri)


def _group_copy(hbm_ref, buf_ref, tbl_ref, i, g, sem, lay, to_hbm):
    G = lay.group
    hbm = hbm_ref.at[pl.ds(pl.multiple_of(tbl_ref[i * LANES + g], G), G), :]
    vmem = buf_ref.at[pl.ds(pl.multiple_of(g * G, G), G), :]
    return pltpu.make_async_copy(vmem, hbm, sem) if to_hbm else pltpu.make_async_copy(hbm, vmem, sem)


def _dispatch_body(tbl_ref, x_ref, rows_ref, xs_ref, buf_ref, sem, *, lay):
    i = pl.program_id(0)
    pos1 = rows_ref[0:1, :].astype(jnp.int32)
    pos2 = rows_ref[1:2, :].astype(jnp.int32)
    r = lax.broadcasted_iota(jnp.int32, (lay.rt, lay.tm), 0)
    perm = jnp.where((r == pos1) | (r == pos2), 1.0, 0.0).astype(BF16)
    buf_ref[...] = _dot(perm, x_ref[...]).astype(BF16)
    for g in range(lay.ng):
        _group_copy(xs_ref, buf_ref, tbl_ref, i, g, sem, lay, True).start()
    for g in range(lay.ng):
        _group_copy(xs_ref, buf_ref, tbl_ref, i, g, sem, lay, True).wait()


def _moe_dispatch(xn, rows, tbl, lay):
    T, D = xn.shape
    tm = lay.tm
    grid_spec = pltpu.PrefetchScalarGridSpec(
        num_scalar_prefetch=1,
        grid=(lay.n_tiles,),
        in_specs=[pl.BlockSpec((tm, D), lambda i, tbl: (i, 0)),
                  pl.BlockSpec((SUBLANES, tm), lambda i, tbl: (0, i))],
        out_specs=pl.BlockSpec(memory_space=pl.ANY),
        scratch_shapes=[pltpu.VMEM((lay.rt, D), BF16), pltpu.SemaphoreType.DMA(())],
    )
    return pl.pallas_call(
        functools.partial(_dispatch_body, lay=lay),
        out_shape=jax.ShapeDtypeStruct((lay.rows, D), BF16),
        grid_spec=grid_spec,
        compiler_params=_cparams("arbitrary"),
        name="moe_dispatch",
    )(tbl, xn, rows)


def _expert_body(te_ref, tb_ref, nv_ref, x_ref, wg_ref, wu_ref, wd_ref, y_ref, wgu_bf, wd_bf):
    n = pl.program_id(0)
    f = wg_ref.shape[2]

    @pl.when((n == 0) | (te_ref[n] != te_ref[jnp.maximum(n - 1, 0)]))
    def _():
        wgu_bf[:, :f] = wg_ref[0].astype(BF16)
        wgu_bf[:, f:] = wu_ref[0].astype(BF16)
        wd_bf[...] = wd_ref[0].astype(BF16)

    @pl.when(n < nv_ref[0])
    def _():
        hgu = _dot(x_ref[...], wgu_bf[...])
        hdn = (_silu(hgu[:, :f]) * hgu[:, f:]).astype(BF16)
        y_ref[...] = _dot(hdn, wd_bf[...]).astype(BF16)


def _moe_experts(xs, wg, wu, wd, layer, tile_e, tile_blk, n_valid, lay):
    R, D = xs.shape
    F = wg.shape[3]
    tmx = lay.tmx
    wspec = lambda shape: pl.BlockSpec((None, 1) + shape, lambda n, te, tb, nv: (layer, te[n], 0, 0))
    grid_spec = pltpu.PrefetchScalarGridSpec(
        num_scalar_prefetch=3,
        grid=(lay.nt,),
        in_specs=[pl.BlockSpec((tmx, D), lambda n, te, tb, nv: (tb[n], 0)),
                  wspec((D, F)), wspec((D, F)), wspec((F, D))],
        out_specs=pl.BlockSpec((tmx, D), lambda n, te, tb, nv: (tb[n], 0)),
        scratch_shapes=[pltpu.VMEM((D, 2 * F), BF16), pltpu.VMEM((F, D), BF16)],
    )
    return pl.pallas_call(
        _expert_body,
        out_shape=jax.ShapeDtypeStruct((R, D), BF16),
        grid_spec=grid_spec,
        compiler_params=_cparams("arbitrary"),
        name="moe_experts",
    )(tile_e, tile_blk, n_valid, xs, wg, wu, wd)


def _combine_body(tbl_ref, cols_ref, h_ref, g2_ref, ys_ref, o_ref, buf_ref, sem, *, lay):
    i = pl.program_id(0)

    @pl.when(i == 0)
    def _():
        buf_ref[...] = jnp.zeros_like(buf_ref)

    n_used = tbl_ref[i * LANES + LANES - 1]

    def start(g, _):
        _group_copy(ys_ref, buf_ref, tbl_ref, i, g, sem, lay, False).start()
        return 0

    def wait(g, _):
        _group_copy(ys_ref, buf_ref, tbl_ref, i, g, sem, lay, False).wait()
        return 0

    lax.fori_loop(0, n_used, start, 0)
    lax.fori_loop(0, n_used, wait, 0)

    cols = cols_ref[...]
    r = lax.broadcasted_iota(jnp.int32, (lay.tm, lay.rt), 1)
    mix = (jnp.where(r == cols[:, 0:1].astype(jnp.int32), cols[:, 2:3], 0.0)
           + jnp.where(r == cols[:, 1:2].astype(jnp.int32), cols[:, 3:4], 0.0)).astype(BF16)
    o_ref[...] = h_ref[...] + g2_ref[0] * _dot(mix, buf_ref[...])


def _moe_combine(ys, cols, tbl, h, g2, S, lay):
    T, D = h.shape
    tm = lay.tm
    per_b = S // tm
    grid_spec = pltpu.PrefetchScalarGridSpec(
        num_scalar_prefetch=1,
        grid=(lay.n_tiles,),
        in_specs=[pl.BlockSpec((tm, LANES), lambda i, tbl: (i, 0)),
                  pl.BlockSpec((tm, D), lambda i, tbl: (i, 0)),
                  pl.BlockSpec((1, 1, D), lambda i, tbl: (i // per_b, 0, 0)),
                  pl.BlockSpec(memory_space=pl.ANY)],
        out_specs=pl.BlockSpec((tm, D), lambda i, tbl: (i, 0)),
        scratch_shapes=[pltpu.VMEM((lay.rt, D), BF16), pltpu.SemaphoreType.DMA(())],
    )
    return pl.pallas_call(
        functools.partial(_combine_body, lay=lay),
        out_shape=jax.ShapeDtypeStruct((T, D), F32),
        grid_spec=grid_spec,
        compiler_params=_cparams("arbitrary"),
        name="moe_combine",
    )(tbl, cols, h, g2, ys)


def _expert_tiles(used, lay):
    tiles = jnp.ceil(used / lay.tmx).astype(jnp.int32)
    ends = jnp.cumsum(tiles)
    n_valid = ends[-1]
    n = jnp.minimum(jnp.arange(lay.nt, dtype=jnp.int32), n_valid - 1)
    e = jnp.sum((ends[None, :] <= n[:, None]).astype(jnp.int32), axis=1)
    blk = e * (lay.cap // lay.tmx) + n - (ends - tiles)[e]
    return e, blk, n_valid.reshape(1)


def _moe(xn, rows, cols, tbl, used, wg, wu, wd, layer, h, g2, S, lay):
    tbl = tbl.reshape(-1)
    xs = _moe_dispatch(xn, rows, tbl, lay)
    ys = _moe_experts(xs, wg, wu, wd, layer, *_expert_tiles(used, lay), lay)
    return _moe_combine(ys, cols, tbl, h, g2, S, lay)


def _kv_body(h_ref, g_ref, sh_ref, sc_ref, wa_ref, gkv_ref, wb_ref, cos_ref, sin_ref,
             k_ref, vt_ref):
    hn = _norm_mod(h_ref[...], g_ref[...], sh_ref[0], sc_ref[0]).astype(BF16)
    a = _dot(hn, wa_ref[...])
    c_kv = _rms(a[:, :KV_LORA], gkv_ref[...]).astype(BF16)
    kr = (a[:, KV_LORA:KV_LORA + LANES] * cos_ref[...]
          + a[:, KV_LORA + LANES:] * sin_ref[...]).astype(BF16)
    kv = _dot(c_kv, wb_ref[...])
    hk = MLA_HEADS * QK_NOPE
    w = QK_NOPE + LANES
    for hd in range(MLA_HEADS):
        k_ref[:, hd * w:hd * w + QK_NOPE] = kv[:, hd * QK_NOPE:(hd + 1) * QK_NOPE].astype(BF16)
        k_ref[:, hd * w + QK_NOPE:(hd + 1) * w] = kr
        vh = kv[:, hk + hd * V_HEAD:hk + (hd + 1) * V_HEAD]
        for g in range(vt_ref.shape[1]):
            vt_ref[hd, g] = vh[g * ATT_KEYS:(g + 1) * ATT_KEYS, :].T.astype(BF16)


def _mla_kv(h, g, sh, sc, wa, gkv, wb, cos, sin, S):
    T, D = h.shape
    tm = min(TOK_TILE, S)
    per_b = S // tm
    vec = pl.BlockSpec((1, 1, D), lambda i: (i // per_b, 0, 0))
    tok = lambda n: pl.BlockSpec((tm, n), lambda i: (i, 0))
    full = lambda a: pl.BlockSpec(a.shape, lambda i: (0, 0))
    kw = MLA_HEADS * (QK_NOPE + LANES)
    gk = tm // ATT_KEYS
    vt_shape = (MLA_HEADS, T // ATT_KEYS, V_HEAD, ATT_KEYS)
    return pl.pallas_call(
        _kv_body,
        out_shape=(jax.ShapeDtypeStruct((T, kw), BF16), jax.ShapeDtypeStruct(vt_shape, BF16)),
        grid=(T // tm,),
        in_specs=[tok(D), full(g), vec, vec, full(wa), full(gkv), full(wb), tok(LANES), tok(LANES)],
        out_specs=(tok(kw),
                   pl.BlockSpec((MLA_HEADS, gk, V_HEAD, ATT_KEYS), lambda i: (0, i, 0, 0))),
        compiler_params=_cparams("parallel"),
        name="mla_kv",
    )(h, g, sh, sc, wa, gkv, wb, cos, sin)


def _q_body(h_ref, g_ref, sh_ref, sc_ref, wa_ref, gq_ref, wb_ref, cos_ref, sin_ref, q_ref, *, scale):
    xn = _norm_mod(h_ref[...], g_ref[...], sh_ref[0], sc_ref[0]).astype(BF16)
    qa = _rms(_dot(xn, wa_ref[...]), gq_ref[...]).astype(BF16)
    cos = cos_ref[...]
    sin = sin_ref[...]
    wi = QK_NOPE + 2 * LANES
    wo = QK_NOPE + LANES
    for hd in range(MLA_HEADS):
        qb = _dot(qa, wb_ref[:, hd * wi:(hd + 1) * wi])
        q_ref[:, hd * wo:hd * wo + QK_NOPE] = (qb[:, :QK_NOPE] * scale).astype(BF16)
        rp = qb[:, QK_NOPE:QK_NOPE + LANES] * cos + qb[:, QK_NOPE + LANES:] * sin
        q_ref[:, hd * wo + QK_NOPE:(hd + 1) * wo] = (rp * scale).astype(BF16)


def _mla_q(h, g, sh, sc, wa, gq, wb, cos, sin, S):
    T, D = h.shape
    tm = min(TOK_TILE, S)
    per_b = S // tm
    vec = pl.BlockSpec((1, 1, D), lambda i: (i // per_b, 0, 0))
    tok = lambda n: pl.BlockSpec((tm, n), lambda i: (i, 0))
    full = lambda a: pl.BlockSpec(a.shape, lambda i: (0, 0))
    qw = MLA_HEADS * (QK_NOPE + LANES)
    return pl.pallas_call(
        functools.partial(_q_body, scale=(QK_NOPE + QK_ROPE) ** -0.5 * LOG2E),
        out_shape=jax.ShapeDtypeStruct((T, qw), BF16),
        grid=(T // tm,),
        in_specs=[tok(D), full(g), vec, vec, full(wa), full(gq), full(wb), tok(LANES), tok(LANES)],
        out_specs=tok(qw),
        compiler_params=_cparams("parallel"),
        name="mla_q",
    )(h, g, sh, sc, wa, gq, wb, cos, sin)


def _flash_body(q_ref, k_ref, vt_ref, o_ref, s_ref, smax_ref, m_ref, l_ref, acc_ref, *, tq, ks):
    i = pl.program_id(2)
    last = (i * tq + tq - 1) // ks

    def scores(slot, c):
        kc = k_ref[pl.ds(pl.multiple_of(c * ks, ks), ks), :]
        s = _dot_nt(kc, q_ref[...])
        s_ref[slot] = s
        smax_ref[slot] = jnp.max(s, axis=0, keepdims=True)

    def update(slot, c, masked):
        if masked:
            s = s_ref[slot]
            kch = (c * ks + lax.broadcasted_iota(jnp.int32, s.shape, 0)) // CHUNK
            qch = (i * tq + lax.broadcasted_iota(jnp.int32, s.shape, 1)) // CHUNK
            s_ref[slot] = jnp.where(kch <= qch, s, NEG_BIG)
            smax = jnp.max(s_ref[slot], axis=0, keepdims=True)
        else:
            smax = smax_ref[slot]
        m = m_ref[...]
        m_new = jnp.maximum(m, smax)
        alpha = jnp.exp2(m - m_new)
        part = jnp.zeros((SUBLANES, tq), F32)
        ps = []
        for r in range(ks // ATT_SLAB):
            p = jnp.exp2(s_ref[slot, r * ATT_SLAB:(r + 1) * ATT_SLAB, :] - m_new)
            for r8 in range(ATT_SLAB // SUBLANES):
                part = part + p[r8 * SUBLANES:(r8 + 1) * SUBLANES, :]
            ps.append(p.astype(BF16))
        m_ref[...] = m_new
        l_ref[...] = alpha * l_ref[...] + jnp.sum(part, axis=0, keepdims=True)
        acc_ref[...] = alpha * acc_ref[...] + _dot(vt_ref[0, c], jnp.concatenate(ps, axis=0))

    m_ref[...] = jnp.full_like(m_ref, NEG_BIG)
    l_ref[...] = jnp.zeros_like(l_ref)
    acc_ref[...] = jnp.zeros_like(acc_ref)
    scores(0, 0)

    def pair(g, _):
        scores(1, 2 * g + 1)
        update(0, 2 * g, False)
        scores(0, 2 * g + 2)
        update(1, 2 * g + 1, False)
        return 0

    lax.fori_loop(0, last // 2, pair, 0)

    @pl.when(last % 2 == 1)
    def _():
        scores(1, last)
        update(0, last - 1, False)
        update(1, last, True)

    @pl.when(last % 2 == 0)
    def _():
        update(0, last, True)

    o_ref[...] = (acc_ref[...] / l_ref[...]).T.astype(o_ref.dtype)


def _flash(q, k, vt, S):
    T = q.shape[0]
    B = T // S
    H = MLA_HEADS
    tq = min(ATT_TILE, S)
    ks = ATT_KEYS
    nq = S // tq
    dq = QK_NOPE + LANES
    return pl.pallas_call(
        functools.partial(_flash_body, tq=tq, ks=ks),
        out_shape=jax.ShapeDtypeStruct((T, H * V_HEAD), BF16),
        grid=(B, H, nq),
        in_specs=[pl.BlockSpec((tq, dq), lambda b, h, i: (b * nq + i, h)),
                  pl.BlockSpec((S, dq), lambda b, h, i: (b, h)),
                  pl.BlockSpec((1, S // ks, V_HEAD, ks), lambda b, h, i: (h, b, 0, 0))],
        out_specs=pl.BlockSpec((tq, V_HEAD), lambda b, h, i: (b * nq + i, h)),
        scratch_shapes=[pltpu.VMEM((2, ks, tq), F32), pltpu.VMEM((2, 1, tq), F32),
                        pltpu.VMEM((1, tq), F32), pltpu.VMEM((1, tq), F32),
                        pltpu.VMEM((V_HEAD, tq), F32)],
        compiler_params=_cparams("parallel", "parallel", "arbitrary"),
        name="flash_attn",
    )(q, k, vt)


def _final_body(h_ref, g_ref, o_ref):
    o_ref[...] = _rms(h_ref[...], g_ref[...])


def _final_norm(h, g):
    T, D = h.shape
    tm = min(T, 1024)
    return pl.pallas_call(
        _final_body,
        out_shape=jax.ShapeDtypeStruct((T, D), F32),
        grid=(T // tm,),
        in_specs=[pl.BlockSpec((tm, D), lambda i: (i, 0)), pl.BlockSpec((1, D), lambda i: (0, 0))],
        out_specs=pl.BlockSpec((tm, D), lambda i: (i, 0)),
        compiler_params=_cparams("parallel"),
        name="final_norm",
    )(h, g)


def _rope_swap(w):
    half = w.shape[-1] // 2
    return jnp.concatenate([w[..., half:], w[..., :half]], axis=-1)


def _pad_lanes(w):
    return jnp.pad(w, [(0, 0)] * (w.ndim - 1) + [(0, LANES - w.shape[-1])])


def kernel(x, c, positions, w_mod, b_mod, norm_mix, norm_ffn, ret_w_in, ret_w_out, w_mod_kv, b_mod_kv, norm_kv, mla_w_kv_a, mla_kv_norm, mla_w_kv_b, mla_w_q_a, mla_q_norm, mla_w_q_b, mla_w_o, router_w, router_b, moe_w_gate, moe_w_up, moe_w_down, final_norm):
    B, S, D = x.shape
    T = B * S
    depth = w_mod.shape[0]
    n_a = ret_w_in.shape[0]

    c8 = jnp.pad(c, ((0, 8 - B), (0, 0)))
    mod = _mod_vectors(c8, w_mod, b_mod)[:, :B]
    mod = mod.reshape(depth, B, 6, 1, D)
    kv_mod = _mod_vectors(c8, w_mod_kv[None], b_mod_kv[None])[0, :B].reshape(B, 2, 1, D)

    pos_col = positions.reshape(T, 1)
    ones = jnp.ones((1, LANES), F32)
    inv_ret = (ROPE_THETA ** (-jnp.arange(LANES, dtype=F32) / LANES)).reshape(1, LANES)
    cos_r, sin_r = _rope_tables(pos_col, inv_ret, ones, ones)
    hr = QK_ROPE // 2
    inv_m = ROPE_THETA ** (-jnp.arange(hr, dtype=F32) / hr)
    inv_m = _pad_lanes(jnp.concatenate([inv_m, inv_m])[None])
    cm = _pad_lanes(jnp.ones((1, QK_ROPE), F32))
    sm = _pad_lanes(jnp.concatenate([-jnp.ones((1, hr), F32), jnp.ones((1, hr), F32)], axis=-1))
    cos_m, sin_m = _rope_tables(pos_col, inv_m, cm, sm)

    wr_hi = router_w.T.astype(BF16)
    wr_t = jnp.concatenate([wr_hi, (router_w.T - wr_hi.astype(F32)).astype(BF16)], axis=0)
    br = router_b.reshape(N_EXPERTS, 1)

    h = x.reshape(T, D)
    lay = _MoeLayout(T, min(TOK_TILE, S))
    k_full = v_full = None
    for layer in range(depth):
        sh1, sc1, g1, sh2, sc2, g2 = (mod[layer, :, i] for i in range(6))
        gmix = norm_mix[layer].reshape(1, D)
        if layer < n_a:
            proj = _ret_inproj(h, gmix, sh1, sc1, ret_w_in[layer].astype(BF16), S)
            mix = _retention(proj, cos_r, sin_r, S, D)
            w_o = ret_w_out[layer].astype(BF16)
        else:
            if layer == n_a:
                wa = mla_w_kv_a
                wa_r = wa[:, KV_LORA:]
                wa_p = jnp.concatenate([wa[:, :KV_LORA], _pad_lanes(wa_r), _pad_lanes(_rope_swap(wa_r))],
                                       axis=-1).astype(BF16)
                wb = mla_w_kv_b.reshape(KV_LORA, MLA_HEADS, QK_NOPE + V_HEAD)
                wb_p = jnp.concatenate([wb[..., :QK_NOPE].reshape(KV_LORA, -1),
                                        wb[..., QK_NOPE:].reshape(KV_LORA, -1)], axis=-1).astype(BF16)
                k_full, v_full = _mla_kv(h, norm_kv.reshape(1, D), kv_mod[:, 0], kv_mod[:, 1], wa_p,
                                         mla_kv_norm.reshape(1, KV_LORA), wb_p, cos_m, sin_m, S)
            j = layer - n_a
            wq = mla_w_q_b[j].reshape(Q_LORA, MLA_HEADS, QK_NOPE + QK_ROPE)
            wq_r = wq[..., QK_NOPE:]
            wq_p = jnp.concatenate([wq[..., :QK_NOPE], _pad_lanes(wq_r), _pad_lanes(_rope_swap(wq_r))],
                                   axis=-1).reshape(Q_LORA, -1).astype(BF16)
            q_full = _mla_q(h, gmix, sh1, sc1, mla_w_q_a[j].astype(BF16),
                            mla_q_norm[j].reshape(1, Q_LORA), wq_p, cos_m, sin_m, S)
            mix = _flash(q_full, k_full, v_full, S)
            w_o = mla_w_o[j].astype(BF16)
        h, xn, rows, cols, tbl, used = _outproj_route(mix, w_o, h, g1, norm_ffn[layer].reshape(1, D),
                                                      sh2, sc2, wr_t, br, S, lay)
        h = _moe(xn, rows, cols, tbl, used[:, 0], moe_w_gate, moe_w_up, moe_w_down, layer,
                 h, g2, S, lay)
    return _final_norm(h, final_norm.reshape(1, D)).reshape(B, S, D)
```

```python
import functools

import jax
import jax.numpy as jnp
from jax import lax
from jax.experimental import pallas as pl
from jax.experimental.pallas import tpu as pltpu

F32 = jnp.float32
BF16 = jnp.bfloat16

CHUNK = 64
RET_HEADS = 4
MLA_HEADS = 8
QK_NOPE = 128
QK_ROPE = 64
V_HEAD = 128
Q_LORA = 256
KV_LORA = 128
N_EXPERTS = 16
N_GROUPS = 4
EXPERTS_PER_GROUP = N_EXPERTS // N_GROUPS
D_EXPERT = 512
ROPE_THETA = 10000.0
EPS = 1e-6

LANES = 128
VMEM_LIMIT = 56 * 1024 * 1024
NEG_BIG = -1e30
LOG2E = 1.4426950408889634

RET_CHUNK = 256
TOK_TILE = 512
ATT_TILE = 512
ATT_KEYS = 512
ATT_SLAB = 32
SUBLANES = 8
MXU_DIM = 256
MOE_GROUP = 16
MOE_ROW_TILE = 512


def _cparams(*sem):
    return pltpu.CompilerParams(dimension_semantics=sem, vmem_limit_bytes=VMEM_LIMIT)


def _silu(x):
    return x * jax.nn.sigmoid(x)


def _rms(x, g):
    return x * lax.rsqrt(jnp.mean(x * x, axis=-1, keepdims=True) + EPS) * g


def _norm_mod(h, g, shift, scale):
    return _rms(h, g) * (1.0 + scale) + shift


def _dot(a, b):
    return jnp.dot(a, b, preferred_element_type=F32)


def _dot_nt(a, b, **kw):
    return lax.dot_general(a, b, (((1,), (1,)), ((), ())), preferred_element_type=F32, **kw)


def _dot_tn(a, b):
    return lax.dot_general(a, b, (((0,), (0,)), ((), ())), preferred_element_type=F32)


def _mod_body(c_ref, w_ref, b_ref, o_ref):
    ca = _silu(c_ref[...])
    o_ref[0] = jnp.dot(ca, w_ref[0], preferred_element_type=F32,
                       precision=lax.Precision.HIGHEST) + b_ref[0]


def _mod_vectors(c8, w, b):
    L, D, N = w.shape
    tn = D
    assert N % tn == 0
    return pl.pallas_call(
        _mod_body,
        out_shape=jax.ShapeDtypeStruct((L, 8, N), F32),
        grid=(L, N // tn),
        in_specs=[pl.BlockSpec((8, D), lambda l, j: (0, 0)),
                  pl.BlockSpec((1, D, tn), lambda l, j: (l, 0, j)),
                  pl.BlockSpec((1, 1, tn), lambda l, j: (l, 0, j))],
        out_specs=pl.BlockSpec((1, 8, tn), lambda l, j: (l, 0, j)),
        compiler_params=_cparams("parallel", "parallel"),
        name="mod_vectors",
    )(c8, w, b.reshape(L, 1, N))


def _rope_body(pos_ref, inv_ref, cm_ref, sm_ref, cos_ref, sin_ref):
    ang = pos_ref[...].astype(F32) * inv_ref[...]
    cos_ref[...] = jnp.cos(ang) * cm_ref[...]
    sin_ref[...] = jnp.sin(ang) * sm_ref[...]


def _rope_tables(pos_col, inv, cm, sm):
    T = pos_col.shape[0]
    tm = min(T, 1024)
    row = pl.BlockSpec((1, LANES), lambda i: (0, 0))
    return pl.pallas_call(
        _rope_body,
        out_shape=(jax.ShapeDtypeStruct((T, LANES), F32),) * 2,
        grid=(T // tm,),
        in_specs=[pl.BlockSpec((tm, 1), lambda i: (i, 0)), row, row, row],
        out_specs=(pl.BlockSpec((tm, LANES), lambda i: (i, 0)),) * 2,
        compiler_params=_cparams("parallel"),
        name="rope_tables",
    )(pos_col, inv, cm, sm)


def _inproj_body(h_ref, g_ref, sh_ref, sc_ref, w_ref, o_ref, *, tn):
    xn = _norm_mod(h_ref[...], g_ref[...], sh_ref[0], sc_ref[0]).astype(BF16)
    for j in range(w_ref.shape[1] // tn):
        o_ref[:, j * tn:(j + 1) * tn] = _dot(xn, w_ref[:, j * tn:(j + 1) * tn]).astype(BF16)


def _ret_inproj(h, g, sh, sc, w, S):
    T, D = h.shape
    N = w.shape[1]
    tm = min(TOK_TILE, S)
    per_b = S // tm
    vec = pl.BlockSpec((1, 1, D), lambda i: (i // per_b, 0, 0))
    return pl.pallas_call(
        functools.partial(_inproj_body, tn=512),
        out_shape=jax.ShapeDtypeStruct((T, N), BF16),
        grid=(T // tm,),
        in_specs=[pl.BlockSpec((tm, D), lambda i: (i, 0)),
                  pl.BlockSpec((1, D), lambda i: (0, 0)),
                  vec, vec,
                  pl.BlockSpec((D, N), lambda i: (0, 0))],
        out_specs=pl.BlockSpec((tm, N), lambda i: (i, 0)),
        compiler_params=_cparams("parallel"),
        name="ret_inproj",
    )(h, g, sh, sc, w)


def _ret_body(q_ref, k_ref, v_ref, g_ref, cos_ref, sin_ref, di_ref, dq_ref, dk_ref, dc_ref,
              y_ref, state_ref, *, dk_dim):
    @pl.when(pl.program_id(1) == 0)
    def _():
        state_ref[...] = jnp.zeros_like(state_ref)

    cos = cos_ref[...]
    sin = sin_ref[...]
    half = dk_dim // 2
    dv_dim = 2 * dk_dim

    def rope(x):
        x1, x2 = x[:, :half], x[:, half:]
        return jnp.concatenate([x1 * cos - x2 * sin, x1 * sin + x2 * cos], axis=-1)

    for hd in range(RET_HEADS):
        qk = slice(hd * dk_dim, (hd + 1) * dk_dim)
        vg = slice(hd * dv_dim, (hd + 1) * dv_dim)
        qr = rope(q_ref[:, qk].astype(F32))
        kr = rope(k_ref[:, qk].astype(F32)) * (dk_dim ** -0.5)
        qb = qr.astype(BF16)
        v = v_ref[:, vg]
        inner = (_dot_nt(qb, kr.astype(BF16)) * di_ref[hd]).astype(BF16)
        st = state_ref[hd]
        out = _dot(inner, v) + _dot(qb, st.astype(BF16)) * dq_ref[hd]
        kd = (kr * dk_ref[hd]).astype(BF16)
        state_ref[hd] = st * dc_ref[hd] + _dot_tn(kd, v)

        mu = jnp.mean(out, axis=-1, keepdims=True)
        cen = out - mu
        var = jnp.mean(cen * cen, axis=-1, keepdims=True)
        o = cen * lax.rsqrt(var + EPS)
        y_ref[:, vg] = (_silu(g_ref[:, vg].astype(F32)) * o).astype(BF16)


def _retention(proj, cos, sin, S, D):
    T = proj.shape[0]
    B = T // S
    H = RET_HEADS
    dk = D // H
    dv = 2 * dk
    C = min(RET_CHUNK, S)
    n = S // C
    log_g = jnp.log1p(-(2.0 ** (-5.0 - jnp.arange(H, dtype=F32))))
    t = jnp.arange(C, dtype=F32)
    diff = t[:, None] - t[None, :]
    d_intra = jnp.where(diff >= 0, jnp.exp(log_g[:, None, None] * jnp.maximum(diff, 0.0)), 0.0)
    d_q = jnp.exp(log_g[:, None] * (t + 1.0))[:, :, None]
    d_k = jnp.exp(log_g[:, None] * (C - 1.0 - t))[:, :, None]
    d_c = jnp.exp(log_g * C)[:, None, None]

    assert 2 * D == H * dv
    row = lambda b, i: b * n + i
    const = lambda a: pl.BlockSpec(a.shape, lambda b, i: (0, 0, 0))
    return pl.pallas_call(
        functools.partial(_ret_body, dk_dim=dk),
        out_shape=jax.ShapeDtypeStruct((T, H * dv), BF16),
        grid=(B, n),
        in_specs=[pl.BlockSpec((C, D), lambda b, i: (row(b, i), 0)),
                  pl.BlockSpec((C, D), lambda b, i: (row(b, i), 1)),
                  pl.BlockSpec((C, H * dv), lambda b, i: (row(b, i), 1)),
                  pl.BlockSpec((C, H * dv), lambda b, i: (row(b, i), 2)),
                  pl.BlockSpec((C, dk // 2), lambda b, i: (row(b, i), 0)),
                  pl.BlockSpec((C, dk // 2), lambda b, i: (row(b, i), 0)),
                  const(d_intra), const(d_q), const(d_k), const(d_c)],
        out_specs=pl.BlockSpec((C, H * dv), lambda b, i: (row(b, i), 0)),
        scratch_shapes=[pltpu.VMEM((H, dk, dv), F32)],
        compiler_params=_cparams("parallel", "arbitrary"),
        name="retention",
    )(proj, proj, proj, proj, cos, sin, d_intra, d_q, d_k, d_c)


def _route(logits_t, bias):
    sc = jax.nn.sigmoid(logits_t)
    bi = sc + bias
    s_rows = [sc[e:e + 1, :] for e in range(N_EXPERTS)]
    b_rows = [bi[e:e + 1, :] for e in range(N_EXPERTS)]

    def top2sum(a, b, c, d):
        p, q = jnp.maximum(a, b), jnp.minimum(a, b)
        r, s = jnp.maximum(c, d), jnp.minimum(c, d)
        return jnp.maximum(p, r) + jnp.maximum(jnp.minimum(p, r), jnp.maximum(q, s))

    n = EXPERTS_PER_GROUP
    gs = [top2sum(*b_rows[n * g:n * g + n]) for g in range(N_GROUPS)]
    best, gi = gs[0], jnp.zeros_like(gs[0], dtype=jnp.int32)
    for g in range(1, N_GROUPS):
        upd = gs[g] > best
        gi = jnp.where(upd, g, gi)
        best = jnp.where(upd, gs[g], best)

    def pick(rows, j):
        out = rows[j]
        for g in range(1, N_GROUPS):
            out = jnp.where(gi == g, rows[n * g + j], out)
        return out

    vb = [pick(b_rows, j) for j in range(n)]
    vs = [pick(s_rows, j) for j in range(n)]

    def argmax_first(vals):
        best, idx = vals[0], jnp.zeros_like(gi)
        for j in range(1, n):
            upd = vals[j] > best
            idx = jnp.where(upd, j, idx)
            best = jnp.where(upd, vals[j], best)
        return idx

    i1 = argmax_first(vb)
    i2 = argmax_first([jnp.where(i1 == j, -jnp.inf, vb[j]) for j in range(n)])

    def take(vals, idx):
        out = vals[0]
        for j in range(1, n):
            out = jnp.where(idx == j, vals[j], out)
        return out

    w1, w2 = take(vs, i1), take(vs, i2)
    tot = w1 + w2
    w1, w2 = w1 / tot, w2 / tot
    return gi * n + i1, gi * n + i2, w1, w2


class _MoeLayout:
    def __init__(self, T, tm):
        self.tm = tm
        self.n_tiles = T // tm
        self.group = MOE_GROUP
        self.rt = -(-(2 * tm + N_EXPERTS * (MOE_GROUP - 1)) // MXU_DIM) * MXU_DIM
        self.ng = self.rt // MOE_GROUP
        assert self.ng < LANES
        self.tmx = MOE_ROW_TILE
        self.cap = -(-(T + self.n_tiles * MOE_GROUP) // self.tmx) * self.tmx
        self.dump = N_EXPERTS * self.cap
        self.rows = self.dump + self.rt
        pad = self.n_tiles * N_EXPERTS * (MOE_GROUP - 1)
        self.nt = (2 * T + pad) // self.tmx + N_EXPERTS


def _dispatch_meta(e1, e2, cum, tri, lay):
    E, G = N_EXPERTS, lay.group
    tm = e1.shape[1]
    eid = lax.broadcasted_iota(jnp.int32, (E, tm), 0)
    oh1, oh2 = eid == e1, eid == e2
    cnt = jnp.where(oh1 | oh2, 1.0, 0.0)
    pre = _dot(cnt.astype(BF16), tri)
    tot = jnp.sum(cnt, axis=1, keepdims=True)
    ptot = jnp.broadcast_to(jnp.ceil(tot * (1.0 / G)) * G, (E, LANES))
    below = jnp.where(lax.broadcasted_iota(jnp.int32, (E, E), 0) > lax.broadcasted_iota(jnp.int32, (E, E), 1),
                      1.0, 0.0)
    loff = jnp.dot(below, ptot, preferred_element_type=F32, precision=lax.Precision.HIGHEST)
    pos_e = loff[:, :1] + pre
    pos1 = jnp.sum(jnp.where(oh1, pos_e, 0.0), axis=0, keepdims=True)
    pos2 = jnp.sum(jnp.where(oh2, pos_e, 0.0), axis=0, keepdims=True)

    lane = lax.broadcasted_iota(jnp.int32, (E, LANES), 1)
    g_row = (lane * G).astype(F32)
    eg = jnp.sum(jnp.where(loff + ptot <= g_row, 1, 0), axis=0, keepdims=True)
    erow = lax.broadcasted_iota(jnp.int32, (E, LANES), 0)
    base = erow.astype(F32) * float(lay.cap) + cum - loff
    sel = jnp.sum(jnp.where(erow == eg, base, 0.0), axis=0, keepdims=True)
    dst = jnp.where(eg < E, g_row[:1] + sel, float(lay.dump) + g_row[:1])
    n_used = jnp.sum(ptot[:, :1], axis=0, keepdims=True) * (1.0 / G)
    table = jnp.where(lane[:1] == LANES - 1, n_used, dst).astype(jnp.int32)
    return pos1, pos2, table, cum + ptot


def _outproj_body(y_ref, w_ref, h_ref, g1_ref, gn_ref, sh_ref, sc_ref, wr_ref, br_ref, tri_ref,
                  ho_ref, xn_ref, rows_ref, cols_ref, tbl_ref, cum_ref, *, lay):
    @pl.when(pl.program_id(0) == 0)
    def _():
        cum_ref[...] = jnp.zeros_like(cum_ref)

    hn = h_ref[...] + g1_ref[0] * _dot(y_ref[...], w_ref[...])
    ho_ref[...] = hn
    xn = _norm_mod(hn, gn_ref[...], sh_ref[0], sc_ref[0])
    xb = xn.astype(BF16)
    xn_ref[...] = xb
    xl = (xn - xb.astype(F32)).astype(BF16)
    hl = _dot_nt(wr_ref[...], xb)
    logits_t = hl[:N_EXPERTS] + hl[N_EXPERTS:] + _dot_nt(wr_ref[:N_EXPERTS, :], xl)
    e1, e2, w1, w2 = _route(logits_t, br_ref[...])
    pos1, pos2, table, cum = _dispatch_meta(e1, e2, cum_ref[...], tri_ref[...], lay)
    cum_ref[...] = cum
    tbl_ref[0] = table
    tm = e1.shape[1]
    rows = jnp.concatenate([pos1, pos2, w1, w2, jnp.zeros((SUBLANES - 4, tm), F32)], axis=0)
    rows_ref[...] = rows
    cols_ref[...] = jnp.concatenate([rows, jnp.zeros((LANES - SUBLANES, tm), F32)], axis=0).T


def _outproj_route(y, w, h, g1, gn, sh, sc, wr_t, br, S, lay):
    T, D = h.shape
    K = y.shape[1]
    tm = lay.tm
    per_b = S // tm
    vec = pl.BlockSpec((1, 1, D), lambda i: (i // per_b, 0, 0))
    tok = lambda n: pl.BlockSpec((tm, n), lambda i: (i, 0))
    tri = jnp.triu(jnp.ones((tm, tm), BF16), k=1)
    return pl.pallas_call(
        functools.partial(_outproj_body, lay=lay),
        out_shape=(jax.ShapeDtypeStruct((T, D), F32),
                   jax.ShapeDtypeStruct((T, D), BF16),
                   jax.ShapeDtypeStruct((SUBLANES, T), F32),
                   jax.ShapeDtypeStruct((T, LANES), F32),
                   jax.ShapeDtypeStruct((lay.n_tiles, 1, LANES), jnp.int32),
                   jax.ShapeDtypeStruct((N_EXPERTS, LANES), F32)),
        grid=(T // tm,),
        in_specs=[tok(K),
                  pl.BlockSpec((K, D), lambda i: (0, 0)),
                  tok(D), vec,
                  pl.BlockSpec((1, D), lambda i: (0, 0)),
                  vec, vec,
                  pl.BlockSpec((2 * N_EXPERTS, D), lambda i: (0, 0)),
                  pl.BlockSpec((N_EXPERTS, 1), lambda i: (0, 0)),
                  pl.BlockSpec((tm, tm), lambda i: (0, 0))],
        out_specs=(tok(D), tok(D),
                   pl.BlockSpec((SUBLANES, tm), lambda i: (0, i)),
                   tok(LANES),
                   pl.BlockSpec((1, 1, LANES), lambda i: (i, 0, 0)),
                   pl.BlockSpec((N_EXPERTS, LANES), lambda i: (0, 0))),
        compiler_params=_cparams("arbitrary"),
        name="outproj_route",
    )(y, w, h, g1, gn, sh, sc, wr_t, br, tri)


def _group_copy(hbm_ref, buf_ref, tbl_ref, i, g, sem, lay, to_hbm):
    G = lay.group
    hbm = hbm_ref.at[pl.ds(pl.multiple_of(tbl_ref[i * LANES + g], G), G), :]
    vmem = buf_ref.at[pl.ds(pl.multiple_of(g * G, G), G), :]
    return pltpu.make_async_copy(vmem, hbm, sem) if to_hbm else pltpu.make_async_copy(hbm, vmem, sem)


def _dispatch_body(tbl_ref, x_ref, rows_ref, xs_ref, buf_ref, sem, *, lay):
    i = pl.program_id(0)
    pos1 = rows_ref[0:1, :].astype(jnp.int32)
    pos2 = rows_ref[1:2, :].astype(jnp.int32)
    r = lax.broadcasted_iota(jnp.int32, (lay.rt, lay.tm), 0)
    perm = jnp.where((r == pos1) | (r == pos2), 1.0, 0.0).astype(BF16)
    slot = i % 2
    buf_ref[slot] = _dot(perm, x_ref[...]).astype(BF16)

    def copies(tile, slot_):
        return [_group_copy(xs_ref, buf_ref.at[slot_], tbl_ref, tile, g, sem.at[slot_], lay, True)
                for g in range(lay.ng)]

    @pl.when(i > 0)
    def _():
        for c in copies(i - 1, 1 - slot):
            c.wait()

    for c in copies(i, slot):
        c.start()

    @pl.when(i == pl.num_programs(0) - 1)
    def _():
        for c in copies(i, slot):
            c.wait()


def _moe_dispatch(xn, rows, tbl, lay):
    T, D = xn.shape
    tm = lay.tm
    grid_spec = pltpu.PrefetchScalarGridSpec(
        num_scalar_prefetch=1,
        grid=(lay.n_tiles,),
        in_specs=[pl.BlockSpec((tm, D), lambda i, tbl: (i, 0)),
                  pl.BlockSpec((SUBLANES, tm), lambda i, tbl: (0, i))],
        out_specs=pl.BlockSpec(memory_space=pl.ANY),
        scratch_shapes=[pltpu.VMEM((2, lay.rt, D), BF16), pltpu.SemaphoreType.DMA((2,))],
    )
    return pl.pallas_call(
        functools.partial(_dispatch_body, lay=lay),
        out_shape=jax.ShapeDtypeStruct((lay.rows, D), BF16),
        grid_spec=grid_spec,
        compiler_params=_cparams("arbitrary"),
        name="moe_dispatch",
    )(tbl, xn, rows)


def _expert_body(te_ref, tb_ref, nv_ref, x_ref, wg_ref, wu_ref, wd_ref, y_ref, wgu_bf, wd_bf):
    n = pl.program_id(0)
    f = wg_ref.shape[2]

    @pl.when((n == 0) | (te_ref[n] != te_ref[jnp.maximum(n - 1, 0)]))
    def _():
        wgu_bf[:, :f] = wg_ref[0].astype(BF16)
        wgu_bf[:, f:] = wu_ref[0].astype(BF16)
        wd_bf[...] = wd_ref[0].astype(BF16)

    @pl.when(n < nv_ref[0])
    def _():
        hgu = _dot(x_ref[...], wgu_bf[...])
        hdn = (_silu(hgu[:, :f]) * hgu[:, f:]).astype(BF16)
        y_ref[...] = _dot(hdn, wd_bf[...]).astype(BF16)


def _moe_experts(xs, wg, wu, wd, layer, tile_e, tile_blk, n_valid, lay):
    R, D = xs.shape
    F = wg.shape[3]
    tmx = lay.tmx
    wspec = lambda shape: pl.BlockSpec((None, 1) + shape, lambda n, te, tb, nv: (layer, te[n], 0, 0))
    grid_spec = pltpu.PrefetchScalarGridSpec(
        num_scalar_prefetch=3,
        grid=(lay.nt,),
        in_specs=[pl.BlockSpec((tmx, D), lambda n, te, tb, nv: (tb[n], 0)),
                  wspec((D, F)), wspec((D, F)), wspec((F, D))],
        out_specs=pl.BlockSpec((tmx, D), lambda n, te, tb, nv: (tb[n], 0)),
        scratch_shapes=[pltpu.VMEM((D, 2 * F), BF16), pltpu.VMEM((F, D), BF16)],
    )
    return pl.pallas_call(
        _expert_body,
        out_shape=jax.ShapeDtypeStruct((R, D), BF16),
        grid_spec=grid_spec,
        compiler_params=_cparams("arbitrary"),
        name="moe_experts",
    )(tile_e, tile_blk, n_valid, xs, wg, wu, wd)


def _combine_body(tbl_ref, cols_ref, h_ref, g2_ref, *rest, lay, final):
    if final:
        gfin_ref, ys_ref, o_ref, buf_ref, sem = rest
    else:
        ys_ref, o_ref, buf_ref, sem = rest
    i = pl.program_id(0)
    slot = i % 2

    def fetch(tile, slot_, start):
        def one(g, _):
            c = _group_copy(ys_ref, buf_ref.at[slot_], tbl_ref, tile, g, sem.at[slot_], lay, False)
            c.start() if start else c.wait()
            return 0
        lax.fori_loop(0, tbl_ref[tile * LANES + LANES - 1], one, 0)

    @pl.when(i == 0)
    def _():
        buf_ref[...] = jnp.zeros_like(buf_ref)
        fetch(0, 0, True)

    @pl.when(i + 1 < pl.num_programs(0))
    def _():
        fetch(i + 1, 1 - slot, True)

    fetch(i, slot, False)

    cols = cols_ref[...]
    r = lax.broadcasted_iota(jnp.int32, (lay.tm, lay.rt), 1)
    mix = (jnp.where(r == cols[:, 0:1].astype(jnp.int32), cols[:, 2:3], 0.0)
           + jnp.where(r == cols[:, 1:2].astype(jnp.int32), cols[:, 3:4], 0.0)).astype(BF16)
    out = h_ref[...] + g2_ref[0] * _dot(mix, buf_ref[slot])
    o_ref[...] = _rms(out, gfin_ref[...]) if final else out


def _moe_combine(ys, cols, tbl, h, g2, S, lay, final_gain=None):
    T, D = h.shape
    tm = lay.tm
    per_b = S // tm
    final = final_gain is not None
    in_specs = [pl.BlockSpec((tm, LANES), lambda i, tbl: (i, 0)),
                pl.BlockSpec((tm, D), lambda i, tbl: (i, 0)),
                pl.BlockSpec((1, 1, D), lambda i, tbl: (i // per_b, 0, 0))]
    args = [cols, h, g2]
    if final:
        in_specs.append(pl.BlockSpec((1, D), lambda i, tbl: (0, 0)))
        args.append(final_gain)
    grid_spec = pltpu.PrefetchScalarGridSpec(
        num_scalar_prefetch=1,
        grid=(lay.n_tiles,),
        in_specs=in_specs + [pl.BlockSpec(memory_space=pl.ANY)],
        out_specs=pl.BlockSpec((tm, D), lambda i, tbl: (i, 0)),
        scratch_shapes=[pltpu.VMEM((2, lay.rt, D), BF16), pltpu.SemaphoreType.DMA((2,))],
    )
    return pl.pallas_call(
        functools.partial(_combine_body, lay=lay, final=final),
        out_shape=jax.ShapeDtypeStruct((T, D), F32),
        grid_spec=grid_spec,
        compiler_params=_cparams("arbitrary"),
        name="moe_combine",
    )(tbl, *args, ys)


def _expert_tiles(used, lay):
    tiles = jnp.ceil(used / lay.tmx).astype(jnp.int32)
    ends = jnp.cumsum(tiles)
    n_valid = ends[-1]
    n = jnp.minimum(jnp.arange(lay.nt, dtype=jnp.int32), n_valid - 1)
    e = jnp.sum((ends[None, :] <= n[:, None]).astype(jnp.int32), axis=1)
    blk = e * (lay.cap // lay.tmx) + n - (ends - tiles)[e]
    return e, blk, n_valid.reshape(1)


def _moe(xn, rows, cols, tbl, used, wg, wu, wd, layer, h, g2, S, lay, final_gain=None):
    tbl = tbl.reshape(-1)
    xs = _moe_dispatch(xn, rows, tbl, lay)
    ys = _moe_experts(xs, wg, wu, wd, layer, *_expert_tiles(used, lay), lay)
    return _moe_combine(ys, cols, tbl, h, g2, S, lay, final_gain)


def _kv_body(h_ref, g_ref, sh_ref, sc_ref, wa_ref, gkv_ref, wb_ref, cos_ref, sin_ref,
             k_ref, vt_ref):
    hn = _norm_mod(h_ref[...], g_ref[...], sh_ref[0], sc_ref[0]).astype(BF16)
    a = _dot(hn, wa_ref[...])
    c_kv = _rms(a[:, :KV_LORA], gkv_ref[...]).astype(BF16)
    kr = (a[:, KV_LORA:KV_LORA + LANES] * cos_ref[...]
          + a[:, KV_LORA + LANES:] * sin_ref[...]).astype(BF16)
    kv = _dot(c_kv, wb_ref[...])
    hk = MLA_HEADS * QK_NOPE
    w = QK_NOPE + LANES
    for hd in range(MLA_HEADS):
        k_ref[:, hd * w:hd * w + QK_NOPE] = kv[:, hd * QK_NOPE:(hd + 1) * QK_NOPE].astype(BF16)
        k_ref[:, hd * w + QK_NOPE:(hd + 1) * w] = kr
        vh = kv[:, hk + hd * V_HEAD:hk + (hd + 1) * V_HEAD]
        for g in range(vt_ref.shape[1]):
            vt_ref[hd, g] = vh[g * ATT_KEYS:(g + 1) * ATT_KEYS, :].T.astype(BF16)


def _mla_kv(h, g, sh, sc, wa, gkv, wb, cos, sin, S):
    T, D = h.shape
    tm = min(TOK_TILE, S)
    per_b = S // tm
    vec = pl.BlockSpec((1, 1, D), lambda i: (i // per_b, 0, 0))
    tok = lambda n: pl.BlockSpec((tm, n), lambda i: (i, 0))
    full = lambda a: pl.BlockSpec(a.shape, lambda i: (0, 0))
    kw = MLA_HEADS * (QK_NOPE + LANES)
    gk = tm // ATT_KEYS
    vt_shape = (MLA_HEADS, T // ATT_KEYS, V_HEAD, ATT_KEYS)
    return pl.pallas_call(
        _kv_body,
        out_shape=(jax.ShapeDtypeStruct((T, kw), BF16), jax.ShapeDtypeStruct(vt_shape, BF16)),
        grid=(T // tm,),
        in_specs=[tok(D), full(g), vec, vec, full(wa), full(gkv), full(wb), tok(LANES), tok(LANES)],
        out_specs=(tok(kw),
                   pl.BlockSpec((MLA_HEADS, gk, V_HEAD, ATT_KEYS), lambda i: (0, i, 0, 0))),
        compiler_params=_cparams("parallel"),
        name="mla_kv",
    )(h, g, sh, sc, wa, gkv, wb, cos, sin)


def _q_body(h_ref, g_ref, sh_ref, sc_ref, wa_ref, gq_ref, wb_ref, cos_ref, sin_ref, q_ref, *, scale):
    xn = _norm_mod(h_ref[...], g_ref[...], sh_ref[0], sc_ref[0]).astype(BF16)
    qa = _rms(_dot(xn, wa_ref[...]), gq_ref[...]).astype(BF16)
    cos = cos_ref[...]
    sin = sin_ref[...]
    wi = QK_NOPE + 2 * LANES
    wo = QK_NOPE + LANES
    for hd in range(MLA_HEADS):
        qb = _dot(qa, wb_ref[:, hd * wi:(hd + 1) * wi])
        q_ref[:, hd * wo:hd * wo + QK_NOPE] = (qb[:, :QK_NOPE] * scale).astype(BF16)
        rp = qb[:, QK_NOPE:QK_NOPE + LANES] * cos + qb[:, QK_NOPE + LANES:] * sin
        q_ref[:, hd * wo + QK_NOPE:(hd + 1) * wo] = (rp * scale).astype(BF16)


def _mla_q(h, g, sh, sc, wa, gq, wb, cos, sin, S):
    T, D = h.shape
    tm = min(TOK_TILE, S)
    per_b = S // tm
    vec = pl.BlockSpec((1, 1, D), lambda i: (i // per_b, 0, 0))
    tok = lambda n: pl.BlockSpec((tm, n), lambda i: (i, 0))
    full = lambda a: pl.BlockSpec(a.shape, lambda i: (0, 0))
    qw = MLA_HEADS * (QK_NOPE + LANES)
    return pl.pallas_call(
        functools.partial(_q_body, scale=(QK_NOPE + QK_ROPE) ** -0.5 * LOG2E),
        out_shape=jax.ShapeDtypeStruct((T, qw), BF16),
        grid=(T // tm,),
        in_specs=[tok(D), full(g), vec, vec, full(wa), full(gq), full(wb), tok(LANES), tok(LANES)],
        out_specs=tok(qw),
        compiler_params=_cparams("parallel"),
        name="mla_q",
    )(h, g, sh, sc, wa, gq, wb, cos, sin)


def _flash_body(q_ref, k_ref, vt_ref, o_ref, s_ref, smax_ref, m_ref, l_ref, acc_ref, *, tq, ks):
    i = pl.program_id(2)
    last = (i * tq + tq - 1) // ks

    def scores(slot, c):
        kc = k_ref[pl.ds(pl.multiple_of(c * ks, ks), ks), :]
        s = _dot_nt(kc, q_ref[...])
        s_ref[slot] = s
        smax_ref[slot] = jnp.max(s, axis=0, keepdims=True)

    def update(slot, c, masked):
        if masked:
            s = s_ref[slot]
            kch = (c * ks + lax.broadcasted_iota(jnp.int32, s.shape, 0)) // CHUNK
            qch = (i * tq + lax.broadcasted_iota(jnp.int32, s.shape, 1)) // CHUNK
            s_ref[slot] = jnp.where(kch <= qch, s, NEG_BIG)
            smax = jnp.max(s_ref[slot], axis=0, keepdims=True)
        else:
            smax = smax_ref[slot]
        m = m_ref[...]
        m_new = jnp.maximum(m, smax)
        alpha = jnp.exp2(m - m_new)
        part = jnp.zeros((SUBLANES, tq), F32)
        ps = []
        for r in range(ks // ATT_SLAB):
            p = jnp.exp2(s_ref[slot, r * ATT_SLAB:(r + 1) * ATT_SLAB, :] - m_new)
            for r8 in range(ATT_SLAB // SUBLANES):
                part = part + p[r8 * SUBLANES:(r8 + 1) * SUBLANES, :]
            ps.append(p.astype(BF16))
        m_ref[...] = m_new
        l_ref[...] = alpha * l_ref[...] + jnp.sum(part, axis=0, keepdims=True)
        acc_ref[...] = alpha * acc_ref[...] + _dot(vt_ref[0, c], jnp.concatenate(ps, axis=0))

    m_ref[...] = jnp.full_like(m_ref, NEG_BIG)
    l_ref[...] = jnp.zeros_like(l_ref)
    acc_ref[...] = jnp.zeros_like(acc_ref)
    scores(0, 0)

    def pair(g, _):
        scores(1, 2 * g + 1)
        update(0, 2 * g, False)
        scores(0, 2 * g + 2)
        update(1, 2 * g + 1, False)
        return 0

    lax.fori_loop(0, last // 2, pair, 0)

    @pl.when(last % 2 == 1)
    def _():
        scores(1, last)
        update(0, last - 1, False)
        update(1, last, True)

    @pl.when(last % 2 == 0)
    def _():
        update(0, last, True)

    o_ref[...] = (acc_ref[...] / l_ref[...]).T.astype(o_ref.dtype)


def _flash(q, k, vt, S):
    T = q.shape[0]
    B = T // S
    H = MLA_HEADS
    tq = min(ATT_TILE, S)
    ks = ATT_KEYS
    nq = S // tq
    dq = QK_NOPE + LANES
    return pl.pallas_call(
        functools.partial(_flash_body, tq=tq, ks=ks),
        out_shape=jax.ShapeDtypeStruct((T, H * V_HEAD), BF16),
        grid=(B, H, nq),
        in_specs=[pl.BlockSpec((tq, dq), lambda b, h, i: (b * nq + i, h)),
                  pl.BlockSpec((S, dq), lambda b, h, i: (b, h)),
                  pl.BlockSpec((1, S // ks, V_HEAD, ks), lambda b, h, i: (h, b, 0, 0))],
        out_specs=pl.BlockSpec((tq, V_HEAD), lambda b, h, i: (b * nq + i, h)),
        scratch_shapes=[pltpu.VMEM((2, ks, tq), F32), pltpu.VMEM((2, 1, tq), F32),
                        pltpu.VMEM((1, tq), F32), pltpu.VMEM((1, tq), F32),
                        pltpu.VMEM((V_HEAD, tq), F32)],
        compiler_params=_cparams("parallel", "parallel", "arbitrary"),
        name="flash_attn",
    )(q, k, vt)


def _rope_swap(w):
    half = w.shape[-1] // 2
    return jnp.concatenate([w[..., half:], w[..., :half]], axis=-1)


def _pad_lanes(w):
    return jnp.pad(w, [(0, 0)] * (w.ndim - 1) + [(0, LANES - w.shape[-1])])


def kernel(x, c, positions, w_mod, b_mod, norm_mix, norm_ffn, ret_w_in, ret_w_out, w_mod_kv, b_mod_kv, norm_kv, mla_w_kv_a, mla_kv_norm, mla_w_kv_b, mla_w_q_a, mla_q_norm, mla_w_q_b, mla_w_o, router_w, router_b, moe_w_gate, moe_w_up, moe_w_down, final_norm):
    B, S, D = x.shape
    T = B * S
    depth = w_mod.shape[0]
    n_a = ret_w_in.shape[0]

    c8 = jnp.pad(c, ((0, 8 - B), (0, 0)))
    mod = _mod_vectors(c8, w_mod, b_mod)[:, :B]
    mod = mod.reshape(depth, B, 6, 1, D)
    kv_mod = _mod_vectors(c8, w_mod_kv[None], b_mod_kv[None])[0, :B].reshape(B, 2, 1, D)

    pos_col = positions.reshape(T, 1)
    ones = jnp.ones((1, LANES), F32)
    inv_ret = (ROPE_THETA ** (-jnp.arange(LANES, dtype=F32) / LANES)).reshape(1, LANES)
    cos_r, sin_r = _rope_tables(pos_col, inv_ret, ones, ones)
    hr = QK_ROPE // 2
    inv_m = ROPE_THETA ** (-jnp.arange(hr, dtype=F32) / hr)
    inv_m = _pad_lanes(jnp.concatenate([inv_m, inv_m])[None])
    cm = _pad_lanes(jnp.ones((1, QK_ROPE), F32))
    sm = _pad_lanes(jnp.concatenate([-jnp.ones((1, hr), F32), jnp.ones((1, hr), F32)], axis=-1))
    cos_m, sin_m = _rope_tables(pos_col, inv_m, cm, sm)

    wr_hi = router_w.T.astype(BF16)
    wr_t = jnp.concatenate([wr_hi, (router_w.T - wr_hi.astype(F32)).astype(BF16)], axis=0)
    br = router_b.reshape(N_EXPERTS, 1)

    h = x.reshape(T, D)
    lay = _MoeLayout(T, min(TOK_TILE, S))
    k_full = v_full = None
    for layer in range(depth):
        sh1, sc1, g1, sh2, sc2, g2 = (mod[layer, :, i] for i in range(6))
        gmix = norm_mix[layer].reshape(1, D)
        if layer < n_a:
            proj = _ret_inproj(h, gmix, sh1, sc1, ret_w_in[layer].astype(BF16), S)
            mix = _retention(proj, cos_r, sin_r, S, D)
            w_o = ret_w_out[layer].astype(BF16)
        else:
            if layer == n_a:
                wa = mla_w_kv_a
                wa_r = wa[:, KV_LORA:]
                wa_p = jnp.concatenate([wa[:, :KV_LORA], _pad_lanes(wa_r), _pad_lanes(_rope_swap(wa_r))],
                                       axis=-1).astype(BF16)
                wb = mla_w_kv_b.reshape(KV_LORA, MLA_HEADS, QK_NOPE + V_HEAD)
                wb_p = jnp.concatenate([wb[..., :QK_NOPE].reshape(KV_LORA, -1),
                                        wb[..., QK_NOPE:].reshape(KV_LORA, -1)], axis=-1).astype(BF16)
                k_full, v_full = _mla_kv(h, norm_kv.reshape(1, D), kv_mod[:, 0], kv_mod[:, 1], wa_p,
                                         mla_kv_norm.reshape(1, KV_LORA), wb_p, cos_m, sin_m, S)
            j = layer - n_a
            wq = mla_w_q_b[j].reshape(Q_LORA, MLA_HEADS, QK_NOPE + QK_ROPE)
            wq_r = wq[..., QK_NOPE:]
            wq_p = jnp.concatenate([wq[..., :QK_NOPE], _pad_lanes(wq_r), _pad_lanes(_rope_swap(wq_r))],
                                   axis=-1).reshape(Q_LORA, -1).astype(BF16)
            q_full = _mla_q(h, gmix, sh1, sc1, mla_w_q_a[j].astype(BF16),
                            mla_q_norm[j].reshape(1, Q_LORA), wq_p, cos_m, sin_m, S)
            mix = _flash(q_full, k_full, v_full, S)
            w_o = mla_w_o[j].astype(BF16)
        h, xn, rows, cols, tbl, used = _outproj_route(mix, w_o, h, g1, norm_ffn[layer].reshape(1, D),
                                                      sh2, sc2, wr_t, br, S, lay)
        h = _moe(xn, rows, cols, tbl, used[:, 0], moe_w_gate, moe_w_up, moe_w_down, layer,
                 h, g2, S, lay, final_norm.reshape(1, D) if layer == depth - 1 else None)
    return h.reshape(B, S, D)
```

```python
import functools

import jax
import jax.numpy as jnp
from jax import lax
from jax.experimental import pallas as pl
from jax.experimental.pallas import tpu as pltpu

F32 = jnp.float32
BF16 = jnp.bfloat16

CHUNK = 64
RET_HEADS = 4
MLA_HEADS = 8
QK_NOPE = 128
QK_ROPE = 64
V_HEAD = 128
Q_LORA = 256
KV_LORA = 128
N_EXPERTS = 16
N_GROUPS = 4
EXPERTS_PER_GROUP = N_EXPERTS // N_GROUPS
D_EXPERT = 512
ROPE_THETA = 10000.0
EPS = 1e-6

LANES = 128
VMEM_LIMIT = 56 * 1024 * 1024
NEG_BIG = -1e30
LOG2E = 1.4426950408889634

RET_CHUNK = 256
TOK_TILE = 512
ATT_TILE = 512
ATT_KEYS = 512
ATT_HEADS = 2
ATT_SLAB = 32
SUBLANES = 8
MXU_DIM = 256
MOE_GROUP = 16
MOE_ROW_TILE = 512


def _cparams(*sem):
    return pltpu.CompilerParams(dimension_semantics=sem, vmem_limit_bytes=VMEM_LIMIT)


def _silu(x):
    return x * jax.nn.sigmoid(x)


def _rms(x, g):
    return x * lax.rsqrt(jnp.mean(x * x, axis=-1, keepdims=True) + EPS) * g


def _norm_mod(h, g, shift, scale):
    return _rms(h, g) * (1.0 + scale) + shift


def _dot(a, b):
    return jnp.dot(a, b, preferred_element_type=F32)


def _dot_nt(a, b, **kw):
    return lax.dot_general(a, b, (((1,), (1,)), ((), ())), preferred_element_type=F32, **kw)


def _dot_tn(a, b):
    return lax.dot_general(a, b, (((0,), (0,)), ((), ())), preferred_element_type=F32)


def _mod_body(c_ref, w_ref, b_ref, o_ref):
    ca = _silu(c_ref[...])
    o_ref[0] = jnp.dot(ca, w_ref[0], preferred_element_type=F32,
                       precision=lax.Precision.HIGHEST) + b_ref[0]


def _mod_vectors(c8, w, b):
    L, D, N = w.shape
    tn = D
    assert N % tn == 0
    return pl.pallas_call(
        _mod_body,
        out_shape=jax.ShapeDtypeStruct((L, 8, N), F32),
        grid=(L, N // tn),
        in_specs=[pl.BlockSpec((8, D), lambda l, j: (0, 0)),
                  pl.BlockSpec((1, D, tn), lambda l, j: (l, 0, j)),
                  pl.BlockSpec((1, 1, tn), lambda l, j: (l, 0, j))],
        out_specs=pl.BlockSpec((1, 8, tn), lambda l, j: (l, 0, j)),
        compiler_params=_cparams("parallel", "parallel"),
        name="mod_vectors",
    )(c8, w, b.reshape(L, 1, N))


def _rope_body(pos_ref, inv_ref, cm_ref, sm_ref, cos_ref, sin_ref):
    ang = pos_ref[...].astype(F32) * inv_ref[...]
    cos_ref[...] = jnp.cos(ang) * cm_ref[...]
    sin_ref[...] = jnp.sin(ang) * sm_ref[...]


def _rope_tables(pos_col, inv, cm, sm):
    T = pos_col.shape[0]
    tm = min(T, 1024)
    row = pl.BlockSpec((1, LANES), lambda i: (0, 0))
    return pl.pallas_call(
        _rope_body,
        out_shape=(jax.ShapeDtypeStruct((T, LANES), F32),) * 2,
        grid=(T // tm,),
        in_specs=[pl.BlockSpec((tm, 1), lambda i: (i, 0)), row, row, row],
        out_specs=(pl.BlockSpec((tm, LANES), lambda i: (i, 0)),) * 2,
        compiler_params=_cparams("parallel"),
        name="rope_tables",
    )(pos_col, inv, cm, sm)


def _inproj_body(h_ref, g_ref, sh_ref, sc_ref, w_ref, o_ref, *, tn):
    xn = _norm_mod(h_ref[...], g_ref[...], sh_ref[0], sc_ref[0]).astype(BF16)
    for j in range(w_ref.shape[1] // tn):
        o_ref[:, j * tn:(j + 1) * tn] = _dot(xn, w_ref[:, j * tn:(j + 1) * tn]).astype(BF16)


def _ret_inproj(h, g, sh, sc, w, S):
    T, D = h.shape
    N = w.shape[1]
    tm = min(TOK_TILE, S)
    per_b = S // tm
    vec = pl.BlockSpec((1, 1, D), lambda i: (i // per_b, 0, 0))
    return pl.pallas_call(
        functools.partial(_inproj_body, tn=512),
        out_shape=jax.ShapeDtypeStruct((T, N), BF16),
        grid=(T // tm,),
        in_specs=[pl.BlockSpec((tm, D), lambda i: (i, 0)),
                  pl.BlockSpec((1, D), lambda i: (0, 0)),
                  vec, vec,
                  pl.BlockSpec((D, N), lambda i: (0, 0))],
        out_specs=pl.BlockSpec((tm, N), lambda i: (i, 0)),
        compiler_params=_cparams("parallel"),
        name="ret_inproj",
    )(h, g, sh, sc, w)


def _ret_body(q_ref, k_ref, v_ref, g_ref, cos_ref, sin_ref, di_ref, dq_ref, dk_ref, dc_ref,
              y_ref, state_ref, *, dk_dim):
    @pl.when(pl.program_id(1) == 0)
    def _():
        state_ref[...] = jnp.zeros_like(state_ref)

    cos = cos_ref[...]
    sin = sin_ref[...]
    half = dk_dim // 2
    dv_dim = 2 * dk_dim

    def rope(x):
        x1, x2 = x[:, :half], x[:, half:]
        return jnp.concatenate([x1 * cos - x2 * sin, x1 * sin + x2 * cos], axis=-1)

    for hd in range(RET_HEADS):
        qk = slice(hd * dk_dim, (hd + 1) * dk_dim)
        vg = slice(hd * dv_dim, (hd + 1) * dv_dim)
        qr = rope(q_ref[:, qk].astype(F32))
        kr = rope(k_ref[:, qk].astype(F32)) * (dk_dim ** -0.5)
        qb = qr.astype(BF16)
        v = v_ref[:, vg]
        inner = (_dot_nt(qb, kr.astype(BF16)) * di_ref[hd]).astype(BF16)
        st = state_ref[hd]
        out = _dot(inner, v) + _dot(qb, st.astype(BF16)) * dq_ref[hd]
        kd = (kr * dk_ref[hd]).astype(BF16)
        state_ref[hd] = st * dc_ref[hd] + _dot_tn(kd, v)

        mu = jnp.mean(out, axis=-1, keepdims=True)
        cen = out - mu
        var = jnp.mean(cen * cen, axis=-1, keepdims=True)
        o = cen * lax.rsqrt(var + EPS)
        y_ref[:, vg] = (_silu(g_ref[:, vg].astype(F32)) * o).astype(BF16)


def _retention(proj, cos, sin, S, D):
    T = proj.shape[0]
    B = T // S
    H = RET_HEADS
    dk = D // H
    dv = 2 * dk
    C = min(RET_CHUNK, S)
    n = S // C
    log_g = jnp.log1p(-(2.0 ** (-5.0 - jnp.arange(H, dtype=F32))))
    t = jnp.arange(C, dtype=F32)
    diff = t[:, None] - t[None, :]
    d_intra = jnp.where(diff >= 0, jnp.exp(log_g[:, None, None] * jnp.maximum(diff, 0.0)), 0.0)
    d_q = jnp.exp(log_g[:, None] * (t + 1.0))[:, :, None]
    d_k = jnp.exp(log_g[:, None] * (C - 1.0 - t))[:, :, None]
    d_c = jnp.exp(log_g * C)[:, None, None]

    assert 2 * D == H * dv
    row = lambda b, i: b * n + i
    const = lambda a: pl.BlockSpec(a.shape, lambda b, i: (0, 0, 0))
    return pl.pallas_call(
        functools.partial(_ret_body, dk_dim=dk),
        out_shape=jax.ShapeDtypeStruct((T, H * dv), BF16),
        grid=(B, n),
        in_specs=[pl.BlockSpec((C, D), lambda b, i: (row(b, i), 0)),
                  pl.BlockSpec((C, D), lambda b, i: (row(b, i), 1)),
                  pl.BlockSpec((C, H * dv), lambda b, i: (row(b, i), 1)),
                  pl.BlockSpec((C, H * dv), lambda b, i: (row(b, i), 2)),
                  pl.BlockSpec((C, dk // 2), lambda b, i: (row(b, i), 0)),
                  pl.BlockSpec((C, dk // 2), lambda b, i: (row(b, i), 0)),
                  const(d_intra), const(d_q), const(d_k), const(d_c)],
        out_specs=pl.BlockSpec((C, H * dv), lambda b, i: (row(b, i), 0)),
        scratch_shapes=[pltpu.VMEM((H, dk, dv), F32)],
        compiler_params=_cparams("parallel", "arbitrary"),
        name="retention",
    )(proj, proj, proj, proj, cos, sin, d_intra, d_q, d_k, d_c)


def _route(logits_t, bias):
    sc = jax.nn.sigmoid(logits_t)
    bi = sc + bias
    s_rows = [sc[e:e + 1, :] for e in range(N_EXPERTS)]
    b_rows = [bi[e:e + 1, :] for e in range(N_EXPERTS)]

    def top2sum(a, b, c, d):
        p, q = jnp.maximum(a, b), jnp.minimum(a, b)
        r, s = jnp.maximum(c, d), jnp.minimum(c, d)
        return jnp.maximum(p, r) + jnp.maximum(jnp.minimum(p, r), jnp.maximum(q, s))

    n = EXPERTS_PER_GROUP
    gs = [top2sum(*b_rows[n * g:n * g + n]) for g in range(N_GROUPS)]
    best, gi = gs[0], jnp.zeros_like(gs[0], dtype=jnp.int32)
    for g in range(1, N_GROUPS):
        upd = gs[g] > best
        gi = jnp.where(upd, g, gi)
        best = jnp.where(upd, gs[g], best)

    def pick(rows, j):
        out = rows[j]
        for g in range(1, N_GROUPS):
            out = jnp.where(gi == g, rows[n * g + j], out)
        return out

    vb = [pick(b_rows, j) for j in range(n)]
    vs = [pick(s_rows, j) for j in range(n)]

    def argmax_first(vals):
        best, idx = vals[0], jnp.zeros_like(gi)
        for j in range(1, n):
            upd = vals[j] > best
            idx = jnp.where(upd, j, idx)
            best = jnp.where(upd, vals[j], best)
        return idx

    i1 = argmax_first(vb)
    i2 = argmax_first([jnp.where(i1 == j, -jnp.inf, vb[j]) for j in range(n)])

    def take(vals, idx):
        out = vals[0]
        for j in range(1, n):
            out = jnp.where(idx == j, vals[j], out)
        return out

    w1, w2 = take(vs, i1), take(vs, i2)
    tot = w1 + w2
    w1, w2 = w1 / tot, w2 / tot
    return gi * n + i1, gi * n + i2, w1, w2


class _MoeLayout:
    def __init__(self, T, tm):
        self.tm = tm
        self.n_tiles = T // tm
        self.group = MOE_GROUP
        self.rt = -(-(2 * tm + N_EXPERTS * (MOE_GROUP - 1)) // MXU_DIM) * MXU_DIM
        self.ng = self.rt // MOE_GROUP
        assert self.ng < LANES
        self.tmx = MOE_ROW_TILE
        self.cap = -(-(T + self.n_tiles * MOE_GROUP) // self.tmx) * self.tmx
        self.dump = N_EXPERTS * self.cap
        self.rows = self.dump + self.rt
        pad = self.n_tiles * N_EXPERTS * (MOE_GROUP - 1)
        self.nt = (2 * T + pad) // self.tmx + N_EXPERTS


def _dispatch_meta(e1, e2, cum, tri, lay):
    E, G = N_EXPERTS, lay.group
    tm = e1.shape[1]
    eid = lax.broadcasted_iota(jnp.int32, (E, tm), 0)
    oh1, oh2 = eid == e1, eid == e2
    cnt = jnp.where(oh1 | oh2, 1.0, 0.0)
    pre = _dot(cnt.astype(BF16), tri)
    tot = jnp.sum(cnt, axis=1, keepdims=True)
    ptot = jnp.broadcast_to(jnp.ceil(tot * (1.0 / G)) * G, (E, LANES))
    below = jnp.where(lax.broadcasted_iota(jnp.int32, (E, E), 0) > lax.broadcasted_iota(jnp.int32, (E, E), 1),
                      1.0, 0.0)
    loff = jnp.dot(below, ptot, preferred_element_type=F32, precision=lax.Precision.HIGHEST)
    pos_e = loff[:, :1] + pre
    pos1 = jnp.sum(jnp.where(oh1, pos_e, 0.0), axis=0, keepdims=True)
    pos2 = jnp.sum(jnp.where(oh2, pos_e, 0.0), axis=0, keepdims=True)

    lane = lax.broadcasted_iota(jnp.int32, (E, LANES), 1)
    g_row = (lane * G).astype(F32)
    eg = jnp.sum(jnp.where(loff + ptot <= g_row, 1, 0), axis=0, keepdims=True)
    erow = lax.broadcasted_iota(jnp.int32, (E, LANES), 0)
    base = erow.astype(F32) * float(lay.cap) + cum - loff
    sel = jnp.sum(jnp.where(erow == eg, base, 0.0), axis=0, keepdims=True)
    dst = jnp.where(eg < E, g_row[:1] + sel, float(lay.dump) + g_row[:1])
    n_used = jnp.sum(ptot[:, :1], axis=0, keepdims=True) * (1.0 / G)
    table = jnp.where(lane[:1] == LANES - 1, n_used, dst).astype(jnp.int32)
    return pos1, pos2, table, cum + ptot


def _outproj_body(y_ref, w_ref, h_ref, g1_ref, gn_ref, sh_ref, sc_ref, wr_ref, br_ref, tri_ref,
                  ho_ref, xn_ref, rows_ref, cols_ref, tbl_ref, cum_ref, *, lay):
    @pl.when(pl.program_id(0) == 0)
    def _():
        cum_ref[...] = jnp.zeros_like(cum_ref)

    hn = h_ref[...] + g1_ref[0] * _dot(y_ref[...], w_ref[...])
    ho_ref[...] = hn
    xn = _norm_mod(hn, gn_ref[...], sh_ref[0], sc_ref[0])
    xb = xn.astype(BF16)
    xn_ref[...] = xb
    xl = (xn - xb.astype(F32)).astype(BF16)
    hl = _dot_nt(wr_ref[...], xb)
    logits_t = hl[:N_EXPERTS] + hl[N_EXPERTS:] + _dot_nt(wr_ref[:N_EXPERTS, :], xl)
    e1, e2, w1, w2 = _route(logits_t, br_ref[...])
    pos1, pos2, table, cum = _dispatch_meta(e1, e2, cum_ref[...], tri_ref[...], lay)
    cum_ref[...] = cum
    tbl_ref[0] = table
    tm = e1.shape[1]
    rows = jnp.concatenate([pos1, pos2, w1, w2, jnp.zeros((SUBLANES - 4, tm), F32)], axis=0)
    rows_ref[...] = rows
    cols_ref[...] = jnp.concatenate([rows, jnp.zeros((LANES - SUBLANES, tm), F32)], axis=0).T


def _outproj_route(y, w, h, g1, gn, sh, sc, wr_t, br, S, lay):
    T, D = h.shape
    K = y.shape[1]
    tm = lay.tm
    per_b = S // tm
    vec = pl.BlockSpec((1, 1, D), lambda i: (i // per_b, 0, 0))
    tok = lambda n: pl.BlockSpec((tm, n), lambda i: (i, 0))
    tri = jnp.triu(jnp.ones((tm, tm), BF16), k=1)
    return pl.pallas_call(
        functools.partial(_outproj_body, lay=lay),
        out_shape=(jax.ShapeDtypeStruct((T, D), F32),
                   jax.ShapeDtypeStruct((T, D), BF16),
                   jax.ShapeDtypeStruct((SUBLANES, T), F32),
                   jax.ShapeDtypeStruct((T, LANES), F32),
                   jax.ShapeDtypeStruct((lay.n_tiles, 1, LANES), jnp.int32),
                   jax.ShapeDtypeStruct((N_EXPERTS, LANES), F32)),
        grid=(T // tm,),
        in_specs=[tok(K),
                  pl.BlockSpec((K, D), lambda i: (0, 0)),
                  tok(D), vec,
                  pl.BlockSpec((1, D), lambda i: (0, 0)),
                  vec, vec,
                  pl.BlockSpec((2 * N_EXPERTS, D), lambda i: (0, 0)),
                  pl.BlockSpec((N_EXPERTS, 1), lambda i: (0, 0)),
                  pl.BlockSpec((tm, tm), lambda i: (0, 0))],
        out_specs=(tok(D), tok(D),
                   pl.BlockSpec((SUBLANES, tm), lambda i: (0, i)),
                   tok(LANES),
                   pl.BlockSpec((1, 1, LANES), lambda i: (i, 0, 0)),
                   pl.BlockSpec((N_EXPERTS, LANES), lambda i: (0, 0))),
        compiler_params=_cparams("arbitrary"),
        name="outproj_route",
    )(y, w, h, g1, gn, sh, sc, wr_t, br, tri)


def _group_copy(hbm_ref, buf_ref, tbl_ref, i, g, sem, lay, to_hbm):
    G = lay.group
    hbm = hbm_ref.at[pl.ds(pl.multiple_of(tbl_ref[i * LANES + g], G), G), :]
    vmem = buf_ref.at[pl.ds(pl.multiple_of(g * G, G), G), :]
    return pltpu.make_async_copy(vmem, hbm, sem) if to_hbm else pltpu.make_async_copy(hbm, vmem, sem)


def _dispatch_body(tbl_ref, x_ref, rows_ref, xs_ref, buf_ref, sem, *, lay):
    i = pl.program_id(0)
    pos1 = rows_ref[0:1, :].astype(jnp.int32)
    pos2 = rows_ref[1:2, :].astype(jnp.int32)
    r = lax.broadcasted_iota(jnp.int32, (lay.rt, lay.tm), 0)
    perm = jnp.where((r == pos1) | (r == pos2), 1.0, 0.0).astype(BF16)
    slot = i % 2
    buf_ref[slot] = _dot(perm, x_ref[...]).astype(BF16)

    def copies(tile, slot_):
        return [_group_copy(xs_ref, buf_ref.at[slot_], tbl_ref, tile, g, sem.at[slot_], lay, True)
                for g in range(lay.ng)]

    @pl.when(i > 0)
    def _():
        for c in copies(i - 1, 1 - slot):
            c.wait()

    for c in copies(i, slot):
        c.start()

    @pl.when(i == pl.num_programs(0) - 1)
    def _():
        for c in copies(i, slot):
            c.wait()


def _moe_dispatch(xn, rows, tbl, lay):
    T, D = xn.shape
    tm = lay.tm
    grid_spec = pltpu.PrefetchScalarGridSpec(
        num_scalar_prefetch=1,
        grid=(lay.n_tiles,),
        in_specs=[pl.BlockSpec((tm, D), lambda i, tbl: (i, 0)),
                  pl.BlockSpec((SUBLANES, tm), lambda i, tbl: (0, i))],
        out_specs=pl.BlockSpec(memory_space=pl.ANY),
        scratch_shapes=[pltpu.VMEM((2, lay.rt, D), BF16), pltpu.SemaphoreType.DMA((2,))],
    )
    return pl.pallas_call(
        functools.partial(_dispatch_body, lay=lay),
        out_shape=jax.ShapeDtypeStruct((lay.rows, D), BF16),
        grid_spec=grid_spec,
        compiler_params=_cparams("arbitrary"),
        name="moe_dispatch",
    )(tbl, xn, rows)


def _expert_body(te_ref, tb_ref, nv_ref, x_ref, wg_ref, wu_ref, wd_ref, y_ref, wgu_bf, wd_bf):
    n = pl.program_id(0)
    f = wg_ref.shape[2]

    @pl.when((n == 0) | (te_ref[n] != te_ref[jnp.maximum(n - 1, 0)]))
    def _():
        wgu_bf[:, :f] = wg_ref[0].astype(BF16)
        wgu_bf[:, f:] = wu_ref[0].astype(BF16)
        wd_bf[...] = wd_ref[0].astype(BF16)

    @pl.when(n < nv_ref[0])
    def _():
        hgu = _dot(x_ref[...], wgu_bf[...])
        hdn = (_silu(hgu[:, :f]) * hgu[:, f:]).astype(BF16)
        y_ref[...] = _dot(hdn, wd_bf[...]).astype(BF16)


def _moe_experts(xs, wg, wu, wd, layer, tile_e, tile_blk, n_valid, lay):
    R, D = xs.shape
    F = wg.shape[3]
    tmx = lay.tmx
    wspec = lambda shape: pl.BlockSpec((None, 1) + shape, lambda n, te, tb, nv: (layer, te[n], 0, 0))
    grid_spec = pltpu.PrefetchScalarGridSpec(
        num_scalar_prefetch=3,
        grid=(lay.nt,),
        in_specs=[pl.BlockSpec((tmx, D), lambda n, te, tb, nv: (tb[n], 0)),
                  wspec((D, F)), wspec((D, F)), wspec((F, D))],
        out_specs=pl.BlockSpec((tmx, D), lambda n, te, tb, nv: (tb[n], 0)),
        scratch_shapes=[pltpu.VMEM((D, 2 * F), BF16), pltpu.VMEM((F, D), BF16)],
    )
    return pl.pallas_call(
        _expert_body,
        out_shape=jax.ShapeDtypeStruct((R, D), BF16),
        grid_spec=grid_spec,
        compiler_params=_cparams("arbitrary"),
        name="moe_experts",
    )(tile_e, tile_blk, n_valid, xs, wg, wu, wd)


def _combine_body(tbl_ref, cols_ref, h_ref, g2_ref, *rest, lay, final):
    if final:
        gfin_ref, ys_ref, o_ref, buf_ref, sem = rest
    else:
        ys_ref, o_ref, buf_ref, sem = rest
    i = pl.program_id(0)
    slot = i % 2

    def fetch(tile, slot_, start):
        def one(g, _):
            c = _group_copy(ys_ref, buf_ref.at[slot_], tbl_ref, tile, g, sem.at[slot_], lay, False)
            c.start() if start else c.wait()
            return 0
        lax.fori_loop(0, tbl_ref[tile * LANES + LANES - 1], one, 0)

    @pl.when(i == 0)
    def _():
        buf_ref[...] = jnp.zeros_like(buf_ref)
        fetch(0, 0, True)

    @pl.when(i + 1 < pl.num_programs(0))
    def _():
        fetch(i + 1, 1 - slot, True)

    fetch(i, slot, False)

    cols = cols_ref[...]
    r = lax.broadcasted_iota(jnp.int32, (lay.tm, lay.rt), 1)
    mix = (jnp.where(r == cols[:, 0:1].astype(jnp.int32), cols[:, 2:3], 0.0)
           + jnp.where(r == cols[:, 1:2].astype(jnp.int32), cols[:, 3:4], 0.0)).astype(BF16)
    out = h_ref[...] + g2_ref[0] * _dot(mix, buf_ref[slot])
    o_ref[...] = _rms(out, gfin_ref[...]) if final else out


def _moe_combine(ys, cols, tbl, h, g2, S, lay, final_gain=None):
    T, D = h.shape
    tm = lay.tm
    per_b = S // tm
    final = final_gain is not None
    in_specs = [pl.BlockSpec((tm, LANES), lambda i, tbl: (i, 0)),
                pl.BlockSpec((tm, D), lambda i, tbl: (i, 0)),
                pl.BlockSpec((1, 1, D), lambda i, tbl: (i // per_b, 0, 0))]
    args = [cols, h, g2]
    if final:
        in_specs.append(pl.BlockSpec((1, D), lambda i, tbl: (0, 0)))
        args.append(final_gain)
    grid_spec = pltpu.PrefetchScalarGridSpec(
        num_scalar_prefetch=1,
        grid=(lay.n_tiles,),
        in_specs=in_specs + [pl.BlockSpec(memory_space=pl.ANY)],
        out_specs=pl.BlockSpec((tm, D), lambda i, tbl: (i, 0)),
        scratch_shapes=[pltpu.VMEM((2, lay.rt, D), BF16), pltpu.SemaphoreType.DMA((2,))],
    )
    return pl.pallas_call(
        functools.partial(_combine_body, lay=lay, final=final),
        out_shape=jax.ShapeDtypeStruct((T, D), F32),
        grid_spec=grid_spec,
        compiler_params=_cparams("arbitrary"),
        name="moe_combine",
    )(tbl, *args, ys)


def _expert_tiles(used, lay):
    tiles = jnp.ceil(used / lay.tmx).astype(jnp.int32)
    ends = jnp.cumsum(tiles)
    n_valid = ends[-1]
    n = jnp.minimum(jnp.arange(lay.nt, dtype=jnp.int32), n_valid - 1)
    e = jnp.sum((ends[None, :] <= n[:, None]).astype(jnp.int32), axis=1)
    blk = e * (lay.cap // lay.tmx) + n - (ends - tiles)[e]
    return e, blk, n_valid.reshape(1)


def _moe(xn, rows, cols, tbl, used, wg, wu, wd, layer, h, g2, S, lay, final_gain=None):
    tbl = tbl.reshape(-1)
    xs = _moe_dispatch(xn, rows, tbl, lay)
    ys = _moe_experts(xs, wg, wu, wd, layer, *_expert_tiles(used, lay), lay)
    return _moe_combine(ys, cols, tbl, h, g2, S, lay, final_gain)


def _kv_body(h_ref, g_ref, sh_ref, sc_ref, wa_ref, gkv_ref, wb_ref, cos_ref, sin_ref,
             k_ref, vt_ref):
    hn = _norm_mod(h_ref[...], g_ref[...], sh_ref[0], sc_ref[0]).astype(BF16)
    a = _dot(hn, wa_ref[...])
    c_kv = _rms(a[:, :KV_LORA], gkv_ref[...]).astype(BF16)
    kr = (a[:, KV_LORA:KV_LORA + LANES] * cos_ref[...]
          + a[:, KV_LORA + LANES:] * sin_ref[...]).astype(BF16)
    kv = _dot(c_kv, wb_ref[...])
    hk = MLA_HEADS * QK_NOPE
    w = QK_NOPE + LANES
    for hd in range(MLA_HEADS):
        k_ref[:, hd * w:hd * w + QK_NOPE] = kv[:, hd * QK_NOPE:(hd + 1) * QK_NOPE].astype(BF16)
        k_ref[:, hd * w + QK_NOPE:(hd + 1) * w] = kr
        vh = kv[:, hk + hd * V_HEAD:hk + (hd + 1) * V_HEAD]
        for g in range(vt_ref.shape[1]):
            vt_ref[hd, g] = vh[g * ATT_KEYS:(g + 1) * ATT_KEYS, :].T.astype(BF16)


def _mla_kv(h, g, sh, sc, wa, gkv, wb, cos, sin, S):
    T, D = h.shape
    tm = min(TOK_TILE, S)
    per_b = S // tm
    vec = pl.BlockSpec((1, 1, D), lambda i: (i // per_b, 0, 0))
    tok = lambda n: pl.BlockSpec((tm, n), lambda i: (i, 0))
    full = lambda a: pl.BlockSpec(a.shape, lambda i: (0, 0))
    kw = MLA_HEADS * (QK_NOPE + LANES)
    gk = tm // ATT_KEYS
    vt_shape = (MLA_HEADS, T // ATT_KEYS, V_HEAD, ATT_KEYS)
    return pl.pallas_call(
        _kv_body,
        out_shape=(jax.ShapeDtypeStruct((T, kw), BF16), jax.ShapeDtypeStruct(vt_shape, BF16)),
        grid=(T // tm,),
        in_specs=[tok(D), full(g), vec, vec, full(wa), full(gkv), full(wb), tok(LANES), tok(LANES)],
        out_specs=(tok(kw),
                   pl.BlockSpec((MLA_HEADS, gk, V_HEAD, ATT_KEYS), lambda i: (0, i, 0, 0))),
        compiler_params=_cparams("parallel"),
        name="mla_kv",
    )(h, g, sh, sc, wa, gkv, wb, cos, sin)


def _q_body(h_ref, g_ref, sh_ref, sc_ref, wa_ref, gq_ref, wb_ref, cos_ref, sin_ref, q_ref, *, scale):
    xn = _norm_mod(h_ref[...], g_ref[...], sh_ref[0], sc_ref[0]).astype(BF16)
    qa = _rms(_dot(xn, wa_ref[...]), gq_ref[...]).astype(BF16)
    cos = cos_ref[...]
    sin = sin_ref[...]
    wi = QK_NOPE + 2 * LANES
    wo = QK_NOPE + LANES
    for hd in range(MLA_HEADS):
        qb = _dot(qa, wb_ref[:, hd * wi:(hd + 1) * wi])
        q_ref[:, hd * wo:hd * wo + QK_NOPE] = (qb[:, :QK_NOPE] * scale).astype(BF16)
        rp = qb[:, QK_NOPE:QK_NOPE + LANES] * cos + qb[:, QK_NOPE + LANES:] * sin
        q_ref[:, hd * wo + QK_NOPE:(hd + 1) * wo] = (rp * scale).astype(BF16)


def _mla_q(h, g, sh, sc, wa, gq, wb, cos, sin, S):
    T, D = h.shape
    tm = min(TOK_TILE, S)
    per_b = S // tm
    vec = pl.BlockSpec((1, 1, D), lambda i: (i // per_b, 0, 0))
    tok = lambda n: pl.BlockSpec((tm, n), lambda i: (i, 0))
    full = lambda a: pl.BlockSpec(a.shape, lambda i: (0, 0))
    qw = MLA_HEADS * (QK_NOPE + LANES)
    return pl.pallas_call(
        functools.partial(_q_body, scale=(QK_NOPE + QK_ROPE) ** -0.5 * LOG2E),
        out_shape=jax.ShapeDtypeStruct((T, qw), BF16),
        grid=(T // tm,),
        in_specs=[tok(D), full(g), vec, vec, full(wa), full(gq), full(wb), tok(LANES), tok(LANES)],
        out_specs=tok(qw),
        compiler_params=_cparams("parallel"),
        name="mla_q",
    )(h, g, sh, sc, wa, gq, wb, cos, sin)


def _flash_body(q_ref, k_ref, vt_ref, o_ref, s_ref, smax_ref, m_ref, l_ref, acc_ref, *, tq, ks, hp):
    i = pl.program_id(2)
    last = (i * tq + tq - 1) // ks
    dq = QK_NOPE + LANES
    heads = range(hp)

    def scores(hd, slot, c):
        kc = k_ref[pl.ds(pl.multiple_of(c * ks, ks), ks), hd * dq:(hd + 1) * dq]
        s = _dot_nt(kc, q_ref[:, hd * dq:(hd + 1) * dq])
        s_ref[hd, slot] = s
        smax_ref[hd, slot] = jnp.max(s, axis=0, keepdims=True)

    def update(hd, slot, c, masked):
        if masked:
            s = s_ref[hd, slot]
            kch = (c * ks + lax.broadcasted_iota(jnp.int32, s.shape, 0)) // CHUNK
            qch = (i * tq + lax.broadcasted_iota(jnp.int32, s.shape, 1)) // CHUNK
            s_ref[hd, slot] = jnp.where(kch <= qch, s, NEG_BIG)
            smax = jnp.max(s_ref[hd, slot], axis=0, keepdims=True)
        else:
            smax = smax_ref[hd, slot]
        m = m_ref[hd]
        m_new = jnp.maximum(m, smax)
        alpha = jnp.exp2(m - m_new)
        part = jnp.zeros((SUBLANES, tq), F32)
        ps = []
        for r in range(ks // ATT_SLAB):
            p = jnp.exp2(s_ref[hd, slot, r * ATT_SLAB:(r + 1) * ATT_SLAB, :] - m_new)
            for r8 in range(ATT_SLAB // SUBLANES):
                part = part + p[r8 * SUBLANES:(r8 + 1) * SUBLANES, :]
            ps.append(p.astype(BF16))
        m_ref[hd] = m_new
        l_ref[hd] = alpha * l_ref[hd] + jnp.sum(part, axis=0, keepdims=True)
        acc_ref[hd] = alpha * acc_ref[hd] + _dot(vt_ref[hd, c], jnp.concatenate(ps, axis=0))

    m_ref[...] = jnp.full_like(m_ref, NEG_BIG)
    l_ref[...] = jnp.zeros_like(l_ref)
    acc_ref[...] = jnp.zeros_like(acc_ref)
    for hd in heads:
        scores(hd, 0, 0)

    def pair(g, _):
        for hd in heads:
            scores(hd, 1, 2 * g + 1)
        for hd in heads:
            update(hd, 0, 2 * g, False)
        for hd in heads:
            scores(hd, 0, 2 * g + 2)
        for hd in heads:
            update(hd, 1, 2 * g + 1, False)
        return 0

    lax.fori_loop(0, last // 2, pair, 0)

    @pl.when(last % 2 == 1)
    def _():
        for hd in heads:
            scores(hd, 1, last)
        for hd in heads:
            update(hd, 0, last - 1, False)
        for hd in heads:
            update(hd, 1, last, True)

    @pl.when(last % 2 == 0)
    def _():
        for hd in heads:
            update(hd, 0, last, True)

    for hd in heads:
        o_ref[:, hd * V_HEAD:(hd + 1) * V_HEAD] = (acc_ref[hd] / l_ref[hd]).T.astype(o_ref.dtype)


def _flash(q, k, vt, S):
    T = q.shape[0]
    B = T // S
    H = MLA_HEADS
    hp = ATT_HEADS
    tq = min(ATT_TILE, S)
    ks = ATT_KEYS
    nq = S // tq
    dq = QK_NOPE + LANES
    return pl.pallas_call(
        functools.partial(_flash_body, tq=tq, ks=ks, hp=hp),
        out_shape=jax.ShapeDtypeStruct((T, H * V_HEAD), BF16),
        grid=(B, H // hp, nq),
        in_specs=[pl.BlockSpec((tq, hp * dq), lambda b, h, i: (b * nq + i, h)),
                  pl.BlockSpec((S, hp * dq), lambda b, h, i: (b, h)),
                  pl.BlockSpec((hp, S // ks, V_HEAD, ks), lambda b, h, i: (h, b, 0, 0))],
        out_specs=pl.BlockSpec((tq, hp * V_HEAD), lambda b, h, i: (b * nq + i, h)),
        scratch_shapes=[pltpu.VMEM((hp, 2, ks, tq), F32), pltpu.VMEM((hp, 2, 1, tq), F32),
                        pltpu.VMEM((hp, 1, tq), F32), pltpu.VMEM((hp, 1, tq), F32),
                        pltpu.VMEM((hp, V_HEAD, tq), F32)],
        compiler_params=_cparams("parallel", "parallel", "arbitrary"),
        name="flash_attn",
    )(q, k, vt)


def _rope_swap(w):
    half = w.shape[-1] // 2
    return jnp.concatenate([w[..., half:], w[..., :half]], axis=-1)


def _pad_lanes(w):
    return jnp.pad(w, [(0, 0)] * (w.ndim - 1) + [(0, LANES - w.shape[-1])])


def kernel(x, c, positions, w_mod, b_mod, norm_mix, norm_ffn, ret_w_in, ret_w_out, w_mod_kv, b_mod_kv, norm_kv, mla_w_kv_a, mla_kv_norm, mla_w_kv_b, mla_w_q_a, mla_q_norm, mla_w_q_b, mla_w_o, router_w, router_b, moe_w_gate, moe_w_up, moe_w_down, final_norm):
    B, S, D = x.shape
    T = B * S
    depth = w_mod.shape[0]
    n_a = ret_w_in.shape[0]

    c8 = jnp.pad(c, ((0, 8 - B), (0, 0)))
    mod = _mod_vectors(c8, w_mod, b_mod)[:, :B]
    mod = mod.reshape(depth, B, 6, 1, D)
    kv_mod = _mod_vectors(c8, w_mod_kv[None], b_mod_kv[None])[0, :B].reshape(B, 2, 1, D)

    pos_col = positions.reshape(T, 1)
    ones = jnp.ones((1, LANES), F32)
    inv_ret = (ROPE_THETA ** (-jnp.arange(LANES, dtype=F32) / LANES)).reshape(1, LANES)
    cos_r, sin_r = _rope_tables(pos_col, inv_ret, ones, ones)
    hr = QK_ROPE // 2
    inv_m = ROPE_THETA ** (-jnp.arange(hr, dtype=F32) / hr)
    inv_m = _pad_lanes(jnp.concatenate([inv_m, inv_m])[None])
    cm = _pad_lanes(jnp.ones((1, QK_ROPE), F32))
    sm = _pad_lanes(jnp.concatenate([-jnp.ones((1, hr), F32), jnp.ones((1, hr), F32)], axis=-1))
    cos_m, sin_m = _rope_tables(pos_col, inv_m, cm, sm)

    wr_hi = router_w.T.astype(BF16)
    wr_t = jnp.concatenate([wr_hi, (router_w.T - wr_hi.astype(F32)).astype(BF16)], axis=0)
    br = router_b.reshape(N_EXPERTS, 1)

    h = x.reshape(T, D)
    lay = _MoeLayout(T, min(TOK_TILE, S))
    k_full = v_full = None
    for layer in range(depth):
        sh1, sc1, g1, sh2, sc2, g2 = (mod[layer, :, i] for i in range(6))
        gmix = norm_mix[layer].reshape(1, D)
        if layer < n_a:
            proj = _ret_inproj(h, gmix, sh1, sc1, ret_w_in[layer].astype(BF16), S)
            mix = _retention(proj, cos_r, sin_r, S, D)
            w_o = ret_w_out[layer].astype(BF16)
        else:
            if layer == n_a:
                wa = mla_w_kv_a
                wa_r = wa[:, KV_LORA:]
                wa_p = jnp.concatenate([wa[:, :KV_LORA], _pad_lanes(wa_r), _pad_lanes(_rope_swap(wa_r))],
                                       axis=-1).astype(BF16)
                wb = mla_w_kv_b.reshape(KV_LORA, MLA_HEADS, QK_NOPE + V_HEAD)
                wb_p = jnp.concatenate([wb[..., :QK_NOPE].reshape(KV_LORA, -1),
                                        wb[..., QK_NOPE:].reshape(KV_LORA, -1)], axis=-1).astype(BF16)
                k_full, v_full = _mla_kv(h, norm_kv.reshape(1, D), kv_mod[:, 0], kv_mod[:, 1], wa_p,
                                         mla_kv_norm.reshape(1, KV_LORA), wb_p, cos_m, sin_m, S)
            j = layer - n_a
            wq = mla_w_q_b[j].reshape(Q_LORA, MLA_HEADS, QK_NOPE + QK_ROPE)
            wq_r = wq[..., QK_NOPE:]
            wq_p = jnp.concatenate([wq[..., :QK_NOPE], _pad_lanes(wq_r), _pad_lanes(_rope_swap(wq_r))],
                                   axis=-1).reshape(Q_LORA, -1).astype(BF16)
            q_full = _mla_q(h, gmix, sh1, sc1, mla_w_q_a[j].astype(BF16),
                            mla_q_norm[j].reshape(1, Q_LORA), wq_p, cos_m, sin_m, S)
            mix = _flash(q_full, k_full, v_full, S)
            w_o = mla_w_o[j].astype(BF16)
        h, xn, rows, cols, tbl, used = _outproj_route(mix, w_o, h, g1, norm_ffn[layer].reshape(1, D),
                                                      sh2, sc2, wr_t, br, S, lay)
        h = _moe(xn, rows, cols, tbl, used[:, 0], moe_w_gate, moe_w_up, moe_w_down, layer,
                 h, g2, S, lay, final_norm.reshape(1, D) if layer == depth - 1 else None)
    return h.reshape(B, S, D)
```

```python
import functools

import jax
import jax.numpy as jnp
from jax import lax
from jax.experimental import pallas as pl
from jax.experimental.pallas import tpu as pltpu

F32 = jnp.float32
BF16 = jnp.bfloat16

CHUNK = 64
RET_HEADS = 4
MLA_HEADS = 8
QK_NOPE = 128
QK_ROPE = 64
V_HEAD = 128
Q_LORA = 256
KV_LORA = 128
N_EXPERTS = 16
N_GROUPS = 4
EXPERTS_PER_GROUP = N_EXPERTS // N_GROUPS
D_EXPERT = 512
ROPE_THETA = 10000.0
EPS = 1e-6

LANES = 128
VMEM_LIMIT = 56 * 1024 * 1024
NEG_BIG = -1e30
LOG2E = 1.4426950408889634

RET_CHUNK = 256
TOK_TILE = 512
ATT_TILE = 1024
ATT_KEYS = 512
ATT_HEADS = 2
ATT_SLAB = 32
SUBLANES = 8
MXU_DIM = 256
MOE_GROUP = 16
MOE_ROW_TILE = 512


def _cparams(*sem):
    return pltpu.CompilerParams(dimension_semantics=sem, vmem_limit_bytes=VMEM_LIMIT)


def _silu(x):
    return x * jax.nn.sigmoid(x)


def _rms(x, g):
    return x * lax.rsqrt(jnp.mean(x * x, axis=-1, keepdims=True) + EPS) * g


def _norm_mod(h, g, shift, scale):
    return _rms(h, g) * (1.0 + scale) + shift


def _dot(a, b):
    return jnp.dot(a, b, preferred_element_type=F32)


def _dot_nt(a, b, **kw):
    return lax.dot_general(a, b, (((1,), (1,)), ((), ())), preferred_element_type=F32, **kw)


def _dot_tn(a, b):
    return lax.dot_general(a, b, (((0,), (0,)), ((), ())), preferred_element_type=F32)


def _mod_body(c_ref, w_ref, b_ref, o_ref):
    ca = _silu(c_ref[...])
    o_ref[0] = jnp.dot(ca, w_ref[0], preferred_element_type=F32,
                       precision=lax.Precision.HIGHEST) + b_ref[0]


def _mod_vectors(c8, w, b):
    L, D, N = w.shape
    tn = D
    assert N % tn == 0
    return pl.pallas_call(
        _mod_body,
        out_shape=jax.ShapeDtypeStruct((L, 8, N), F32),
        grid=(L, N // tn),
        in_specs=[pl.BlockSpec((8, D), lambda l, j: (0, 0)),
                  pl.BlockSpec((1, D, tn), lambda l, j: (l, 0, j)),
                  pl.BlockSpec((1, 1, tn), lambda l, j: (l, 0, j))],
        out_specs=pl.BlockSpec((1, 8, tn), lambda l, j: (l, 0, j)),
        compiler_params=_cparams("parallel", "parallel"),
        name="mod_vectors",
    )(c8, w, b.reshape(L, 1, N))


def _rope_body(pos_ref, inv_ref, cm_ref, sm_ref, cos_ref, sin_ref):
    ang = pos_ref[...].astype(F32) * inv_ref[...]
    cos_ref[...] = jnp.cos(ang) * cm_ref[...]
    sin_ref[...] = jnp.sin(ang) * sm_ref[...]


def _rope_tables(pos_col, inv, cm, sm):
    T = pos_col.shape[0]
    tm = min(T, 1024)
    row = pl.BlockSpec((1, LANES), lambda i: (0, 0))
    return pl.pallas_call(
        _rope_body,
        out_shape=(jax.ShapeDtypeStruct((T, LANES), F32),) * 2,
        grid=(T // tm,),
        in_specs=[pl.BlockSpec((tm, 1), lambda i: (i, 0)), row, row, row],
        out_specs=(pl.BlockSpec((tm, LANES), lambda i: (i, 0)),) * 2,
        compiler_params=_cparams("parallel"),
        name="rope_tables",
    )(pos_col, inv, cm, sm)


def _inproj_body(h_ref, g_ref, sh_ref, sc_ref, w_ref, cos_ref, sin_ref, o_ref, *, tn, dk_dim):
    xn = _norm_mod(h_ref[...], g_ref[...], sh_ref[0], sc_ref[0]).astype(BF16)
    d_model = h_ref.shape[1]
    half = dk_dim // 2
    for j in range(w_ref.shape[1] // tn):
        acc = _dot(xn, w_ref[:, j * tn:(j + 1) * tn])
        if j * tn >= 2 * d_model:
            o_ref[:, j * tn:(j + 1) * tn] = acc.astype(BF16)
            continue
        cos = cos_ref[...]
        sin = sin_ref[...]
        scale = 1.0 if j * tn < d_model else dk_dim ** -0.5
        for hd in range(tn // dk_dim):
            x1 = acc[:, hd * dk_dim:hd * dk_dim + half]
            x2 = acc[:, hd * dk_dim + half:(hd + 1) * dk_dim]
            c0 = j * tn + hd * dk_dim
            o_ref[:, c0:c0 + half] = ((x1 * cos - x2 * sin) * scale).astype(BF16)
            o_ref[:, c0 + half:c0 + dk_dim] = ((x1 * sin + x2 * cos) * scale).astype(BF16)


def _ret_inproj(h, g, sh, sc, w, cos, sin, S):
    T, D = h.shape
    N = w.shape[1]
    tm = min(TOK_TILE, S)
    per_b = S // tm
    vec = pl.BlockSpec((1, 1, D), lambda i: (i // per_b, 0, 0))
    tab = pl.BlockSpec((tm, LANES), lambda i: (i, 0))
    return pl.pallas_call(
        functools.partial(_inproj_body, tn=512, dk_dim=D // RET_HEADS),
        out_shape=jax.ShapeDtypeStruct((T, N), BF16),
        grid=(T // tm,),
        in_specs=[pl.BlockSpec((tm, D), lambda i: (i, 0)),
                  pl.BlockSpec((1, D), lambda i: (0, 0)),
                  vec, vec,
                  pl.BlockSpec((D, N), lambda i: (0, 0)),
                  tab, tab],
        out_specs=pl.BlockSpec((tm, N), lambda i: (i, 0)),
        compiler_params=_cparams("parallel"),
        name="ret_inproj",
    )(h, g, sh, sc, w, cos, sin)


def _ret_body(q_ref, k_ref, v_ref, g_ref, di_ref, dq_ref, dk_ref, dc_ref,
              y_ref, state_ref, *, dk_dim):
    @pl.when(pl.program_id(1) == 0)
    def _():
        state_ref[...] = jnp.zeros_like(state_ref)

    dv_dim = 2 * dk_dim
    for hd in range(RET_HEADS):
        qk = slice(hd * dk_dim, (hd + 1) * dk_dim)
        vg = slice(hd * dv_dim, (hd + 1) * dv_dim)
        qb = q_ref[:, qk]
        kb = k_ref[:, qk]
        v = v_ref[:, vg]
        inner = (_dot_nt(qb, kb) * di_ref[hd]).astype(BF16)
        st = state_ref[hd]
        out = _dot(inner, v) + _dot(qb, st.astype(BF16)) * dq_ref[hd]
        kd = (kb.astype(F32) * dk_ref[hd]).astype(BF16)
        state_ref[hd] = st * dc_ref[hd] + _dot_tn(kd, v)

        mu = jnp.mean(out, axis=-1, keepdims=True)
        cen = out - mu
        var = jnp.mean(cen * cen, axis=-1, keepdims=True)
        o = cen * lax.rsqrt(var + EPS)
        y_ref[:, vg] = (_silu(g_ref[:, vg].astype(F32)) * o).astype(BF16)


def _retention(proj, S, D):
    T = proj.shape[0]
    B = T // S
    H = RET_HEADS
    dk = D // H
    dv = 2 * dk
    C = min(RET_CHUNK, S)
    n = S // C
    log_g = jnp.log1p(-(2.0 ** (-5.0 - jnp.arange(H, dtype=F32))))
    t = jnp.arange(C, dtype=F32)
    diff = t[:, None] - t[None, :]
    d_intra = jnp.where(diff >= 0, jnp.exp(log_g[:, None, None] * jnp.maximum(diff, 0.0)), 0.0)
    d_q = jnp.exp(log_g[:, None] * (t + 1.0))[:, :, None]
    d_k = jnp.exp(log_g[:, None] * (C - 1.0 - t))[:, :, None]
    d_c = jnp.exp(log_g * C)[:, None, None]

    assert 2 * D == H * dv
    row = lambda b, i: b * n + i
    const = lambda a: pl.BlockSpec(a.shape, lambda b, i: (0, 0, 0))
    return pl.pallas_call(
        functools.partial(_ret_body, dk_dim=dk),
        out_shape=jax.ShapeDtypeStruct((T, H * dv), BF16),
        grid=(B, n),
        in_specs=[pl.BlockSpec((C, D), lambda b, i: (row(b, i), 0)),
                  pl.BlockSpec((C, D), lambda b, i: (row(b, i), 1)),
                  pl.BlockSpec((C, H * dv), lambda b, i: (row(b, i), 1)),
                  pl.BlockSpec((C, H * dv), lambda b, i: (row(b, i), 2)),
                  const(d_intra), const(d_q), const(d_k), const(d_c)],
        out_specs=pl.BlockSpec((C, H * dv), lambda b, i: (row(b, i), 0)),
        scratch_shapes=[pltpu.VMEM((H, dk, dv), F32)],
        compiler_params=_cparams("parallel", "arbitrary"),
        name="retention",
    )(proj, proj, proj, proj, d_intra, d_q, d_k, d_c)


def _route(logits_t, bias):
    sc = jax.nn.sigmoid(logits_t)
    bi = sc + bias
    s_rows = [sc[e:e + 1, :] for e in range(N_EXPERTS)]
    b_rows = [bi[e:e + 1, :] for e in range(N_EXPERTS)]

    def top2sum(a, b, c, d):
        p, q = jnp.maximum(a, b), jnp.minimum(a, b)
        r, s = jnp.maximum(c, d), jnp.minimum(c, d)
        return jnp.maximum(p, r) + jnp.maximum(jnp.minimum(p, r), jnp.maximum(q, s))

    n = EXPERTS_PER_GROUP
    gs = [top2sum(*b_rows[n * g:n * g + n]) for g in range(N_GROUPS)]
    best, gi = gs[0], jnp.zeros_like(gs[0], dtype=jnp.int32)
    for g in range(1, N_GROUPS):
        upd = gs[g] > best
        gi = jnp.where(upd, g, gi)
        best = jnp.where(upd, gs[g], best)

    def pick(rows, j):
        out = rows[j]
        for g in range(1, N_GROUPS):
            out = jnp.where(gi == g, rows[n * g + j], out)
        return out

    vb = [pick(b_rows, j) for j in range(n)]
    vs = [pick(s_rows, j) for j in range(n)]

    def argmax_first(vals):
        best, idx = vals[0], jnp.zeros_like(gi)
        for j in range(1, n):
            upd = vals[j] > best
            idx = jnp.where(upd, j, idx)
            best = jnp.where(upd, vals[j], best)
        return idx

    i1 = argmax_first(vb)
    i2 = argmax_first([jnp.where(i1 == j, -jnp.inf, vb[j]) for j in range(n)])

    def take(vals, idx):
        out = vals[0]
        for j in range(1, n):
            out = jnp.where(idx == j, vals[j], out)
        return out

    w1, w2 = take(vs, i1), take(vs, i2)
    tot = w1 + w2
    w1, w2 = w1 / tot, w2 / tot
    return gi * n + i1, gi * n + i2, w1, w2


class _MoeLayout:
    def __init__(self, T, tm):
        self.tm = tm
        self.n_tiles = T // tm
        self.group = MOE_GROUP
        self.rt = -(-(2 * tm + N_EXPERTS * (MOE_GROUP - 1)) // MXU_DIM) * MXU_DIM
        self.ng = self.rt // MOE_GROUP
        assert self.ng < LANES
        self.tmx = MOE_ROW_TILE
        self.cap = -(-(T + self.n_tiles * MOE_GROUP) // self.tmx) * self.tmx
        self.dump = N_EXPERTS * self.cap
        self.rows = self.dump + self.rt
        pad = self.n_tiles * N_EXPERTS * (MOE_GROUP - 1)
        self.nt = (2 * T + pad) // self.tmx + N_EXPERTS


def _dispatch_meta(e1, e2, cum, tri, lay):
    E, G = N_EXPERTS, lay.group
    tm = e1.shape[1]
    eid = lax.broadcasted_iota(jnp.int32, (E, tm), 0)
    oh1, oh2 = eid == e1, eid == e2
    cnt = jnp.where(oh1 | oh2, 1.0, 0.0)
    pre = _dot(cnt.astype(BF16), tri)
    tot = jnp.sum(cnt, axis=1, keepdims=True)
    ptot = jnp.broadcast_to(jnp.ceil(tot * (1.0 / G)) * G, (E, LANES))
    below = jnp.where(lax.broadcasted_iota(jnp.int32, (E, E), 0) > lax.broadcasted_iota(jnp.int32, (E, E), 1),
                      1.0, 0.0)
    loff = jnp.dot(below, ptot, preferred_element_type=F32, precision=lax.Precision.HIGHEST)
    pos_e = loff[:, :1] + pre
    pos1 = jnp.sum(jnp.where(oh1, pos_e, 0.0), axis=0, keepdims=True)
    pos2 = jnp.sum(jnp.where(oh2, pos_e, 0.0), axis=0, keepdims=True)

    lane = lax.broadcasted_iota(jnp.int32, (E, LANES), 1)
    g_row = (lane * G).astype(F32)
    eg = jnp.sum(jnp.where(loff + ptot <= g_row, 1, 0), axis=0, keepdims=True)
    erow = lax.broadcasted_iota(jnp.int32, (E, LANES), 0)
    base = erow.astype(F32) * float(lay.cap) + cum - loff
    sel = jnp.sum(jnp.where(erow == eg, base, 0.0), axis=0, keepdims=True)
    dst = jnp.where(eg < E, g_row[:1] + sel, float(lay.dump) + g_row[:1])
    n_used = jnp.sum(ptot[:, :1], axis=0, keepdims=True) * (1.0 / G)
    table = jnp.where(lane[:1] == LANES - 1, n_used, dst).astype(jnp.int32)
    return pos1, pos2, table, cum + ptot


def _outproj_body(y_ref, w_ref, h_ref, g1_ref, gn_ref, sh_ref, sc_ref, wr_ref, br_ref, tri_ref,
                  ho_ref, xn_ref, rows_ref, cols_ref, tbl_ref, cum_ref, *, lay):
    @pl.when(pl.program_id(0) == 0)
    def _():
        cum_ref[...] = jnp.zeros_like(cum_ref)

    hn = h_ref[...] + g1_ref[0] * _dot(y_ref[...], w_ref[...])
    ho_ref[...] = hn
    xn = _norm_mod(hn, gn_ref[...], sh_ref[0], sc_ref[0])
    xb = xn.astype(BF16)
    xn_ref[...] = xb
    xl = (xn - xb.astype(F32)).astype(BF16)
    hl = _dot_nt(wr_ref[...], xb)
    logits_t = hl[:N_EXPERTS] + hl[N_EXPERTS:] + _dot_nt(wr_ref[:N_EXPERTS, :], xl)
    e1, e2, w1, w2 = _route(logits_t, br_ref[...])
    pos1, pos2, table, cum = _dispatch_meta(e1, e2, cum_ref[...], tri_ref[...], lay)
    cum_ref[...] = cum
    tbl_ref[0] = table
    tm = e1.shape[1]
    rows = jnp.concatenate([pos1, pos2, w1, w2, jnp.zeros((SUBLANES - 4, tm), F32)], axis=0)
    rows_ref[...] = rows
    cols_ref[...] = jnp.concatenate([rows, jnp.zeros((LANES - SUBLANES, tm), F32)], axis=0).T


def _outproj_route(y, w, h, g1, gn, sh, sc, wr_t, br, S, lay):
    T, D = h.shape
    K = y.shape[1]
    tm = lay.tm
    per_b = S // tm
    vec = pl.BlockSpec((1, 1, D), lambda i: (i // per_b, 0, 0))
    tok = lambda n: pl.BlockSpec((tm, n), lambda i: (i, 0))
    tri = jnp.triu(jnp.ones((tm, tm), BF16), k=1)
    return pl.pallas_call(
        functools.partial(_outproj_body, lay=lay),
        out_shape=(jax.ShapeDtypeStruct((T, D), F32),
                   jax.ShapeDtypeStruct((T, D), BF16),
                   jax.ShapeDtypeStruct((SUBLANES, T), F32),
                   jax.ShapeDtypeStruct((T, LANES), F32),
                   jax.ShapeDtypeStruct((lay.n_tiles, 1, LANES), jnp.int32),
                   jax.ShapeDtypeStruct((N_EXPERTS, LANES), F32)),
        grid=(T // tm,),
        in_specs=[tok(K),
                  pl.BlockSpec((K, D), lambda i: (0, 0)),
                  tok(D), vec,
                  pl.BlockSpec((1, D), lambda i: (0, 0)),
                  vec, vec,
                  pl.BlockSpec((2 * N_EXPERTS, D), lambda i: (0, 0)),
                  pl.BlockSpec((N_EXPERTS, 1), lambda i: (0, 0)),
                  pl.BlockSpec((tm, tm), lambda i: (0, 0))],
        out_specs=(tok(D), tok(D),
                   pl.BlockSpec((SUBLANES, tm), lambda i: (0, i)),
                   tok(LANES),
                   pl.BlockSpec((1, 1, LANES), lambda i: (i, 0, 0)),
                   pl.BlockSpec((N_EXPERTS, LANES), lambda i: (0, 0))),
        compiler_params=_cparams("arbitrary"),
        name="outproj_route",
    )(y, w, h, g1, gn, sh, sc, wr_t, br, tri)


def _group_copy(hbm_ref, buf_ref, tbl_ref, i, g, sem, lay, to_hbm):
    G = lay.group
    hbm = hbm_ref.at[pl.ds(pl.multiple_of(tbl_ref[i * LANES + g], G), G), :]
    vmem = buf_ref.at[pl.ds(pl.multiple_of(g * G, G), G), :]
    return pltpu.make_async_copy(vmem, hbm, sem) if to_hbm else pltpu.make_async_copy(hbm, vmem, sem)


def _dispatch_body(tbl_ref, x_ref, rows_ref, xs_ref, buf_ref, sem, *, lay):
    i = pl.program_id(0)
    pos1 = rows_ref[0:1, :].astype(jnp.int32)
    pos2 = rows_ref[1:2, :].astype(jnp.int32)
    r = lax.broadcasted_iota(jnp.int32, (lay.rt, lay.tm), 0)
    perm = jnp.where((r == pos1) | (r == pos2), 1.0, 0.0).astype(BF16)
    slot = i % 2
    buf_ref[slot] = _dot(perm, x_ref[...]).astype(BF16)

    def copies(tile, slot_):
        return [_group_copy(xs_ref, buf_ref.at[slot_], tbl_ref, tile, g, sem.at[slot_], lay, True)
                for g in range(lay.ng)]

    @pl.when(i > 0)
    def _():
        for c in copies(i - 1, 1 - slot):
            c.wait()

    for c in copies(i, slot):
        c.start()

    @pl.when(i == pl.num_programs(0) - 1)
    def _():
        for c in copies(i, slot):
            c.wait()


def _moe_dispatch(xn, rows, tbl, lay):
    T, D = xn.shape
    tm = lay.tm
    grid_spec = pltpu.PrefetchScalarGridSpec(
        num_scalar_prefetch=1,
        grid=(lay.n_tiles,),
        in_specs=[pl.BlockSpec((tm, D), lambda i, tbl: (i, 0)),
                  pl.BlockSpec((SUBLANES, tm), lambda i, tbl: (0, i))],
        out_specs=pl.BlockSpec(memory_space=pl.ANY),
        scratch_shapes=[pltpu.VMEM((2, lay.rt, D), BF16), pltpu.SemaphoreType.DMA((2,))],
    )
    return pl.pallas_call(
        functools.partial(_dispatch_body, lay=lay),
        out_shape=jax.ShapeDtypeStruct((lay.rows, D), BF16),
        grid_spec=grid_spec,
        compiler_params=_cparams("arbitrary"),
        name="moe_dispatch",
    )(tbl, xn, rows)


def _expert_body(te_ref, tb_ref, nv_ref, x_ref, wg_ref, wu_ref, wd_ref, y_ref, wgu_bf, wd_bf):
    n = pl.program_id(0)
    f = wg_ref.shape[2]

    @pl.when((n == 0) | (te_ref[n] != te_ref[jnp.maximum(n - 1, 0)]))
    def _():
        wgu_bf[:, :f] = wg_ref[0].astype(BF16)
        wgu_bf[:, f:] = wu_ref[0].astype(BF16)
        wd_bf[...] = wd_ref[0].astype(BF16)

    @pl.when(n < nv_ref[0])
    def _():
        hgu = _dot(x_ref[...], wgu_bf[...])
        hdn = (_silu(hgu[:, :f]) * hgu[:, f:]).astype(BF16)
        y_ref[...] = _dot(hdn, wd_bf[...]).astype(BF16)


def _moe_experts(xs, wg, wu, wd, layer, tile_e, tile_blk, n_valid, lay):
    R, D = xs.shape
    F = wg.shape[3]
    tmx = lay.tmx
    wspec = lambda shape: pl.BlockSpec((None, 1) + shape, lambda n, te, tb, nv: (layer, te[n], 0, 0))
    grid_spec = pltpu.PrefetchScalarGridSpec(
        num_scalar_prefetch=3,
        grid=(lay.nt,),
        in_specs=[pl.BlockSpec((tmx, D), lambda n, te, tb, nv: (tb[n], 0)),
                  wspec((D, F)), wspec((D, F)), wspec((F, D))],
        out_specs=pl.BlockSpec((tmx, D), lambda n, te, tb, nv: (tb[n], 0)),
        scratch_shapes=[pltpu.VMEM((D, 2 * F), BF16), pltpu.VMEM((F, D), BF16)],
    )
    return pl.pallas_call(
        _expert_body,
        out_shape=jax.ShapeDtypeStruct((R, D), BF16),
        grid_spec=grid_spec,
        compiler_params=_cparams("arbitrary"),
        name="moe_experts",
    )(tile_e, tile_blk, n_valid, xs, wg, wu, wd)


def _combine_body(tbl_ref, cols_ref, h_ref, g2_ref, *rest, lay, final):
    if final:
        gfin_ref, ys_ref, o_ref, buf_ref, sem = rest
    else:
        ys_ref, o_ref, buf_ref, sem = rest
    i = pl.program_id(0)
    slot = i % 2

    def fetch(tile, slot_, start):
        def one(g, _):
            c = _group_copy(ys_ref, buf_ref.at[slot_], tbl_ref, tile, g, sem.at[slot_], lay, False)
            c.start() if start else c.wait()
            return 0
        lax.fori_loop(0, tbl_ref[tile * LANES + LANES - 1], one, 0)

    @pl.when(i == 0)
    def _():
        buf_ref[...] = jnp.zeros_like(buf_ref)
        fetch(0, 0, True)

    @pl.when(i + 1 < pl.num_programs(0))
    def _():
        fetch(i + 1, 1 - slot, True)

    fetch(i, slot, False)

    cols = cols_ref[...]
    r = lax.broadcasted_iota(jnp.int32, (lay.tm, lay.rt), 1)
    mix = (jnp.where(r == cols[:, 0:1].astype(jnp.int32), cols[:, 2:3], 0.0)
           + jnp.where(r == cols[:, 1:2].astype(jnp.int32), cols[:, 3:4], 0.0)).astype(BF16)
    out = h_ref[...] + g2_ref[0] * _dot(mix, buf_ref[slot])
    o_ref[...] = _rms(out, gfin_ref[...]) if final else out


def _moe_combine(ys, cols, tbl, h, g2, S, lay, final_gain=None):
    T, D = h.shape
    tm = lay.tm
    per_b = S // tm
    final = final_gain is not None
    in_specs = [pl.BlockSpec((tm, LANES), lambda i, tbl: (i, 0)),
                pl.BlockSpec((tm, D), lambda i, tbl: (i, 0)),
                pl.BlockSpec((1, 1, D), lambda i, tbl: (i // per_b, 0, 0))]
    args = [cols, h, g2]
    if final:
        in_specs.append(pl.BlockSpec((1, D), lambda i, tbl: (0, 0)))
        args.append(final_gain)
    grid_spec = pltpu.PrefetchScalarGridSpec(
        num_scalar_prefetch=1,
        grid=(lay.n_tiles,),
        in_specs=in_specs + [pl.BlockSpec(memory_space=pl.ANY)],
        out_specs=pl.BlockSpec((tm, D), lambda i, tbl: (i, 0)),
        scratch_shapes=[pltpu.VMEM((2, lay.rt, D), BF16), pltpu.SemaphoreType.DMA((2,))],
    )
    return pl.pallas_call(
        functools.partial(_combine_body, lay=lay, final=final),
        out_shape=jax.ShapeDtypeStruct((T, D), F32),
        grid_spec=grid_spec,
        compiler_params=_cparams("arbitrary"),
        name="moe_combine",
    )(tbl, *args, ys)


def _expert_tiles(used, lay):
    tiles = jnp.ceil(used / lay.tmx).astype(jnp.int32)
    ends = jnp.cumsum(tiles)
    n_valid = ends[-1]
    n = jnp.minimum(jnp.arange(lay.nt, dtype=jnp.int32), n_valid - 1)
    e = jnp.sum((ends[None, :] <= n[:, None]).astype(jnp.int32), axis=1)
    blk = e * (lay.cap // lay.tmx) + n - (ends - tiles)[e]
    return e, blk, n_valid.reshape(1)


def _moe(xn, rows, cols, tbl, used, wg, wu, wd, layer, h, g2, S, lay, final_gain=None):
    tbl = tbl.reshape(-1)
    xs = _moe_dispatch(xn, rows, tbl, lay)
    ys = _moe_experts(xs, wg, wu, wd, layer, *_expert_tiles(used, lay), lay)
    return _moe_combine(ys, cols, tbl, h, g2, S, lay, final_gain)


def _kv_body(h_ref, g_ref, sh_ref, sc_ref, wa_ref, gkv_ref, wb_ref, cos_ref, sin_ref,
             k_ref, vt_ref):
    hn = _norm_mod(h_ref[...], g_ref[...], sh_ref[0], sc_ref[0]).astype(BF16)
    a = _dot(hn, wa_ref[...])
    c_kv = _rms(a[:, :KV_LORA], gkv_ref[...]).astype(BF16)
    kr = (a[:, KV_LORA:KV_LORA + LANES] * cos_ref[...]
          + a[:, KV_LORA + LANES:] * sin_ref[...]).astype(BF16)
    kv = _dot(c_kv, wb_ref[...])
    hk = MLA_HEADS * QK_NOPE
    w = QK_NOPE + LANES
    for hd in range(MLA_HEADS):
        k_ref[:, hd * w:hd * w + QK_NOPE] = kv[:, hd * QK_NOPE:(hd + 1) * QK_NOPE].astype(BF16)
        k_ref[:, hd * w + QK_NOPE:(hd + 1) * w] = kr
        vh = kv[:, hk + hd * V_HEAD:hk + (hd + 1) * V_HEAD]
        for g in range(vt_ref.shape[1]):
            vt_ref[hd, g] = vh[g * ATT_KEYS:(g + 1) * ATT_KEYS, :].T.astype(BF16)


def _mla_kv(h, g, sh, sc, wa, gkv, wb, cos, sin, S):
    T, D = h.shape
    tm = min(TOK_TILE, S)
    per_b = S // tm
    vec = pl.BlockSpec((1, 1, D), lambda i: (i // per_b, 0, 0))
    tok = lambda n: pl.BlockSpec((tm, n), lambda i: (i, 0))
    full = lambda a: pl.BlockSpec(a.shape, lambda i: (0, 0))
    kw = MLA_HEADS * (QK_NOPE + LANES)
    gk = tm // ATT_KEYS
    vt_shape = (MLA_HEADS, T // ATT_KEYS, V_HEAD, ATT_KEYS)
    return pl.pallas_call(
        _kv_body,
        out_shape=(jax.ShapeDtypeStruct((T, kw), BF16), jax.ShapeDtypeStruct(vt_shape, BF16)),
        grid=(T // tm,),
        in_specs=[tok(D), full(g), vec, vec, full(wa), full(gkv), full(wb), tok(LANES), tok(LANES)],
        out_specs=(tok(kw),
                   pl.BlockSpec((MLA_HEADS, gk, V_HEAD, ATT_KEYS), lambda i: (0, i, 0, 0))),
        compiler_params=_cparams("parallel"),
        name="mla_kv",
    )(h, g, sh, sc, wa, gkv, wb, cos, sin)


def _q_body(h_ref, g_ref, sh_ref, sc_ref, wa_ref, gq_ref, wb_ref, cos_ref, sin_ref, q_ref, *, scale):
    xn = _norm_mod(h_ref[...], g_ref[...], sh_ref[0], sc_ref[0]).astype(BF16)
    qa = _rms(_dot(xn, wa_ref[...]), gq_ref[...]).astype(BF16)
    cos = cos_ref[...]
    sin = sin_ref[...]
    wi = QK_NOPE + 2 * LANES
    wo = QK_NOPE + LANES
    for hd in range(MLA_HEADS):
        qb = _dot(qa, wb_ref[:, hd * wi:(hd + 1) * wi])
        q_ref[:, hd * wo:hd * wo + QK_NOPE] = (qb[:, :QK_NOPE] * scale).astype(BF16)
        rp = qb[:, QK_NOPE:QK_NOPE + LANES] * cos + qb[:, QK_NOPE + LANES:] * sin
        q_ref[:, hd * wo + QK_NOPE:(hd + 1) * wo] = (rp * scale).astype(BF16)


def _mla_q(h, g, sh, sc, wa, gq, wb, cos, sin, S):
    T, D = h.shape
    tm = min(TOK_TILE, S)
    per_b = S // tm
    vec = pl.BlockSpec((1, 1, D), lambda i: (i // per_b, 0, 0))
    tok = lambda n: pl.BlockSpec((tm, n), lambda i: (i, 0))
    full = lambda a: pl.BlockSpec(a.shape, lambda i: (0, 0))
    qw = MLA_HEADS * (QK_NOPE + LANES)
    return pl.pallas_call(
        functools.partial(_q_body, scale=(QK_NOPE + QK_ROPE) ** -0.5 * LOG2E),
        out_shape=jax.ShapeDtypeStruct((T, qw), BF16),
        grid=(T // tm,),
        in_specs=[tok(D), full(g), vec, vec, full(wa), full(gq), full(wb), tok(LANES), tok(LANES)],
        out_specs=tok(qw),
        compiler_params=_cparams("parallel"),
        name="mla_q",
    )(h, g, sh, sc, wa, gq, wb, cos, sin)


def _flash_body(q_ref, k_ref, vt_ref, o_ref, s_ref, smax_ref, m_ref, l_ref, acc_ref, *, tq, ks, hp):
    i = pl.program_id(2)
    last = (i * tq + tq - 1) // ks
    dq = QK_NOPE + LANES
    heads = range(hp)

    def scores(hd, slot, c):
        kc = k_ref[pl.ds(pl.multiple_of(c * ks, ks), ks), hd * dq:(hd + 1) * dq]
        s = _dot_nt(kc, q_ref[:, hd * dq:(hd + 1) * dq])
        s_ref[hd, slot] = s
        smax_ref[hd, slot] = jnp.max(s, axis=0, keepdims=True)

    def update(hd, slot, c, masked):
        if masked:
            s = s_ref[hd, slot]
            kch = (c * ks + lax.broadcasted_iota(jnp.int32, s.shape, 0)) // CHUNK
            qch = (i * tq + lax.broadcasted_iota(jnp.int32, s.shape, 1)) // CHUNK
            s_ref[hd, slot] = jnp.where(kch <= qch, s, NEG_BIG)
            smax = jnp.max(s_ref[hd, slot], axis=0, keepdims=True)
        else:
            smax = smax_ref[hd, slot]
        m = m_ref[hd]
        m_new = jnp.maximum(m, smax)
        alpha = jnp.exp2(m - m_new)
        part = jnp.zeros((SUBLANES, tq), F32)
        ps = []
        for r in range(ks // ATT_SLAB):
            p = jnp.exp2(s_ref[hd, slot, r * ATT_SLAB:(r + 1) * ATT_SLAB, :] - m_new)
            for r8 in range(ATT_SLAB // SUBLANES):
                part = part + p[r8 * SUBLANES:(r8 + 1) * SUBLANES, :]
            ps.append(p.astype(BF16))
        m_ref[hd] = m_new
        l_ref[hd] = alpha * l_ref[hd] + jnp.sum(part, axis=0, keepdims=True)
        acc_ref[hd] = alpha * acc_ref[hd] + _dot(vt_ref[hd, c], jnp.concatenate(ps, axis=0))

    m_ref[...] = jnp.full_like(m_ref, NEG_BIG)
    l_ref[...] = jnp.zeros_like(l_ref)
    acc_ref[...] = jnp.zeros_like(acc_ref)
    for hd in heads:
        scores(hd, 0, 0)

    def pair(g, _):
        for hd in heads:
            scores(hd, 1, 2 * g + 1)
        for hd in heads:
            update(hd, 0, 2 * g, False)
        for hd in heads:
            scores(hd, 0, 2 * g + 2)
        for hd in heads:
            update(hd, 1, 2 * g + 1, False)
        return 0

    assert tq == 2 * ks
    lax.fori_loop(0, i, pair, 0)
    for hd in heads:
        scores(hd, 1, last)
    for hd in heads:
        update(hd, 0, last - 1, True)
    for hd in heads:
        update(hd, 1, last, True)

    for hd in heads:
        o_ref[:, hd * V_HEAD:(hd + 1) * V_HEAD] = (acc_ref[hd] / l_ref[hd]).T.astype(o_ref.dtype)


def _flash(q, k, vt, S):
    T = q.shape[0]
    B = T // S
    H = MLA_HEADS
    hp = ATT_HEADS
    tq = min(ATT_TILE, S)
    ks = ATT_KEYS
    nq = S // tq
    dq = QK_NOPE + LANES
    return pl.pallas_call(
        functools.partial(_flash_body, tq=tq, ks=ks, hp=hp),
        out_shape=jax.ShapeDtypeStruct((T, H * V_HEAD), BF16),
        grid=(B, H // hp, nq),
        in_specs=[pl.BlockSpec((tq, hp * dq), lambda b, h, i: (b * nq + i, h)),
                  pl.BlockSpec((S, hp * dq), lambda b, h, i: (b, h)),
                  pl.BlockSpec((hp, S // ks, V_HEAD, ks), lambda b, h, i: (h, b, 0, 0))],
        out_specs=pl.BlockSpec((tq, hp * V_HEAD), lambda b, h, i: (b * nq + i, h)),
        scratch_shapes=[pltpu.VMEM((hp, 2, ks, tq), F32), pltpu.VMEM((hp, 2, 1, tq), F32),
                        pltpu.VMEM((hp, 1, tq), F32), pltpu.VMEM((hp, 1, tq), F32),
                        pltpu.VMEM((hp, V_HEAD, tq), F32)],
        compiler_params=_cparams("parallel", "parallel", "arbitrary"),
        name="flash_attn",
    )(q, k, vt)


def _rope_swap(w):
    half = w.shape[-1] // 2
    return jnp.concatenate([w[..., half:], w[..., :half]], axis=-1)


def _pad_lanes(w):
    return jnp.pad(w, [(0, 0)] * (w.ndim - 1) + [(0, LANES - w.shape[-1])])


def kernel(x, c, positions, w_mod, b_mod, norm_mix, norm_ffn, ret_w_in, ret_w_out, w_mod_kv, b_mod_kv, norm_kv, mla_w_kv_a, mla_kv_norm, mla_w_kv_b, mla_w_q_a, mla_q_norm, mla_w_q_b, mla_w_o, router_w, router_b, moe_w_gate, moe_w_up, moe_w_down, final_norm):
    B, S, D = x.shape
    T = B * S
    depth = w_mod.shape[0]
    n_a = ret_w_in.shape[0]

    c8 = jnp.pad(c, ((0, 8 - B), (0, 0)))
    mod = _mod_vectors(c8, w_mod, b_mod)[:, :B]
    mod = mod.reshape(depth, B, 6, 1, D)
    kv_mod = _mod_vectors(c8, w_mod_kv[None], b_mod_kv[None])[0, :B].reshape(B, 2, 1, D)

    pos_col = positions.reshape(T, 1)
    ones = jnp.ones((1, LANES), F32)
    inv_ret = (ROPE_THETA ** (-jnp.arange(LANES, dtype=F32) / LANES)).reshape(1, LANES)
    cos_r, sin_r = _rope_tables(pos_col, inv_ret, ones, ones)
    hr = QK_ROPE // 2
    inv_m = ROPE_THETA ** (-jnp.arange(hr, dtype=F32) / hr)
    inv_m = _pad_lanes(jnp.concatenate([inv_m, inv_m])[None])
    cm = _pad_lanes(jnp.ones((1, QK_ROPE), F32))
    sm = _pad_lanes(jnp.concatenate([-jnp.ones((1, hr), F32), jnp.ones((1, hr), F32)], axis=-1))
    cos_m, sin_m = _rope_tables(pos_col, inv_m, cm, sm)

    wr_hi = router_w.T.astype(BF16)
    wr_t = jnp.concatenate([wr_hi, (router_w.T - wr_hi.astype(F32)).astype(BF16)], axis=0)
    br = router_b.reshape(N_EXPERTS, 1)

    h = x.reshape(T, D)
    lay = _MoeLayout(T, min(TOK_TILE, S))
    k_full = v_full = None
    for layer in range(depth):
        sh1, sc1, g1, sh2, sc2, g2 = (mod[layer, :, i] for i in range(6))
        gmix = norm_mix[layer].reshape(1, D)
        if layer < n_a:
            proj = _ret_inproj(h, gmix, sh1, sc1, ret_w_in[layer].astype(BF16), cos_r, sin_r, S)
            mix = _retention(proj, S, D)
            w_o = ret_w_out[layer].astype(BF16)
        else:
            if layer == n_a:
                wa = mla_w_kv_a
                wa_r = wa[:, KV_LORA:]
                wa_p = jnp.concatenate([wa[:, :KV_LORA], _pad_lanes(wa_r), _pad_lanes(_rope_swap(wa_r))],
                                       axis=-1).astype(BF16)
                wb = mla_w_kv_b.reshape(KV_LORA, MLA_HEADS, QK_NOPE + V_HEAD)
                wb_p = jnp.concatenate([wb[..., :QK_NOPE].reshape(KV_LORA, -1),
                                        wb[..., QK_NOPE:].reshape(KV_LORA, -1)], axis=-1).astype(BF16)
                k_full, v_full = _mla_kv(h, norm_kv.reshape(1, D), kv_mod[:, 0], kv_mod[:, 1], wa_p,
                                         mla_kv_norm.reshape(1, KV_LORA), wb_p, cos_m, sin_m, S)
            j = layer - n_a
            wq = mla_w_q_b[j].reshape(Q_LORA, MLA_HEADS, QK_NOPE + QK_ROPE)
            wq_r = wq[..., QK_NOPE:]
            wq_p = jnp.concatenate([wq[..., :QK_NOPE], _pad_lanes(wq_r), _pad_lanes(_rope_swap(wq_r))],
                                   axis=-1).reshape(Q_LORA, -1).astype(BF16)
            q_full = _mla_q(h, gmix, sh1, sc1, mla_w_q_a[j].astype(BF16),
                            mla_q_norm[j].reshape(1, Q_LORA), wq_p, cos_m, sin_m, S)
            mix = _flash(q_full, k_full, v_full, S)
            w_o = mla_w_o[j].astype(BF16)
        h, xn, rows, cols, tbl, used = _outproj_route(mix, w_o, h, g1, norm_ffn[layer].reshape(1, D),
                                                      sh2, sc2, wr_t, br, S, lay)
        h = _moe(xn, rows, cols, tbl, used[:, 0], moe_w_gate, moe_w_up, moe_w_down, layer,
                 h, g2, S, lay, final_norm.reshape(1, D) if layer == depth - 1 else None)
    return h.reshape(B, S, D)
```

```python
import functools

import jax
import jax.numpy as jnp
from jax import lax
from jax.experimental import pallas as pl
from jax.experimental.pallas import tpu as pltpu

F32 = jnp.float32
BF16 = jnp.bfloat16

CHUNK = 64
RET_HEADS = 4
MLA_HEADS = 8
QK_NOPE = 128
QK_ROPE = 64
V_HEAD = 128
Q_LORA = 256
KV_LORA = 128
N_EXPERTS = 16
N_GROUPS = 4
EXPERTS_PER_GROUP = N_EXPERTS // N_GROUPS
D_EXPERT = 512
ROPE_THETA = 10000.0
EPS = 1e-6

LANES = 128
VMEM_LIMIT = 56 * 1024 * 1024
LOG2E = 1.4426950408889634

RET_CHUNK = 256
TOK_TILE = 512
ATT_TILE = 1024
ATT_KEYS = 512
ATT_HEADS = 2
ATT_SLAB = 32
SUBLANES = 8
MXU_DIM = 256
MOE_GROUP = 16
MOE_ROW_TILE = 1024


def _cparams(*sem):
    return pltpu.CompilerParams(dimension_semantics=sem, vmem_limit_bytes=VMEM_LIMIT)


def _silu(x):
    return x * jax.nn.sigmoid(x)


def _rms(x, g):
    return x * lax.rsqrt(jnp.mean(x * x, axis=-1, keepdims=True) + EPS) * g


def _norm_mod(h, g, shift, scale):
    return _rms(h, g) * (1.0 + scale) + shift


def _dot(a, b):
    return jnp.dot(a, b, preferred_element_type=F32)


def _dot_nt(a, b, **kw):
    return lax.dot_general(a, b, (((1,), (1,)), ((), ())), preferred_element_type=F32, **kw)


def _dot_tn(a, b):
    return lax.dot_general(a, b, (((0,), (0,)), ((), ())), preferred_element_type=F32)


def _mod_body(c_ref, w_ref, b_ref, o_ref):
    ca = _silu(c_ref[...])
    o_ref[0] = jnp.dot(ca, w_ref[0], preferred_element_type=F32,
                       precision=lax.Precision.HIGHEST) + b_ref[0]


def _mod_vectors(c8, w, b):
    L, D, N = w.shape
    tn = D
    assert N % tn == 0
    return pl.pallas_call(
        _mod_body,
        out_shape=jax.ShapeDtypeStruct((L, 8, N), F32),
        grid=(L, N // tn),
        in_specs=[pl.BlockSpec((8, D), lambda l, j: (0, 0)),
                  pl.BlockSpec((1, D, tn), lambda l, j: (l, 0, j)),
                  pl.BlockSpec((1, 1, tn), lambda l, j: (l, 0, j))],
        out_specs=pl.BlockSpec((1, 8, tn), lambda l, j: (l, 0, j)),
        compiler_params=_cparams("parallel", "parallel"),
        name="mod_vectors",
    )(c8, w, b.reshape(L, 1, N))


def _rope_body(pos_ref, inv_ref, cm_ref, sm_ref, cos_ref, sin_ref):
    ang = pos_ref[...].astype(F32) * inv_ref[...]
    cos_ref[...] = jnp.cos(ang) * cm_ref[...]
    sin_ref[...] = jnp.sin(ang) * sm_ref[...]


def _rope_tables(pos_col, inv, cm, sm):
    T = pos_col.shape[0]
    tm = min(T, 1024)
    row = pl.BlockSpec((1, LANES), lambda i: (0, 0))
    return pl.pallas_call(
        _rope_body,
        out_shape=(jax.ShapeDtypeStruct((T, LANES), F32),) * 2,
        grid=(T // tm,),
        in_specs=[pl.BlockSpec((tm, 1), lambda i: (i, 0)), row, row, row],
        out_specs=(pl.BlockSpec((tm, LANES), lambda i: (i, 0)),) * 2,
        compiler_params=_cparams("parallel"),
        name="rope_tables",
    )(pos_col, inv, cm, sm)


def _inproj_body(h_ref, g_ref, sh_ref, sc_ref, w_ref, cos_ref, sin_ref, o_ref, *, tn, dk_dim):
    xn = _norm_mod(h_ref[...], g_ref[...], sh_ref[0], sc_ref[0]).astype(BF16)
    d_model = h_ref.shape[1]
    half = dk_dim // 2
    for j in range(w_ref.shape[1] // tn):
        acc = _dot(xn, w_ref[:, j * tn:(j + 1) * tn])
        if j * tn >= 2 * d_model:
            o_ref[:, j * tn:(j + 1) * tn] = acc.astype(BF16)
            continue
        cos = cos_ref[...]
        sin = sin_ref[...]
        scale = 1.0 if j * tn < d_model else dk_dim ** -0.5
        for hd in range(tn // dk_dim):
            x1 = acc[:, hd * dk_dim:hd * dk_dim + half]
            x2 = acc[:, hd * dk_dim + half:(hd + 1) * dk_dim]
            c0 = j * tn + hd * dk_dim
            o_ref[:, c0:c0 + half] = ((x1 * cos - x2 * sin) * scale).astype(BF16)
            o_ref[:, c0 + half:c0 + dk_dim] = ((x1 * sin + x2 * cos) * scale).astype(BF16)


def _ret_inproj(h, g, sh, sc, w, cos, sin, S):
    T, D = h.shape
    N = w.shape[1]
    tm = min(TOK_TILE, S)
    per_b = S // tm
    vec = pl.BlockSpec((1, 1, D), lambda i: (i // per_b, 0, 0))
    tab = pl.BlockSpec((tm, LANES), lambda i: (i, 0))
    return pl.pallas_call(
        functools.partial(_inproj_body, tn=512, dk_dim=D // RET_HEADS),
        out_shape=jax.ShapeDtypeStruct((T, N), BF16),
        grid=(T // tm,),
        in_specs=[pl.BlockSpec((tm, D), lambda i: (i, 0)),
                  pl.BlockSpec((1, D), lambda i: (0, 0)),
                  vec, vec,
                  pl.BlockSpec((D, N), lambda i: (0, 0)),
                  tab, tab],
        out_specs=pl.BlockSpec((tm, N), lambda i: (i, 0)),
        compiler_params=_cparams("parallel"),
        name="ret_inproj",
    )(h, g, sh, sc, w, cos, sin)


def _ret_body(q_ref, k_ref, v_ref, g_ref, di_ref, dq_ref, dk_ref, dc_ref,
              y_ref, state_ref, *, dk_dim):
    @pl.when(pl.program_id(1) == 0)
    def _():
        state_ref[...] = jnp.zeros_like(state_ref)

    dv_dim = 2 * dk_dim
    for hd in range(RET_HEADS):
        qk = slice(hd * dk_dim, (hd + 1) * dk_dim)
        vg = slice(hd * dv_dim, (hd + 1) * dv_dim)
        qb = q_ref[:, qk]
        kb = k_ref[:, qk]
        v = v_ref[:, vg]
        inner = (_dot_nt(qb, kb) * di_ref[hd]).astype(BF16)
        st = state_ref[hd]
        out = _dot(inner, v) + _dot(qb, st.astype(BF16)) * dq_ref[hd]
        kd = (kb.astype(F32) * dk_ref[hd]).astype(BF16)
        state_ref[hd] = st * dc_ref[hd] + _dot_tn(kd, v)

        mu = jnp.mean(out, axis=-1, keepdims=True)
        cen = out - mu
        var = jnp.mean(cen * cen, axis=-1, keepdims=True)
        o = cen * lax.rsqrt(var + EPS)
        y_ref[:, vg] = (_silu(g_ref[:, vg].astype(F32)) * o).astype(BF16)


def _retention(proj, S, D):
    T = proj.shape[0]
    B = T // S
    H = RET_HEADS
    dk = D // H
    dv = 2 * dk
    C = min(RET_CHUNK, S)
    n = S // C
    log_g = jnp.log1p(-(2.0 ** (-5.0 - jnp.arange(H, dtype=F32))))
    t = jnp.arange(C, dtype=F32)
    diff = t[:, None] - t[None, :]
    d_intra = jnp.where(diff >= 0, jnp.exp(log_g[:, None, None] * jnp.maximum(diff, 0.0)), 0.0)
    d_q = jnp.exp(log_g[:, None] * (t + 1.0))[:, :, None]
    d_k = jnp.exp(log_g[:, None] * (C - 1.0 - t))[:, :, None]
    d_c = jnp.exp(log_g * C)[:, None, None]

    assert 2 * D == H * dv
    row = lambda b, i: b * n + i
    const = lambda a: pl.BlockSpec(a.shape, lambda b, i: (0, 0, 0))
    return pl.pallas_call(
        functools.partial(_ret_body, dk_dim=dk),
        out_shape=jax.ShapeDtypeStruct((T, H * dv), BF16),
        grid=(B, n),
        in_specs=[pl.BlockSpec((C, D), lambda b, i: (row(b, i), 0)),
                  pl.BlockSpec((C, D), lambda b, i: (row(b, i), 1)),
                  pl.BlockSpec((C, H * dv), lambda b, i: (row(b, i), 1)),
                  pl.BlockSpec((C, H * dv), lambda b, i: (row(b, i), 2)),
                  const(d_intra), const(d_q), const(d_k), const(d_c)],
        out_specs=pl.BlockSpec((C, H * dv), lambda b, i: (row(b, i), 0)),
        scratch_shapes=[pltpu.VMEM((H, dk, dv), F32)],
        compiler_params=_cparams("parallel", "arbitrary"),
        name="retention",
    )(proj, proj, proj, proj, d_intra, d_q, d_k, d_c)


def _route(logits_t, bias):
    sc = jax.nn.sigmoid(logits_t)
    bi = sc + bias
    s_rows = [sc[e:e + 1, :] for e in range(N_EXPERTS)]
    b_rows = [bi[e:e + 1, :] for e in range(N_EXPERTS)]

    def top2sum(a, b, c, d):
        p, q = jnp.maximum(a, b), jnp.minimum(a, b)
        r, s = jnp.maximum(c, d), jnp.minimum(c, d)
        return jnp.maximum(p, r) + jnp.maximum(jnp.minimum(p, r), jnp.maximum(q, s))

    n = EXPERTS_PER_GROUP
    gs = [top2sum(*b_rows[n * g:n * g + n]) for g in range(N_GROUPS)]
    best, gi = gs[0], jnp.zeros_like(gs[0], dtype=jnp.int32)
    for g in range(1, N_GROUPS):
        upd = gs[g] > best
        gi = jnp.where(upd, g, gi)
        best = jnp.where(upd, gs[g], best)

    def pick(rows, j):
        out = rows[j]
        for g in range(1, N_GROUPS):
            out = jnp.where(gi == g, rows[n * g + j], out)
        return out

    vb = [pick(b_rows, j) for j in range(n)]
    vs = [pick(s_rows, j) for j in range(n)]

    def argmax_first(vals):
        best, idx = vals[0], jnp.zeros_like(gi)
        for j in range(1, n):
            upd = vals[j] > best
            idx = jnp.where(upd, j, idx)
            best = jnp.where(upd, vals[j], best)
        return idx

    i1 = argmax_first(vb)
    i2 = argmax_first([jnp.where(i1 == j, -jnp.inf, vb[j]) for j in range(n)])

    def take(vals, idx):
        out = vals[0]
        for j in range(1, n):
            out = jnp.where(idx == j, vals[j], out)
        return out

    w1, w2 = take(vs, i1), take(vs, i2)
    tot = w1 + w2
    w1, w2 = w1 / tot, w2 / tot
    return gi * n + i1, gi * n + i2, w1, w2


class _MoeLayout:
    def __init__(self, T, tm):
        self.tm = tm
        self.n_tiles = T // tm
        self.group = MOE_GROUP
        self.rt = -(-(2 * tm + N_EXPERTS * (MOE_GROUP - 1)) // MXU_DIM) * MXU_DIM
        self.ng = self.rt // MOE_GROUP
        assert self.ng < LANES
        self.tmx = MOE_ROW_TILE
        self.cap = -(-(T + self.n_tiles * MOE_GROUP) // self.tmx) * self.tmx
        self.dump = N_EXPERTS * self.cap
        self.rows = self.dump + self.rt
        pad = self.n_tiles * N_EXPERTS * (MOE_GROUP - 1)
        self.nt = (2 * T + pad) // self.tmx + N_EXPERTS


def _dispatch_meta(e1, e2, cum, tri, lay):
    E, G = N_EXPERTS, lay.group
    tm = e1.shape[1]
    eid = lax.broadcasted_iota(jnp.int32, (E, tm), 0)
    oh1, oh2 = eid == e1, eid == e2
    cnt = jnp.where(oh1 | oh2, 1.0, 0.0)
    pre = _dot(cnt.astype(BF16), tri)
    tot = jnp.sum(cnt, axis=1, keepdims=True)
    ptot = jnp.broadcast_to(jnp.ceil(tot * (1.0 / G)) * G, (E, LANES))
    below = jnp.where(lax.broadcasted_iota(jnp.int32, (E, E), 0) > lax.broadcasted_iota(jnp.int32, (E, E), 1),
                      1.0, 0.0)
    loff = jnp.dot(below, ptot, preferred_element_type=F32, precision=lax.Precision.HIGHEST)
    pos_e = loff[:, :1] + pre
    pos1 = jnp.sum(jnp.where(oh1, pos_e, 0.0), axis=0, keepdims=True)
    pos2 = jnp.sum(jnp.where(oh2, pos_e, 0.0), axis=0, keepdims=True)

    lane = lax.broadcasted_iota(jnp.int32, (E, LANES), 1)
    g_row = (lane * G).astype(F32)
    eg = jnp.sum(jnp.where(loff + ptot <= g_row, 1, 0), axis=0, keepdims=True)
    erow = lax.broadcasted_iota(jnp.int32, (E, LANES), 0)
    base = erow.astype(F32) * float(lay.cap) + cum - loff
    sel = jnp.sum(jnp.where(erow == eg, base, 0.0), axis=0, keepdims=True)
    dst = jnp.where(eg < E, g_row[:1] + sel, float(lay.dump) + g_row[:1])
    n_used = jnp.sum(ptot[:, :1], axis=0, keepdims=True) * (1.0 / G)
    table = jnp.where(lane[:1] == LANES - 1, n_used, dst).astype(jnp.int32)
    return pos1, pos2, table, cum + ptot


def _outproj_body(y_ref, w_ref, h_ref, g1_ref, gn_ref, sh_ref, sc_ref, wr_ref, br_ref, tri_ref,
                  ho_ref, xn_ref, rows_ref, cols_ref, tbl_ref, cum_ref, *, lay):
    @pl.when(pl.program_id(0) == 0)
    def _():
        cum_ref[...] = jnp.zeros_like(cum_ref)

    hn = h_ref[...] + g1_ref[0] * _dot(y_ref[...], w_ref[...])
    ho_ref[...] = hn
    xn = _norm_mod(hn, gn_ref[...], sh_ref[0], sc_ref[0])
    xb = xn.astype(BF16)
    xn_ref[...] = xb
    xl = (xn - xb.astype(F32)).astype(BF16)
    hl = _dot_nt(wr_ref[...], xb)
    logits_t = hl[:N_EXPERTS] + hl[N_EXPERTS:] + _dot_nt(wr_ref[:N_EXPERTS, :], xl)
    e1, e2, w1, w2 = _route(logits_t, br_ref[...])
    pos1, pos2, table, cum = _dispatch_meta(e1, e2, cum_ref[...], tri_ref[...], lay)
    cum_ref[...] = cum
    tbl_ref[0] = table
    tm = e1.shape[1]
    rows = jnp.concatenate([pos1, pos2, w1, w2, jnp.zeros((SUBLANES - 4, tm), F32)], axis=0)
    rows_ref[...] = rows
    cols_ref[...] = jnp.concatenate([rows, jnp.zeros((LANES - SUBLANES, tm), F32)], axis=0).T


def _outproj_route(y, w, h, g1, gn, sh, sc, wr_t, br, S, lay):
    T, D = h.shape
    K = y.shape[1]
    tm = lay.tm
    per_b = S // tm
    vec = pl.BlockSpec((1, 1, D), lambda i: (i // per_b, 0, 0))
    tok = lambda n: pl.BlockSpec((tm, n), lambda i: (i, 0))
    tri = jnp.triu(jnp.ones((tm, tm), BF16), k=1)
    return pl.pallas_call(
        functools.partial(_outproj_body, lay=lay),
        out_shape=(jax.ShapeDtypeStruct((T, D), F32),
                   jax.ShapeDtypeStruct((T, D), BF16),
                   jax.ShapeDtypeStruct((SUBLANES, T), F32),
                   jax.ShapeDtypeStruct((T, LANES), F32),
                   jax.ShapeDtypeStruct((lay.n_tiles, 1, LANES), jnp.int32),
                   jax.ShapeDtypeStruct((N_EXPERTS, LANES), F32)),
        grid=(T // tm,),
        in_specs=[tok(K),
                  pl.BlockSpec((K, D), lambda i: (0, 0)),
                  tok(D), vec,
                  pl.BlockSpec((1, D), lambda i: (0, 0)),
                  vec, vec,
                  pl.BlockSpec((2 * N_EXPERTS, D), lambda i: (0, 0)),
                  pl.BlockSpec((N_EXPERTS, 1), lambda i: (0, 0)),
                  pl.BlockSpec((tm, tm), lambda i: (0, 0))],
        out_specs=(tok(D), tok(D),
                   pl.BlockSpec((SUBLANES, tm), lambda i: (0, i)),
                   tok(LANES),
                   pl.BlockSpec((1, 1, LANES), lambda i: (i, 0, 0)),
                   pl.BlockSpec((N_EXPERTS, LANES), lambda i: (0, 0))),
        compiler_params=_cparams("arbitrary"),
        name="outproj_route",
    )(y, w, h, g1, gn, sh, sc, wr_t, br, tri)


def _group_copy(hbm_ref, buf_ref, tbl_ref, i, g, sem, lay, to_hbm):
    G = lay.group
    hbm = hbm_ref.at[pl.ds(pl.multiple_of(tbl_ref[i * LANES + g], G), G), :]
    vmem = buf_ref.at[pl.ds(pl.multiple_of(g * G, G), G), :]
    return pltpu.make_async_copy(vmem, hbm, sem) if to_hbm else pltpu.make_async_copy(hbm, vmem, sem)


def _dispatch_body(tbl_ref, x_ref, rows_ref, xs_ref, buf_ref, sem, *, lay):
    i = pl.program_id(0)
    pos1 = rows_ref[0:1, :].astype(jnp.int32)
    pos2 = rows_ref[1:2, :].astype(jnp.int32)
    r = lax.broadcasted_iota(jnp.int32, (lay.rt, lay.tm), 0)
    perm = jnp.where((r == pos1) | (r == pos2), 1.0, 0.0).astype(BF16)
    slot = i % 2
    buf_ref[slot] = _dot(perm, x_ref[...]).astype(BF16)

    def copies(tile, slot_):
        return [_group_copy(xs_ref, buf_ref.at[slot_], tbl_ref, tile, g, sem.at[slot_], lay, True)
                for g in range(lay.ng)]

    @pl.when(i > 0)
    def _():
        for c in copies(i - 1, 1 - slot):
            c.wait()

    for c in copies(i, slot):
        c.start()

    @pl.when(i == pl.num_programs(0) - 1)
    def _():
        for c in copies(i, slot):
            c.wait()


def _moe_dispatch(xn, rows, tbl, lay):
    T, D = xn.shape
    tm = lay.tm
    grid_spec = pltpu.PrefetchScalarGridSpec(
        num_scalar_prefetch=1,
        grid=(lay.n_tiles,),
        in_specs=[pl.BlockSpec((tm, D), lambda i, tbl: (i, 0)),
                  pl.BlockSpec((SUBLANES, tm), lambda i, tbl: (0, i))],
        out_specs=pl.BlockSpec(memory_space=pl.ANY),
        scratch_shapes=[pltpu.VMEM((2, lay.rt, D), BF16), pltpu.SemaphoreType.DMA((2,))],
    )
    return pl.pallas_call(
        functools.partial(_dispatch_body, lay=lay),
        out_shape=jax.ShapeDtypeStruct((lay.rows, D), BF16),
        grid_spec=grid_spec,
        compiler_params=_cparams("arbitrary"),
        name="moe_dispatch",
    )(tbl, xn, rows)


def _expert_body(te_ref, tb_ref, nv_ref, x_ref, wg_ref, wu_ref, wd_ref, y_ref, wgu_bf, wd_bf):
    n = pl.program_id(0)
    f = wg_ref.shape[2]

    @pl.when((n == 0) | (te_ref[n] != te_ref[jnp.maximum(n - 1, 0)]))
    def _():
        wgu_bf[:, :f] = wg_ref[0].astype(BF16)
        wgu_bf[:, f:] = wu_ref[0].astype(BF16)
        wd_bf[...] = wd_ref[0].astype(BF16)

    @pl.when(n < nv_ref[0])
    def _():
        hgu = _dot(x_ref[...], wgu_bf[...])
        hdn = (_silu(hgu[:, :f]) * hgu[:, f:]).astype(BF16)
        y_ref[...] = _dot(hdn, wd_bf[...]).astype(BF16)


def _moe_experts(xs, wg, wu, wd, layer, tile_e, tile_blk, n_valid, lay):
    R, D = xs.shape
    F = wg.shape[3]
    tmx = lay.tmx
    wspec = lambda shape: pl.BlockSpec((None, 1) + shape, lambda n, te, tb, nv: (layer, te[n], 0, 0))
    grid_spec = pltpu.PrefetchScalarGridSpec(
        num_scalar_prefetch=3,
        grid=(lay.nt,),
        in_specs=[pl.BlockSpec((tmx, D), lambda n, te, tb, nv: (tb[n], 0)),
                  wspec((D, F)), wspec((D, F)), wspec((F, D))],
        out_specs=pl.BlockSpec((tmx, D), lambda n, te, tb, nv: (tb[n], 0)),
        scratch_shapes=[pltpu.VMEM((D, 2 * F), BF16), pltpu.VMEM((F, D), BF16)],
    )
    return pl.pallas_call(
        _expert_body,
        out_shape=jax.ShapeDtypeStruct((R, D), BF16),
        grid_spec=grid_spec,
        compiler_params=_cparams("arbitrary"),
        name="moe_experts",
    )(tile_e, tile_blk, n_valid, xs, wg, wu, wd)


def _combine_body(tbl_ref, cols_ref, h_ref, g2_ref, *rest, lay, final):
    if final:
        gfin_ref, ys_ref, o_ref, buf_ref, sem = rest
    else:
        ys_ref, o_ref, buf_ref, sem = rest
    i = pl.program_id(0)
    slot = i % 2

    def fetch(tile, slot_, start):
        def one(g, _):
            c = _group_copy(ys_ref, buf_ref.at[slot_], tbl_ref, tile, g, sem.at[slot_], lay, False)
            c.start() if start else c.wait()
            return 0
        lax.fori_loop(0, tbl_ref[tile * LANES + LANES - 1], one, 0)

    @pl.when(i == 0)
    def _():
        buf_ref[...] = jnp.zeros_like(buf_ref)
        fetch(0, 0, True)

    @pl.when(i + 1 < pl.num_programs(0))
    def _():
        fetch(i + 1, 1 - slot, True)

    fetch(i, slot, False)

    cols = cols_ref[...]
    r = lax.broadcasted_iota(jnp.int32, (lay.tm, lay.rt), 1)
    mix = (jnp.where(r == cols[:, 0:1].astype(jnp.int32), cols[:, 2:3], 0.0)
           + jnp.where(r == cols[:, 1:2].astype(jnp.int32), cols[:, 3:4], 0.0)).astype(BF16)
    out = h_ref[...] + g2_ref[0] * _dot(mix, buf_ref[slot])
    o_ref[...] = _rms(out, gfin_ref[...]) if final else out


def _moe_combine(ys, cols, tbl, h, g2, S, lay, final_gain=None):
    T, D = h.shape
    tm = lay.tm
    per_b = S // tm
    final = final_gain is not None
    in_specs = [pl.BlockSpec((tm, LANES), lambda i, tbl: (i, 0)),
                pl.BlockSpec((tm, D), lambda i, tbl: (i, 0)),
                pl.BlockSpec((1, 1, D), lambda i, tbl: (i // per_b, 0, 0))]
    args = [cols, h, g2]
    if final:
        in_specs.append(pl.BlockSpec((1, D), lambda i, tbl: (0, 0)))
        args.append(final_gain)
    grid_spec = pltpu.PrefetchScalarGridSpec(
        num_scalar_prefetch=1,
        grid=(lay.n_tiles,),
        in_specs=in_specs + [pl.BlockSpec(memory_space=pl.ANY)],
        out_specs=pl.BlockSpec((tm, D), lambda i, tbl: (i, 0)),
        scratch_shapes=[pltpu.VMEM((2, lay.rt, D), BF16), pltpu.SemaphoreType.DMA((2,))],
    )
    return pl.pallas_call(
        functools.partial(_combine_body, lay=lay, final=final),
        out_shape=jax.ShapeDtypeStruct((T, D), F32),
        grid_spec=grid_spec,
        compiler_params=_cparams("arbitrary"),
        name="moe_combine",
    )(tbl, *args, ys)


def _expert_tiles(used, lay):
    tiles = jnp.ceil(used / lay.tmx).astype(jnp.int32)
    ends = jnp.cumsum(tiles)
    n_valid = ends[-1]
    n = jnp.minimum(jnp.arange(lay.nt, dtype=jnp.int32), n_valid - 1)
    e = jnp.sum((ends[None, :] <= n[:, None]).astype(jnp.int32), axis=1)
    blk = e * (lay.cap // lay.tmx) + n - (ends - tiles)[e]
    return e, blk, n_valid.reshape(1)


def _moe(xn, rows, cols, tbl, used, wg, wu, wd, layer, h, g2, S, lay, final_gain=None):
    tbl = tbl.reshape(-1)
    xs = _moe_dispatch(xn, rows, tbl, lay)
    ys = _moe_experts(xs, wg, wu, wd, layer, *_expert_tiles(used, lay), lay)
    return _moe_combine(ys, cols, tbl, h, g2, S, lay, final_gain)


def _kv_body(h_ref, g_ref, sh_ref, sc_ref, wa_ref, gkv_ref, wb_ref, cos_ref, sin_ref,
             k_ref, vt_ref):
    hn = _norm_mod(h_ref[...], g_ref[...], sh_ref[0], sc_ref[0]).astype(BF16)
    a = _dot(hn, wa_ref[...])
    c_kv = _rms(a[:, :KV_LORA], gkv_ref[...]).astype(BF16)
    kr = (a[:, KV_LORA:KV_LORA + LANES] * cos_ref[...]
          + a[:, KV_LORA + LANES:] * sin_ref[...]).astype(BF16)
    kv = _dot(c_kv, wb_ref[...])
    hk = MLA_HEADS * QK_NOPE
    w = QK_NOPE + LANES
    for hd in range(MLA_HEADS):
        k_ref[:, hd * w:hd * w + QK_NOPE] = kv[:, hd * QK_NOPE:(hd + 1) * QK_NOPE].astype(BF16)
        k_ref[:, hd * w + QK_NOPE:(hd + 1) * w] = kr
        vh = kv[:, hk + hd * V_HEAD:hk + (hd + 1) * V_HEAD]
        for g in range(vt_ref.shape[1]):
            vt_ref[hd, g] = vh[g * ATT_KEYS:(g + 1) * ATT_KEYS, :].T.astype(BF16)


def _mla_kv(h, g, sh, sc, wa, gkv, wb, cos, sin, S):
    T, D = h.shape
    tm = min(TOK_TILE, S)
    per_b = S // tm
    vec = pl.BlockSpec((1, 1, D), lambda i: (i // per_b, 0, 0))
    tok = lambda n: pl.BlockSpec((tm, n), lambda i: (i, 0))
    full = lambda a: pl.BlockSpec(a.shape, lambda i: (0, 0))
    kw = MLA_HEADS * (QK_NOPE + LANES)
    gk = tm // ATT_KEYS
    vt_shape = (MLA_HEADS, T // ATT_KEYS, V_HEAD, ATT_KEYS)
    return pl.pallas_call(
        _kv_body,
        out_shape=(jax.ShapeDtypeStruct((T, kw), BF16), jax.ShapeDtypeStruct(vt_shape, BF16)),
        grid=(T // tm,),
        in_specs=[tok(D), full(g), vec, vec, full(wa), full(gkv), full(wb), tok(LANES), tok(LANES)],
        out_specs=(tok(kw),
                   pl.BlockSpec((MLA_HEADS, gk, V_HEAD, ATT_KEYS), lambda i: (0, i, 0, 0))),
        compiler_params=_cparams("parallel"),
        name="mla_kv",
    )(h, g, sh, sc, wa, gkv, wb, cos, sin)


def _q_body(h_ref, g_ref, sh_ref, sc_ref, wa_ref, gq_ref, wb_ref, cos_ref, sin_ref, q_ref, *, scale):
    xn = _norm_mod(h_ref[...], g_ref[...], sh_ref[0], sc_ref[0]).astype(BF16)
    qa = _rms(_dot(xn, wa_ref[...]), gq_ref[...] * scale).astype(BF16)
    cos = cos_ref[...]
    sin = sin_ref[...]
    wi = QK_NOPE + 2 * LANES
    wo = QK_NOPE + LANES
    for hd in range(MLA_HEADS):
        qb = _dot(qa, wb_ref[:, hd * wi:(hd + 1) * wi])
        q_ref[:, hd * wo:hd * wo + QK_NOPE] = qb[:, :QK_NOPE].astype(BF16)
        rp = qb[:, QK_NOPE:QK_NOPE + LANES] * cos + qb[:, QK_NOPE + LANES:] * sin
        q_ref[:, hd * wo + QK_NOPE:(hd + 1) * wo] = rp.astype(BF16)


def _mla_q(h, g, sh, sc, wa, gq, wb, cos, sin, S):
    T, D = h.shape
    tm = min(TOK_TILE, S)
    per_b = S // tm
    vec = pl.BlockSpec((1, 1, D), lambda i: (i // per_b, 0, 0))
    tok = lambda n: pl.BlockSpec((tm, n), lambda i: (i, 0))
    full = lambda a: pl.BlockSpec(a.shape, lambda i: (0, 0))
    qw = MLA_HEADS * (QK_NOPE + LANES)
    return pl.pallas_call(
        functools.partial(_q_body, scale=(QK_NOPE + QK_ROPE) ** -0.5 * LOG2E),
        out_shape=jax.ShapeDtypeStruct((T, qw), BF16),
        grid=(T // tm,),
        in_specs=[tok(D), full(g), vec, vec, full(wa), full(gq), full(wb), tok(LANES), tok(LANES)],
        out_specs=tok(qw),
        compiler_params=_cparams("parallel"),
        name="mla_q",
    )(h, g, sh, sc, wa, gq, wb, cos, sin)


def _flash_body(q_ref, k_ref, vt_ref, bias_ref, o_ref, s_ref, smax_ref, m_ref, l_ref, acc_ref,
                *, tq, ks, hp):
    i = pl.program_id(2)
    last = (i * tq + tq - 1) // ks
    dq = QK_NOPE + LANES
    heads = range(hp)

    def scores(hd, slot, c, lo=0):
        kc = k_ref[pl.ds(pl.multiple_of(c * ks, ks), ks), hd * dq:(hd + 1) * dq]
        s = _dot_nt(kc, q_ref[lo:, hd * dq:(hd + 1) * dq])
        s_ref[hd, slot, :, lo:] = s
        smax_ref[hd, slot, :, lo:] = jnp.max(s, axis=0, keepdims=True)

    def update(hd, slot, c, lo=0, diag=None):
        if diag is None:
            smax = smax_ref[hd, slot, :, lo:]
        else:
            s_ref[hd, slot, :, diag:diag + ks] += bias_ref[...]
            smax = jnp.max(s_ref[hd, slot, :, lo:], axis=0, keepdims=True)
        m = m_ref[hd, :, lo:]
        m_new = jnp.maximum(m, smax)
        alpha = jnp.exp2(m - m_new)
        part = jnp.zeros((SUBLANES, tq - lo), F32)
        ps = []
        for r in range(ks // ATT_SLAB):
            p = jnp.exp2(s_ref[hd, slot, r * ATT_SLAB:(r + 1) * ATT_SLAB, lo:] - m_new)
            for r8 in range(ATT_SLAB // SUBLANES):
                part = part + p[r8 * SUBLANES:(r8 + 1) * SUBLANES, :]
            ps.append(p.astype(BF16))
        m_ref[hd, :, lo:] = m_new
        l_ref[hd, :, lo:] = alpha * l_ref[hd, :, lo:] + jnp.sum(part, axis=0, keepdims=True)
        acc_ref[hd, :, lo:] = (alpha * acc_ref[hd, :, lo:]
                               + _dot(vt_ref[hd, c], jnp.concatenate(ps, axis=0)))

    m_ref[...] = jnp.full_like(m_ref, -jnp.inf)
    l_ref[...] = jnp.zeros_like(l_ref)
    acc_ref[...] = jnp.zeros_like(acc_ref)
    for hd in heads:
        scores(hd, 0, 0)

    def pair(g, _):
        for hd in heads:
            scores(hd, 1, 2 * g + 1)
        for hd in heads:
            update(hd, 0, 2 * g)
        for hd in heads:
            scores(hd, 0, 2 * g + 2)
        for hd in heads:
            update(hd, 1, 2 * g + 1)
        return 0

    assert tq == 2 * ks
    lax.fori_loop(0, i, pair, 0)
    for hd in heads:
        scores(hd, 1, last, lo=ks)
    for hd in heads:
        update(hd, 0, last - 1, diag=0)
    for hd in heads:
        update(hd, 1, last, lo=ks, diag=ks)

    for hd in heads:
        o_ref[:, hd * V_HEAD:(hd + 1) * V_HEAD] = (acc_ref[hd] / l_ref[hd]).T.astype(o_ref.dtype)


def _flash(q, k, vt, S):
    T = q.shape[0]
    B = T // S
    H = MLA_HEADS
    hp = ATT_HEADS
    tq = min(ATT_TILE, S)
    ks = ATT_KEYS
    nq = S // tq
    dq = QK_NOPE + LANES
    chunk = jnp.arange(ks, dtype=jnp.int32) // CHUNK
    bias = jnp.where(chunk[:, None] <= chunk[None, :], 0.0, -jnp.inf).astype(F32)
    return pl.pallas_call(
        functools.partial(_flash_body, tq=tq, ks=ks, hp=hp),
        out_shape=jax.ShapeDtypeStruct((T, H * V_HEAD), BF16),
        grid=(B, H // hp, nq),
        in_specs=[pl.BlockSpec((tq, hp * dq), lambda b, h, i: (b * nq + i, h)),
                  pl.BlockSpec((S, hp * dq), lambda b, h, i: (b, h)),
                  pl.BlockSpec((hp, S // ks, V_HEAD, ks), lambda b, h, i: (h, b, 0, 0)),
                  pl.BlockSpec((ks, ks), lambda b, h, i: (0, 0))],
        out_specs=pl.BlockSpec((tq, hp * V_HEAD), lambda b, h, i: (b * nq + i, h)),
        scratch_shapes=[pltpu.VMEM((hp, 2, ks, tq), F32), pltpu.VMEM((hp, 2, 1, tq), F32),
                        pltpu.VMEM((hp, 1, tq), F32), pltpu.VMEM((hp, 1, tq), F32),
                        pltpu.VMEM((hp, V_HEAD, tq), F32)],
        compiler_params=_cparams("parallel", "parallel", "arbitrary"),
        name="flash_attn",
    )(q, k, vt, bias)


def _rope_swap(w):
    half = w.shape[-1] // 2
    return jnp.concatenate([w[..., half:], w[..., :half]], axis=-1)


def _pad_lanes(w):
    return jnp.pad(w, [(0, 0)] * (w.ndim - 1) + [(0, LANES - w.shape[-1])])


def kernel(x, c, positions, w_mod, b_mod, norm_mix, norm_ffn, ret_w_in, ret_w_out, w_mod_kv, b_mod_kv, norm_kv, mla_w_kv_a, mla_kv_norm, mla_w_kv_b, mla_w_q_a, mla_q_norm, mla_w_q_b, mla_w_o, router_w, router_b, moe_w_gate, moe_w_up, moe_w_down, final_norm):
    B, S, D = x.shape
    T = B * S
    depth = w_mod.shape[0]
    n_a = ret_w_in.shape[0]

    c8 = jnp.pad(c, ((0, 8 - B), (0, 0)))
    mod = _mod_vectors(c8, w_mod, b_mod)[:, :B]
    mod = mod.reshape(depth, B, 6, 1, D)
    kv_mod = _mod_vectors(c8, w_mod_kv[None], b_mod_kv[None])[0, :B].reshape(B, 2, 1, D)

    pos_col = positions.reshape(T, 1)
    ones = jnp.ones((1, LANES), F32)
    inv_ret = (ROPE_THETA ** (-jnp.arange(LANES, dtype=F32) / LANES)).reshape(1, LANES)
    cos_r, sin_r = _rope_tables(pos_col, inv_ret, ones, ones)
    hr = QK_ROPE // 2
    inv_m = ROPE_THETA ** (-jnp.arange(hr, dtype=F32) / hr)
    inv_m = _pad_lanes(jnp.concatenate([inv_m, inv_m])[None])
    cm = _pad_lanes(jnp.ones((1, QK_ROPE), F32))
    sm = _pad_lanes(jnp.concatenate([-jnp.ones((1, hr), F32), jnp.ones((1, hr), F32)], axis=-1))
    cos_m, sin_m = _rope_tables(pos_col, inv_m, cm, sm)

    wr_hi = router_w.T.astype(BF16)
    wr_t = jnp.concatenate([wr_hi, (router_w.T - wr_hi.astype(F32)).astype(BF16)], axis=0)
    br = router_b.reshape(N_EXPERTS, 1)

    h = x.reshape(T, D)
    lay = _MoeLayout(T, min(TOK_TILE, S))
    k_full = v_full = None
    for layer in range(depth):
        sh1, sc1, g1, sh2, sc2, g2 = (mod[layer, :, i] for i in range(6))
        gmix = norm_mix[layer].reshape(1, D)
        if layer < n_a:
            proj = _ret_inproj(h, gmix, sh1, sc1, ret_w_in[layer].astype(BF16), cos_r, sin_r, S)
            mix = _retention(proj, S, D)
            w_o = ret_w_out[layer].astype(BF16)
        else:
            if layer == n_a:
                wa = mla_w_kv_a
                wa_r = wa[:, KV_LORA:]
                wa_p = jnp.concatenate([wa[:, :KV_LORA], _pad_lanes(wa_r), _pad_lanes(_rope_swap(wa_r))],
                                       axis=-1).astype(BF16)
                wb = mla_w_kv_b.reshape(KV_LORA, MLA_HEADS, QK_NOPE + V_HEAD)
                wb_p = jnp.concatenate([wb[..., :QK_NOPE].reshape(KV_LORA, -1),
                                        wb[..., QK_NOPE:].reshape(KV_LORA, -1)], axis=-1).astype(BF16)
                k_full, v_full = _mla_kv(h, norm_kv.reshape(1, D), kv_mod[:, 0], kv_mod[:, 1], wa_p,
                                         mla_kv_norm.reshape(1, KV_LORA), wb_p, cos_m, sin_m, S)
            j = layer - n_a
            wq = mla_w_q_b[j].reshape(Q_LORA, MLA_HEADS, QK_NOPE + QK_ROPE)
            wq_r = wq[..., QK_NOPE:]
            wq_p = jnp.concatenate([wq[..., :QK_NOPE], _pad_lanes(wq_r), _pad_lanes(_rope_swap(wq_r))],
                                   axis=-1).reshape(Q_LORA, -1).astype(BF16)
            q_full = _mla_q(h, gmix, sh1, sc1, mla_w_q_a[j].astype(BF16),
                            mla_q_norm[j].reshape(1, Q_LORA), wq_p, cos_m, sin_m, S)
            mix = _flash(q_full, k_full, v_full, S)
            w_o = mla_w_o[j].astype(BF16)
        h, xn, rows, cols, tbl, used = _outproj_route(mix, w_o, h, g1, norm_ffn[layer].reshape(1, D),
                                                      sh2, sc2, wr_t, br, S, lay)
        h = _moe(xn, rows, cols, tbl, used[:, 0], moe_w_gate, moe_w_up, moe_w_down, layer,
                 h, g2, S, lay, final_norm.reshape(1, D) if layer == depth - 1 else None)
    return h.reshape(B, S, D)
```

```python
import functools

import jax
import jax.numpy as jnp
from jax import lax
from jax.experimental import pallas as pl
from jax.experimental.pallas import tpu as pltpu

F32 = jnp.float32
BF16 = jnp.bfloat16

CHUNK = 64
RET_HEADS = 4
MLA_HEADS = 8
QK_NOPE = 128
QK_ROPE = 64
V_HEAD = 128
Q_LORA = 256
KV_LORA = 128
N_EXPERTS = 16
N_GROUPS = 4
EXPERTS_PER_GROUP = N_EXPERTS // N_GROUPS
D_EXPERT = 512
ROPE_THETA = 10000.0
EPS = 1e-6

LANES = 128
VMEM_LIMIT = 56 * 1024 * 1024
LOG2E = 1.4426950408889634

RET_CHUNK = 256
TOK_TILE = 512
ATT_TILE = 1024
ATT_KEYS = 512
ATT_HEADS = 2
ATT_SLAB = 32
SUBLANES = 8
MXU_DIM = 256
MOE_GROUP = 16
MOE_ROW_TILE = 1024


def _cparams(*sem):
    return pltpu.CompilerParams(dimension_semantics=sem, vmem_limit_bytes=VMEM_LIMIT)


def _silu(x):
    return x * jax.nn.sigmoid(x)


def _rms(x, g):
    return x * lax.rsqrt(jnp.mean(x * x, axis=-1, keepdims=True) + EPS) * g


def _norm_mod(h, g, shift, scale):
    return _rms(h, g) * (1.0 + scale) + shift


def _dot(a, b):
    return jnp.dot(a, b, preferred_element_type=F32)


def _dot_nt(a, b, **kw):
    return lax.dot_general(a, b, (((1,), (1,)), ((), ())), preferred_element_type=F32, **kw)


def _dot_tn(a, b):
    return lax.dot_general(a, b, (((0,), (0,)), ((), ())), preferred_element_type=F32)


def _mod_body(c_ref, w_ref, b_ref, o_ref):
    ca = _silu(c_ref[...])
    o_ref[0] = jnp.dot(ca, w_ref[0], preferred_element_type=F32,
                       precision=lax.Precision.HIGHEST) + b_ref[0]


def _mod_vectors(c8, w, b):
    L, D, N = w.shape
    tn = D
    assert N % tn == 0
    return pl.pallas_call(
        _mod_body,
        out_shape=jax.ShapeDtypeStruct((L, 8, N), F32),
        grid=(L, N // tn),
        in_specs=[pl.BlockSpec((8, D), lambda l, j: (0, 0)),
                  pl.BlockSpec((1, D, tn), lambda l, j: (l, 0, j)),
                  pl.BlockSpec((1, 1, tn), lambda l, j: (l, 0, j))],
        out_specs=pl.BlockSpec((1, 8, tn), lambda l, j: (l, 0, j)),
        compiler_params=_cparams("parallel", "parallel"),
        name="mod_vectors",
    )(c8, w, b.reshape(L, 1, N))


def _rope_body(pos_ref, inv_ref, cm_ref, sm_ref, cos_ref, sin_ref):
    ang = pos_ref[...].astype(F32) * inv_ref[...]
    cos_ref[...] = jnp.cos(ang) * cm_ref[...]
    sin_ref[...] = jnp.sin(ang) * sm_ref[...]


def _rope_tables(pos_col, inv, cm, sm):
    T = pos_col.shape[0]
    tm = min(T, 1024)
    row = pl.BlockSpec((1, LANES), lambda i: (0, 0))
    return pl.pallas_call(
        _rope_body,
        out_shape=(jax.ShapeDtypeStruct((T, LANES), F32),) * 2,
        grid=(T // tm,),
        in_specs=[pl.BlockSpec((tm, 1), lambda i: (i, 0)), row, row, row],
        out_specs=(pl.BlockSpec((tm, LANES), lambda i: (i, 0)),) * 2,
        compiler_params=_cparams("parallel"),
        name="rope_tables",
    )(pos_col, inv, cm, sm)


def _inproj_body(h_ref, g_ref, sh_ref, sc_ref, w_ref, cos_ref, sin_ref, o_ref, *, tn, dk_dim):
    xn = _norm_mod(h_ref[...], g_ref[...], sh_ref[0], sc_ref[0]).astype(BF16)
    d_model = h_ref.shape[1]
    half = dk_dim // 2
    for j in range(w_ref.shape[1] // tn):
        acc = _dot(xn, w_ref[:, j * tn:(j + 1) * tn])
        if j * tn >= 2 * d_model:
            o_ref[:, j * tn:(j + 1) * tn] = acc.astype(BF16)
            continue
        cos = cos_ref[...]
        sin = sin_ref[...]
        scale = 1.0 if j * tn < d_model else dk_dim ** -0.5
        for hd in range(tn // dk_dim):
            x1 = acc[:, hd * dk_dim:hd * dk_dim + half]
            x2 = acc[:, hd * dk_dim + half:(hd + 1) * dk_dim]
            c0 = j * tn + hd * dk_dim
            o_ref[:, c0:c0 + half] = ((x1 * cos - x2 * sin) * scale).astype(BF16)
            o_ref[:, c0 + half:c0 + dk_dim] = ((x1 * sin + x2 * cos) * scale).astype(BF16)


def _ret_inproj(h, g, sh, sc, w, cos, sin, S):
    T, D = h.shape
    N = w.shape[1]
    tm = min(TOK_TILE, S)
    per_b = S // tm
    vec = pl.BlockSpec((1, 1, D), lambda i: (i // per_b, 0, 0))
    tab = pl.BlockSpec((tm, LANES), lambda i: (i, 0))
    return pl.pallas_call(
        functools.partial(_inproj_body, tn=512, dk_dim=D // RET_HEADS),
        out_shape=jax.ShapeDtypeStruct((T, N), BF16),
        grid=(T // tm,),
        in_specs=[pl.BlockSpec((tm, D), lambda i: (i, 0)),
                  pl.BlockSpec((1, D), lambda i: (0, 0)),
                  vec, vec,
                  pl.BlockSpec((D, N), lambda i: (0, 0)),
                  tab, tab],
        out_specs=pl.BlockSpec((tm, N), lambda i: (i, 0)),
        compiler_params=_cparams("parallel"),
        name="ret_inproj",
    )(h, g, sh, sc, w, cos, sin)


def _ret_body(q_ref, k_ref, v_ref, g_ref, di_ref, dq_ref, dk_ref, dc_ref,
              y_ref, state_ref, *, dk_dim):
    @pl.when(pl.program_id(1) == 0)
    def _():
        state_ref[...] = jnp.zeros_like(state_ref)

    dv_dim = 2 * dk_dim
    for hd in range(RET_HEADS):
        qk = slice(hd * dk_dim, (hd + 1) * dk_dim)
        vg = slice(hd * dv_dim, (hd + 1) * dv_dim)
        qb = q_ref[:, qk]
        kb = k_ref[:, qk]
        v = v_ref[:, vg]
        inner = (_dot_nt(qb, kb) * di_ref[hd]).astype(BF16)
        st = state_ref[hd]
        out = _dot(inner, v) + _dot(qb, st.astype(BF16)) * dq_ref[hd]
        kd = (kb.astype(F32) * dk_ref[hd]).astype(BF16)
        state_ref[hd] = st * dc_ref[hd] + _dot_tn(kd, v)

        mu = jnp.mean(out, axis=-1, keepdims=True)
        cen = out - mu
        var = jnp.mean(cen * cen, axis=-1, keepdims=True)
        o = cen * lax.rsqrt(var + EPS)
        y_ref[:, vg] = (_silu(g_ref[:, vg].astype(F32)) * o).astype(BF16)


def _retention(proj, S, D):
    T = proj.shape[0]
    B = T // S
    H = RET_HEADS
    dk = D // H
    dv = 2 * dk
    C = min(RET_CHUNK, S)
    n = S // C
    log_g = jnp.log1p(-(2.0 ** (-5.0 - jnp.arange(H, dtype=F32))))
    t = jnp.arange(C, dtype=F32)
    diff = t[:, None] - t[None, :]
    d_intra = jnp.where(diff >= 0, jnp.exp(log_g[:, None, None] * jnp.maximum(diff, 0.0)), 0.0)
    d_q = jnp.exp(log_g[:, None] * (t + 1.0))[:, :, None]
    d_k = jnp.exp(log_g[:, None] * (C - 1.0 - t))[:, :, None]
    d_c = jnp.exp(log_g * C)[:, None, None]

    assert 2 * D == H * dv
    row = lambda b, i: b * n + i
    const = lambda a: pl.BlockSpec(a.shape, lambda b, i: (0, 0, 0))
    return pl.pallas_call(
        functools.partial(_ret_body, dk_dim=dk),
        out_shape=jax.ShapeDtypeStruct((T, H * dv), BF16),
        grid=(B, n),
        in_specs=[pl.BlockSpec((C, D), lambda b, i: (row(b, i), 0)),
                  pl.BlockSpec((C, D), lambda b, i: (row(b, i), 1)),
                  pl.BlockSpec((C, H * dv), lambda b, i: (row(b, i), 1)),
                  pl.BlockSpec((C, H * dv), lambda b, i: (row(b, i), 2)),
                  const(d_intra), const(d_q), const(d_k), const(d_c)],
        out_specs=pl.BlockSpec((C, H * dv), lambda b, i: (row(b, i), 0)),
        scratch_shapes=[pltpu.VMEM((H, dk, dv), F32)],
        compiler_params=_cparams("parallel", "arbitrary"),
        name="retention",
    )(proj, proj, proj, proj, d_intra, d_q, d_k, d_c)


def _route(logits_t, bias):
    sc = jax.nn.sigmoid(logits_t)
    bi = sc + bias
    s_rows = [sc[e:e + 1, :] for e in range(N_EXPERTS)]
    b_rows = [bi[e:e + 1, :] for e in range(N_EXPERTS)]

    def top2sum(a, b, c, d):
        p, q = jnp.maximum(a, b), jnp.minimum(a, b)
        r, s = jnp.maximum(c, d), jnp.minimum(c, d)
        return jnp.maximum(p, r) + jnp.maximum(jnp.minimum(p, r), jnp.maximum(q, s))

    n = EXPERTS_PER_GROUP
    gs = [top2sum(*b_rows[n * g:n * g + n]) for g in range(N_GROUPS)]
    best, gi = gs[0], jnp.zeros_like(gs[0], dtype=jnp.int32)
    for g in range(1, N_GROUPS):
        upd = gs[g] > best
        gi = jnp.where(upd, g, gi)
        best = jnp.where(upd, gs[g], best)

    def pick(rows, j):
        out = rows[j]
        for g in range(1, N_GROUPS):
            out = jnp.where(gi == g, rows[n * g + j], out)
        return out

    vb = [pick(b_rows, j) for j in range(n)]
    vs = [pick(s_rows, j) for j in range(n)]

    def argmax_first(vals):
        best, idx = vals[0], jnp.zeros_like(gi)
        for j in range(1, n):
            upd = vals[j] > best
            idx = jnp.where(upd, j, idx)
            best = jnp.where(upd, vals[j], best)
        return idx

    i1 = argmax_first(vb)
    i2 = argmax_first([jnp.where(i1 == j, -jnp.inf, vb[j]) for j in range(n)])

    def take(vals, idx):
        out = vals[0]
        for j in range(1, n):
            out = jnp.where(idx == j, vals[j], out)
        return out

    w1, w2 = take(vs, i1), take(vs, i2)
    tot = w1 + w2
    w1, w2 = w1 / tot, w2 / tot
    return gi * n + i1, gi * n + i2, w1, w2


class _MoeLayout:
    def __init__(self, T, tm):
        self.tm = tm
        self.n_tiles = T // tm
        self.group = MOE_GROUP
        self.rt = -(-(2 * tm + N_EXPERTS * (MOE_GROUP - 1)) // MXU_DIM) * MXU_DIM
        self.ng = self.rt // MOE_GROUP
        assert self.ng < LANES
        self.tmx = MOE_ROW_TILE
        self.cap = -(-(T + self.n_tiles * MOE_GROUP) // self.tmx) * self.tmx
        self.dump = N_EXPERTS * self.cap
        self.rows = self.dump + self.rt
        pad = self.n_tiles * N_EXPERTS * (MOE_GROUP - 1)
        self.nt = (2 * T + pad) // self.tmx + N_EXPERTS


def _dispatch_meta(e1, e2, cum, tri, lay):
    E, G = N_EXPERTS, lay.group
    tm = e1.shape[1]
    eid = lax.broadcasted_iota(jnp.int32, (E, tm), 0)
    oh1, oh2 = eid == e1, eid == e2
    cnt = jnp.where(oh1 | oh2, 1.0, 0.0)
    pre = _dot(cnt.astype(BF16), tri)
    tot = jnp.sum(cnt, axis=1, keepdims=True)
    ptot = jnp.broadcast_to(jnp.ceil(tot * (1.0 / G)) * G, (E, LANES))
    below = jnp.where(lax.broadcasted_iota(jnp.int32, (E, E), 0) > lax.broadcasted_iota(jnp.int32, (E, E), 1),
                      1.0, 0.0)
    loff = jnp.dot(below, ptot, preferred_element_type=F32, precision=lax.Precision.HIGHEST)
    pos_e = loff[:, :1] + pre
    pos1 = jnp.sum(jnp.where(oh1, pos_e, 0.0), axis=0, keepdims=True)
    pos2 = jnp.sum(jnp.where(oh2, pos_e, 0.0), axis=0, keepdims=True)

    lane = lax.broadcasted_iota(jnp.int32, (E, LANES), 1)
    g_row = (lane * G).astype(F32)
    eg = jnp.sum(jnp.where(loff + ptot <= g_row, 1, 0), axis=0, keepdims=True)
    erow = lax.broadcasted_iota(jnp.int32, (E, LANES), 0)
    base = erow.astype(F32) * float(lay.cap) + cum - loff
    sel = jnp.sum(jnp.where(erow == eg, base, 0.0), axis=0, keepdims=True)
    dst = jnp.where(eg < E, g_row[:1] + sel, float(lay.dump) + g_row[:1])
    n_used = jnp.sum(ptot[:, :1], axis=0, keepdims=True) * (1.0 / G)
    table = jnp.where(lane[:1] == LANES - 1, n_used, dst).astype(jnp.int32)
    return pos1, pos2, table, cum + ptot


def _outproj_body(y_ref, w_ref, h_ref, g1_ref, gn_ref, sh_ref, sc_ref, wr_ref, br_ref, tri_ref,
                  ho_ref, xn_ref, rows_ref, cols_ref, tbl_ref, cum_ref, *, lay):
    @pl.when(pl.program_id(0) == 0)
    def _():
        cum_ref[...] = jnp.zeros_like(cum_ref)

    hn = h_ref[...] + g1_ref[0] * _dot(y_ref[...], w_ref[...])
    ho_ref[...] = hn
    xn = _norm_mod(hn, gn_ref[...], sh_ref[0], sc_ref[0])
    xb = xn.astype(BF16)
    xn_ref[...] = xb
    xl = (xn - xb.astype(F32)).astype(BF16)
    hl = _dot_nt(wr_ref[...], xb)
    logits_t = hl[:N_EXPERTS] + hl[N_EXPERTS:] + _dot_nt(wr_ref[:N_EXPERTS, :], xl)
    e1, e2, w1, w2 = _route(logits_t, br_ref[...])
    pos1, pos2, table, cum = _dispatch_meta(e1, e2, cum_ref[...], tri_ref[...], lay)
    cum_ref[...] = cum
    tbl_ref[0] = table
    tm = e1.shape[1]
    rows = jnp.concatenate([pos1, pos2, w1, w2, jnp.zeros((SUBLANES - 4, tm), F32)], axis=0)
    rows_ref[...] = rows
    cols_ref[...] = jnp.concatenate([rows, jnp.zeros((LANES - SUBLANES, tm), F32)], axis=0).T


def _outproj_route(y, w, h, g1, gn, sh, sc, wr_t, br, S, lay):
    T, D = h.shape
    K = y.shape[1]
    tm = lay.tm
    per_b = S // tm
    vec = pl.BlockSpec((1, 1, D), lambda i: (i // per_b, 0, 0))
    tok = lambda n: pl.BlockSpec((tm, n), lambda i: (i, 0))
    tri = jnp.triu(jnp.ones((tm, tm), BF16), k=1)
    return pl.pallas_call(
        functools.partial(_outproj_body, lay=lay),
        out_shape=(jax.ShapeDtypeStruct((T, D), F32),
                   jax.ShapeDtypeStruct((T, D), BF16),
                   jax.ShapeDtypeStruct((SUBLANES, T), F32),
                   jax.ShapeDtypeStruct((T, LANES), F32),
                   jax.ShapeDtypeStruct((lay.n_tiles, 1, LANES), jnp.int32),
                   jax.ShapeDtypeStruct((N_EXPERTS, LANES), F32)),
        grid=(T // tm,),
        in_specs=[tok(K),
                  pl.BlockSpec((K, D), lambda i: (0, 0)),
                  tok(D), vec,
                  pl.BlockSpec((1, D), lambda i: (0, 0)),
                  vec, vec,
                  pl.BlockSpec((2 * N_EXPERTS, D), lambda i: (0, 0)),
                  pl.BlockSpec((N_EXPERTS, 1), lambda i: (0, 0)),
                  pl.BlockSpec((tm, tm), lambda i: (0, 0))],
        out_specs=(tok(D), tok(D),
                   pl.BlockSpec((SUBLANES, tm), lambda i: (0, i)),
                   tok(LANES),
                   pl.BlockSpec((1, 1, LANES), lambda i: (i, 0, 0)),
                   pl.BlockSpec((N_EXPERTS, LANES), lambda i: (0, 0))),
        compiler_params=_cparams("arbitrary"),
        name="outproj_route",
    )(y, w, h, g1, gn, sh, sc, wr_t, br, tri)


def _group_copy(hbm_ref, buf_ref, tbl_ref, i, g_hbm, g_vmem, sem, lay, to_hbm):
    G = lay.group
    hbm = hbm_ref.at[pl.ds(pl.multiple_of(tbl_ref[i * LANES + g_hbm], G), G), :]
    vmem = buf_ref.at[pl.ds(g_vmem * G, G), :]
    return pltpu.make_async_copy(vmem, hbm, sem) if to_hbm else pltpu.make_async_copy(hbm, vmem, sem)


def _dispatch_body(tbl_ref, x_ref, rows_ref, xs_ref, buf_ref, sem, *, lay):
    i = pl.program_id(0)
    pos1 = rows_ref[0:1, :].astype(jnp.int32)
    pos2 = rows_ref[1:2, :].astype(jnp.int32)
    r = lax.broadcasted_iota(jnp.int32, (lay.rt, lay.tm), 0)
    perm = jnp.where((r == pos1) | (r == pos2), 1.0, 0.0).astype(BF16)
    slot = i % 2
    buf_ref[slot] = _dot(perm, x_ref[...]).astype(BF16)

    def copies(tile, slot_):
        return [_group_copy(xs_ref, buf_ref.at[slot_], tbl_ref, tile, g, g, sem.at[slot_], lay, True)
                for g in range(lay.ng)]

    @pl.when(i > 0)
    def _():
        for c in copies(i - 1, 1 - slot):
            c.wait()

    for c in copies(i, slot):
        c.start()

    @pl.when(i == pl.num_programs(0) - 1)
    def _():
        for c in copies(i, slot):
            c.wait()


def _moe_dispatch(xn, rows, tbl, lay):
    T, D = xn.shape
    tm = lay.tm
    grid_spec = pltpu.PrefetchScalarGridSpec(
        num_scalar_prefetch=1,
        grid=(lay.n_tiles,),
        in_specs=[pl.BlockSpec((tm, D), lambda i, tbl: (i, 0)),
                  pl.BlockSpec((SUBLANES, tm), lambda i, tbl: (0, i))],
        out_specs=pl.BlockSpec(memory_space=pl.ANY),
        scratch_shapes=[pltpu.VMEM((2, lay.rt, D), BF16), pltpu.SemaphoreType.DMA((2,))],
    )
    return pl.pallas_call(
        functools.partial(_dispatch_body, lay=lay),
        out_shape=jax.ShapeDtypeStruct((lay.rows, D), BF16),
        grid_spec=grid_spec,
        compiler_params=_cparams("arbitrary"),
        name="moe_dispatch",
    )(tbl, xn, rows)


def _expert_body(te_ref, tb_ref, nv_ref, x_ref, wg_ref, wu_ref, wd_ref, y_ref, wgu_bf, wd_bf):
    n = pl.program_id(0)
    f = wg_ref.shape[2]

    @pl.when((n == 0) | (te_ref[n] != te_ref[jnp.maximum(n - 1, 0)]))
    def _():
        wgu_bf[:, :f] = wg_ref[0].astype(BF16)
        wgu_bf[:, f:] = wu_ref[0].astype(BF16)
        wd_bf[...] = wd_ref[0].astype(BF16)

    @pl.when(n < nv_ref[0])
    def _():
        hgu = _dot(x_ref[...], wgu_bf[...])
        hdn = (_silu(hgu[:, :f]) * hgu[:, f:]).astype(BF16)
        y_ref[...] = _dot(hdn, wd_bf[...]).astype(BF16)


def _moe_experts(xs, wg, wu, wd, layer, tile_e, tile_blk, n_valid, lay):
    R, D = xs.shape
    F = wg.shape[3]
    tmx = lay.tmx
    wspec = lambda shape: pl.BlockSpec((None, 1) + shape, lambda n, te, tb, nv: (layer, te[n], 0, 0))
    grid_spec = pltpu.PrefetchScalarGridSpec(
        num_scalar_prefetch=3,
        grid=(lay.nt,),
        in_specs=[pl.BlockSpec((tmx, D), lambda n, te, tb, nv: (tb[n], 0)),
                  wspec((D, F)), wspec((D, F)), wspec((F, D))],
        out_specs=pl.BlockSpec((tmx, D), lambda n, te, tb, nv: (tb[n], 0)),
        scratch_shapes=[pltpu.VMEM((D, 2 * F), BF16), pltpu.VMEM((F, D), BF16)],
    )
    return pl.pallas_call(
        _expert_body,
        out_shape=jax.ShapeDtypeStruct((R, D), BF16),
        grid_spec=grid_spec,
        compiler_params=_cparams("arbitrary"),
        name="moe_experts",
    )(tile_e, tile_blk, n_valid, xs, wg, wu, wd)


def _combine_body(tbl_ref, cols_ref, h_ref, g2_ref, *rest, lay, final):
    if final:
        gfin_ref, ys_ref, o_ref, buf_ref, sem = rest
    else:
        ys_ref, o_ref, buf_ref, sem = rest
    i = pl.program_id(0)
    last = pl.num_programs(0) - 1
    slot = i % 2

    def copies(tile, slot_):
        n_used = tbl_ref[tile * LANES + LANES - 1]
        return [_group_copy(ys_ref, buf_ref.at[slot_], tbl_ref, tile, jnp.where(g < n_used, g, 0), g,
                            sem.at[slot_], lay, False) for g in range(lay.ng)]

    @pl.when(i == 0)
    def _():
        for c in copies(0, 0):
            c.start()

    nxt = jnp.minimum(i + 1, last)
    for c in copies(nxt, 1 - slot):
        c.start()
    for c in copies(i, slot):
        c.wait()

    cols = cols_ref[...]
    r = lax.broadcasted_iota(jnp.int32, (lay.tm, lay.rt), 1)
    mix = (jnp.where(r == cols[:, 0:1].astype(jnp.int32), cols[:, 2:3], 0.0)
           + jnp.where(r == cols[:, 1:2].astype(jnp.int32), cols[:, 3:4], 0.0)).astype(BF16)
    out = h_ref[...] + g2_ref[0] * _dot(mix, buf_ref[slot])
    o_ref[...] = _rms(out, gfin_ref[...]) if final else out

    @pl.when(i == last)
    def _():
        for c in copies(nxt, 1 - slot):
            c.wait()


def _moe_combine(ys, cols, tbl, h, g2, S, lay, final_gain=None):
    T, D = h.shape
    tm = lay.tm
    per_b = S // tm
    final = final_gain is not None
    in_specs = [pl.BlockSpec((tm, LANES), lambda i, tbl: (i, 0)),
                pl.BlockSpec((tm, D), lambda i, tbl: (i, 0)),
                pl.BlockSpec((1, 1, D), lambda i, tbl: (i // per_b, 0, 0))]
    args = [cols, h, g2]
    if final:
        in_specs.append(pl.BlockSpec((1, D), lambda i, tbl: (0, 0)))
        args.append(final_gain)
    grid_spec = pltpu.PrefetchScalarGridSpec(
        num_scalar_prefetch=1,
        grid=(lay.n_tiles,),
        in_specs=in_specs + [pl.BlockSpec(memory_space=pl.ANY)],
        out_specs=pl.BlockSpec((tm, D), lambda i, tbl: (i, 0)),
        scratch_shapes=[pltpu.VMEM((2, lay.rt, D), BF16), pltpu.SemaphoreType.DMA((2,))],
    )
    return pl.pallas_call(
        functools.partial(_combine_body, lay=lay, final=final),
        out_shape=jax.ShapeDtypeStruct((T, D), F32),
        grid_spec=grid_spec,
        compiler_params=_cparams("arbitrary"),
        name="moe_combine",
    )(tbl, *args, ys)


def _expert_tiles(used, lay):
    tiles = jnp.ceil(used / lay.tmx).astype(jnp.int32)
    ends = jnp.cumsum(tiles)
    n_valid = ends[-1]
    n = jnp.minimum(jnp.arange(lay.nt, dtype=jnp.int32), n_valid - 1)
    e = jnp.sum((ends[None, :] <= n[:, None]).astype(jnp.int32), axis=1)
    blk = e * (lay.cap // lay.tmx) + n - (ends - tiles)[e]
    return e, blk, n_valid.reshape(1)


def _moe(xn, rows, cols, tbl, used, wg, wu, wd, layer, h, g2, S, lay, final_gain=None):
    tbl = tbl.reshape(-1)
    xs = _moe_dispatch(xn, rows, tbl, lay)
    ys = _moe_experts(xs, wg, wu, wd, layer, *_expert_tiles(used, lay), lay)
    return _moe_combine(ys, cols, tbl, h, g2, S, lay, final_gain)


def _kv_body(h_ref, g_ref, sh_ref, sc_ref, wa_ref, gkv_ref, wb_ref, cos_ref, sin_ref,
             k_ref, vt_ref):
    hn = _norm_mod(h_ref[...], g_ref[...], sh_ref[0], sc_ref[0]).astype(BF16)
    a = _dot(hn, wa_ref[...])
    c_kv = _rms(a[:, :KV_LORA], gkv_ref[...]).astype(BF16)
    kr = (a[:, KV_LORA:KV_LORA + LANES] * cos_ref[...]
          + a[:, KV_LORA + LANES:] * sin_ref[...]).astype(BF16)
    kv = _dot(c_kv, wb_ref[...])
    hk = MLA_HEADS * QK_NOPE
    w = QK_NOPE + LANES
    for hd in range(MLA_HEADS):
        k_ref[:, hd * w:hd * w + QK_NOPE] = kv[:, hd * QK_NOPE:(hd + 1) * QK_NOPE].astype(BF16)
        k_ref[:, hd * w + QK_NOPE:(hd + 1) * w] = kr
        vh = kv[:, hk + hd * V_HEAD:hk + (hd + 1) * V_HEAD]
        for g in range(vt_ref.shape[1]):
            vt_ref[hd, g] = vh[g * ATT_KEYS:(g + 1) * ATT_KEYS, :].T.astype(BF16)


def _mla_kv(h, g, sh, sc, wa, gkv, wb, cos, sin, S):
    T, D = h.shape
    tm = min(TOK_TILE, S)
    per_b = S // tm
    vec = pl.BlockSpec((1, 1, D), lambda i: (i // per_b, 0, 0))
    tok = lambda n: pl.BlockSpec((tm, n), lambda i: (i, 0))
    full = lambda a: pl.BlockSpec(a.shape, lambda i: (0, 0))
    kw = MLA_HEADS * (QK_NOPE + LANES)
    gk = tm // ATT_KEYS
    vt_shape = (MLA_HEADS, T // ATT_KEYS, V_HEAD, ATT_KEYS)
    return pl.pallas_call(
        _kv_body,
        out_shape=(jax.ShapeDtypeStruct((T, kw), BF16), jax.ShapeDtypeStruct(vt_shape, BF16)),
        grid=(T // tm,),
        in_specs=[tok(D), full(g), vec, vec, full(wa), full(gkv), full(wb), tok(LANES), tok(LANES)],
        out_specs=(tok(kw),
                   pl.BlockSpec((MLA_HEADS, gk, V_HEAD, ATT_KEYS), lambda i: (0, i, 0, 0))),
        compiler_params=_cparams("parallel"),
        name="mla_kv",
    )(h, g, sh, sc, wa, gkv, wb, cos, sin)


def _q_body(h_ref, g_ref, sh_ref, sc_ref, wa_ref, gq_ref, wb_ref, cos_ref, sin_ref, q_ref, *, scale):
    xn = _norm_mod(h_ref[...], g_ref[...], sh_ref[0], sc_ref[0]).astype(BF16)
    qa = _rms(_dot(xn, wa_ref[...]), gq_ref[...] * scale).astype(BF16)
    cos = cos_ref[...]
    sin = sin_ref[...]
    wi = QK_NOPE + 2 * LANES
    wo = QK_NOPE + LANES
    for hd in range(MLA_HEADS):
        qb = _dot(qa, wb_ref[:, hd * wi:(hd + 1) * wi])
        q_ref[:, hd * wo:hd * wo + QK_NOPE] = qb[:, :QK_NOPE].astype(BF16)
        rp = qb[:, QK_NOPE:QK_NOPE + LANES] * cos + qb[:, QK_NOPE + LANES:] * sin
        q_ref[:, hd * wo + QK_NOPE:(hd + 1) * wo] = rp.astype(BF16)


def _mla_q(h, g, sh, sc, wa, gq, wb, cos, sin, S):
    T, D = h.shape
    tm = min(TOK_TILE, S)
    per_b = S // tm
    vec = pl.BlockSpec((1, 1, D), lambda i: (i // per_b, 0, 0))
    tok = lambda n: pl.BlockSpec((tm, n), lambda i: (i, 0))
    full = lambda a: pl.BlockSpec(a.shape, lambda i: (0, 0))
    qw = MLA_HEADS * (QK_NOPE + LANES)
    return pl.pallas_call(
        functools.partial(_q_body, scale=(QK_NOPE + QK_ROPE) ** -0.5 * LOG2E),
        out_shape=jax.ShapeDtypeStruct((T, qw), BF16),
        grid=(T // tm,),
        in_specs=[tok(D), full(g), vec, vec, full(wa), full(gq), full(wb), tok(LANES), tok(LANES)],
        out_specs=tok(qw),
        compiler_params=_cparams("parallel"),
        name="mla_q",
    )(h, g, sh, sc, wa, gq, wb, cos, sin)


def _flash_body(q_ref, k_ref, vt_ref, bias_ref, o_ref, s_ref, smax_ref, m_ref, l_ref, acc_ref,
                *, tq, ks, hp):
    i = pl.program_id(2)
    last = (i * tq + tq - 1) // ks
    dq = QK_NOPE + LANES
    heads = range(hp)

    def scores(hd, slot, c, lo=0):
        kc = k_ref[pl.ds(pl.multiple_of(c * ks, ks), ks), hd * dq:(hd + 1) * dq]
        s = _dot_nt(kc, q_ref[lo:, hd * dq:(hd + 1) * dq])
        s_ref[hd, slot, :, lo:] = s
        smax_ref[hd, slot, :, lo:] = jnp.max(s, axis=0, keepdims=True)

    def update(hd, slot, c, lo=0, diag=None):
        if diag is None:
            smax = smax_ref[hd, slot, :, lo:]
        else:
            s_ref[hd, slot, :, diag:diag + ks] += bias_ref[...]
            smax = jnp.max(s_ref[hd, slot, :, lo:], axis=0, keepdims=True)
        m = m_ref[hd, :, lo:]
        m_new = jnp.maximum(m, smax)
        alpha = jnp.exp2(m - m_new)
        part = jnp.zeros((SUBLANES, tq - lo), F32)
        ps = []
        for r in range(ks // ATT_SLAB):
            p = jnp.exp2(s_ref[hd, slot, r * ATT_SLAB:(r + 1) * ATT_SLAB, lo:] - m_new)
            for r8 in range(ATT_SLAB // SUBLANES):
                part = part + p[r8 * SUBLANES:(r8 + 1) * SUBLANES, :]
            ps.append(p.astype(BF16))
        m_ref[hd, :, lo:] = m_new
        l_ref[hd, :, lo:] = alpha * l_ref[hd, :, lo:] + jnp.sum(part, axis=0, keepdims=True)
        acc_ref[hd, :, lo:] = (alpha * acc_ref[hd, :, lo:]
                               + _dot(vt_ref[hd, c], jnp.concatenate(ps, axis=0)))

    m_ref[...] = jnp.full_like(m_ref, -jnp.inf)
    l_ref[...] = jnp.zeros_like(l_ref)
    acc_ref[...] = jnp.zeros_like(acc_ref)
    for hd in heads:
        scores(hd, 0, 0)

    def pair(g, _):
        for hd in heads:
            scores(hd, 1, 2 * g + 1)
        for hd in heads:
            update(hd, 0, 2 * g)
        for hd in heads:
            scores(hd, 0, 2 * g + 2)
        for hd in heads:
            update(hd, 1, 2 * g + 1)
        return 0

    assert tq == 2 * ks
    lax.fori_loop(0, i, pair, 0)
    for hd in heads:
        scores(hd, 1, last, lo=ks)
    for hd in heads:
        update(hd, 0, last - 1, diag=0)
    for hd in heads:
        update(hd, 1, last, lo=ks, diag=ks)

    for hd in heads:
        o_ref[:, hd * V_HEAD:(hd + 1) * V_HEAD] = (acc_ref[hd] / l_ref[hd]).T.astype(o_ref.dtype)


def _flash(q, k, vt, S):
    T = q.shape[0]
    B = T // S
    H = MLA_HEADS
    hp = ATT_HEADS
    tq = min(ATT_TILE, S)
    ks = ATT_KEYS
    nq = S // tq
    dq = QK_NOPE + LANES
    chunk = jnp.arange(ks, dtype=jnp.int32) // CHUNK
    bias = jnp.where(chunk[:, None] <= chunk[None, :], 0.0, -jnp.inf).astype(F32)
    return pl.pallas_call(
        functools.partial(_flash_body, tq=tq, ks=ks, hp=hp),
        out_shape=jax.ShapeDtypeStruct((T, H * V_HEAD), BF16),
        grid=(B, H // hp, nq),
        in_specs=[pl.BlockSpec((tq, hp * dq), lambda b, h, i: (b * nq + i, h)),
                  pl.BlockSpec((S, hp * dq), lambda b, h, i: (b, h)),
                  pl.BlockSpec((hp, S // ks, V_HEAD, ks), lambda b, h, i: (h, b, 0, 0)),
                  pl.BlockSpec((ks, ks), lambda b, h, i: (0, 0))],
        out_specs=pl.BlockSpec((tq, hp * V_HEAD), lambda b, h, i: (b * nq + i, h)),
        scratch_shapes=[pltpu.VMEM((hp, 2, ks, tq), F32), pltpu.VMEM((hp, 2, 1, tq), F32),
                        pltpu.VMEM((hp, 1, tq), F32), pltpu.VMEM((hp, 1, tq), F32),
                        pltpu.VMEM((hp, V_HEAD, tq), F32)],
        compiler_params=_cparams("parallel", "parallel", "arbitrary"),
        name="flash_attn",
    )(q, k, vt, bias)


def _rope_swap(w):
    half = w.shape[-1] // 2
    return jnp.concatenate([w[..., half:], w[..., :half]], axis=-1)


def _pad_lanes(w):
    return jnp.pad(w, [(0, 0)] * (w.ndim - 1) + [(0, LANES - w.shape[-1])])


def kernel(x, c, positions, w_mod, b_mod, norm_mix, norm_ffn, ret_w_in, ret_w_out, w_mod_kv, b_mod_kv, norm_kv, mla_w_kv_a, mla_kv_norm, mla_w_kv_b, mla_w_q_a, mla_q_norm, mla_w_q_b, mla_w_o, router_w, router_b, moe_w_gate, moe_w_up, moe_w_down, final_norm):
    B, S, D = x.shape
    T = B * S
    depth = w_mod.shape[0]
    n_a = ret_w_in.shape[0]

    c8 = jnp.pad(c, ((0, 8 - B), (0, 0)))
    mod = _mod_vectors(c8, w_mod, b_mod)[:, :B]
    mod = mod.reshape(depth, B, 6, 1, D)
    kv_mod = _mod_vectors(c8, w_mod_kv[None], b_mod_kv[None])[0, :B].reshape(B, 2, 1, D)

    pos_col = positions.reshape(T, 1)
    ones = jnp.ones((1, LANES), F32)
    inv_ret = (ROPE_THETA ** (-jnp.arange(LANES, dtype=F32) / LANES)).reshape(1, LANES)
    cos_r, sin_r = _rope_tables(pos_col, inv_ret, ones, ones)
    hr = QK_ROPE // 2
    inv_m = ROPE_THETA ** (-jnp.arange(hr, dtype=F32) / hr)
    inv_m = _pad_lanes(jnp.concatenate([inv_m, inv_m])[None])
    cm = _pad_lanes(jnp.ones((1, QK_ROPE), F32))
    sm = _pad_lanes(jnp.concatenate([-jnp.ones((1, hr), F32), jnp.ones((1, hr), F32)], axis=-1))
    cos_m, sin_m = _rope_tables(pos_col, inv_m, cm, sm)

    wr_hi = router_w.T.astype(BF16)
    wr_t = jnp.concatenate([wr_hi, (router_w.T - wr_hi.astype(F32)).astype(BF16)], axis=0)
    br = router_b.reshape(N_EXPERTS, 1)

    h = x.reshape(T, D)
    lay = _MoeLayout(T, min(TOK_TILE, S))
    k_full = v_full = None
    for layer in range(depth):
        sh1, sc1, g1, sh2, sc2, g2 = (mod[layer, :, i] for i in range(6))
        gmix = norm_mix[layer].reshape(1, D)
        if layer < n_a:
            proj = _ret_inproj(h, gmix, sh1, sc1, ret_w_in[layer].astype(BF16), cos_r, sin_r, S)
            mix = _retention(proj, S, D)
            w_o = ret_w_out[layer].astype(BF16)
        else:
            if layer == n_a:
                wa = mla_w_kv_a
                wa_r = wa[:, KV_LORA:]
                wa_p = jnp.concatenate([wa[:, :KV_LORA], _pad_lanes(wa_r), _pad_lanes(_rope_swap(wa_r))],
                                       axis=-1).astype(BF16)
                wb = mla_w_kv_b.reshape(KV_LORA, MLA_HEADS, QK_NOPE + V_HEAD)
                wb_p = jnp.concatenate([wb[..., :QK_NOPE].reshape(KV_LORA, -1),
                                        wb[..., QK_NOPE:].reshape(KV_LORA, -1)], axis=-1).astype(BF16)
                k_full, v_full = _mla_kv(h, norm_kv.reshape(1, D), kv_mod[:, 0], kv_mod[:, 1], wa_p,
                                         mla_kv_norm.reshape(1, KV_LORA), wb_p, cos_m, sin_m, S)
            j = layer - n_a
            wq = mla_w_q_b[j].reshape(Q_LORA, MLA_HEADS, QK_NOPE + QK_ROPE)
            wq_r = wq[..., QK_NOPE:]
            wq_p = jnp.concatenate([wq[..., :QK_NOPE], _pad_lanes(wq_r), _pad_lanes(_rope_swap(wq_r))],
                                   axis=-1).reshape(Q_LORA, -1).astype(BF16)
            q_full = _mla_q(h, gmix, sh1, sc1, mla_w_q_a[j].astype(BF16),
                            mla_q_norm[j].reshape(1, Q_LORA), wq_p, cos_m, sin_m, S)
            mix = _flash(q_full, k_full, v_full, S)
            w_o = mla_w_o[j].astype(BF16)
        h, xn, rows, cols, tbl, used = _outproj_route(mix, w_o, h, g1, norm_ffn[layer].reshape(1, D),
                                                      sh2, sc2, wr_t, br, S, lay)
        h = _moe(xn, rows, cols, tbl, used[:, 0], moe_w_gate, moe_w_up, moe_w_down, layer,
                 h, g2, S, lay, final_norm.reshape(1, D) if layer == depth - 1 else None)
    return h.reshape(B, S, D)
```

```python
import functools

import jax
import jax.numpy as jnp
from jax import lax
from jax.experimental import pallas as pl
from jax.experimental.pallas import tpu as pltpu

F32 = jnp.float32
BF16 = jnp.bfloat16

CHUNK = 64
RET_HEADS = 4
MLA_HEADS = 8
QK_NOPE = 128
QK_ROPE = 64
V_HEAD = 128
Q_LORA = 256
KV_LORA = 128
N_EXPERTS = 16
N_GROUPS = 4
EXPERTS_PER_GROUP = N_EXPERTS // N_GROUPS
ROPE_THETA = 10000.0
EPS = 1e-6

LANES = 128
VMEM_LIMIT = 56 * 1024 * 1024
LOG2E = 1.4426950408889634

RET_CHUNK = 256
TOK_TILE = 512
ATT_TILE = 1024
ATT_KEYS = 512
ATT_HEADS = 2
ATT_SLAB = 32
SUBLANES = 8
MXU_DIM = 256
MOE_GROUP = 16
MOE_ROW_TILE = 1024


def _cparams(*sem):
    return pltpu.CompilerParams(dimension_semantics=sem, vmem_limit_bytes=VMEM_LIMIT)


def _silu(x):
    return x * jax.nn.sigmoid(x)


def _rms(x, g):
    return x * lax.rsqrt(jnp.mean(x * x, axis=-1, keepdims=True) + EPS) * g


def _norm_mod(h, g, shift, scale):
    return _rms(h, g) * (1.0 + scale) + shift


def _dot(a, b):
    return jnp.dot(a, b, preferred_element_type=F32)


def _dot_nt(a, b, **kw):
    return lax.dot_general(a, b, (((1,), (1,)), ((), ())), preferred_element_type=F32, **kw)


def _dot_tn(a, b):
    return lax.dot_general(a, b, (((0,), (0,)), ((), ())), preferred_element_type=F32)


def _mod_body(c_ref, w_ref, b_ref, o_ref):
    ca = _silu(c_ref[...])
    o_ref[0] = jnp.dot(ca, w_ref[0], preferred_element_type=F32,
                       precision=lax.Precision.HIGHEST) + b_ref[0]


def _mod_vectors(c8, w, b):
    L, D, N = w.shape
    tn = D
    assert N % tn == 0
    return pl.pallas_call(
        _mod_body,
        out_shape=jax.ShapeDtypeStruct((L, 8, N), F32),
        grid=(L, N // tn),
        in_specs=[pl.BlockSpec((8, D), lambda l, j: (0, 0)),
                  pl.BlockSpec((1, D, tn), lambda l, j: (l, 0, j)),
                  pl.BlockSpec((1, 1, tn), lambda l, j: (l, 0, j))],
        out_specs=pl.BlockSpec((1, 8, tn), lambda l, j: (l, 0, j)),
        compiler_params=_cparams("parallel", "parallel"),
        name="mod_vectors",
    )(c8, w, b.reshape(L, 1, N))


def _rope_body(pos_ref, inv_ref, cos_ref, sin_ref):
    ang = pos_ref[...].astype(F32) * inv_ref[...]
    cos_ref[...] = jnp.cos(ang)
    sin_ref[...] = jnp.sin(ang)


def _rope_tables(pos_col, inv):
    T = pos_col.shape[0]
    tm = min(T, 1024)
    return pl.pallas_call(
        _rope_body,
        out_shape=(jax.ShapeDtypeStruct((T, LANES), F32),) * 2,
        grid=(T // tm,),
        in_specs=[pl.BlockSpec((tm, 1), lambda i: (i, 0)), pl.BlockSpec((1, LANES), lambda i: (0, 0))],
        out_specs=(pl.BlockSpec((tm, LANES), lambda i: (i, 0)),) * 2,
        compiler_params=_cparams("parallel"),
        name="rope_tables",
    )(pos_col, inv)


def _inproj_body(h_ref, g_ref, sh_ref, sc_ref, w_ref, cos_ref, sin_ref, o_ref, *, tn, dk_dim):
    xn = _norm_mod(h_ref[...], g_ref[...], sh_ref[0], sc_ref[0]).astype(BF16)
    d_model = h_ref.shape[1]
    half = dk_dim // 2
    for j in range(w_ref.shape[1] // tn):
        acc = _dot(xn, w_ref[:, j * tn:(j + 1) * tn])
        if j * tn >= 2 * d_model:
            o_ref[:, j * tn:(j + 1) * tn] = acc.astype(BF16)
            continue
        cos = cos_ref[...]
        sin = sin_ref[...]
        scale = 1.0 if j * tn < d_model else dk_dim ** -0.5
        for hd in range(tn // dk_dim):
            x1 = acc[:, hd * dk_dim:hd * dk_dim + half]
            x2 = acc[:, hd * dk_dim + half:(hd + 1) * dk_dim]
            c0 = j * tn + hd * dk_dim
            o_ref[:, c0:c0 + half] = ((x1 * cos - x2 * sin) * scale).astype(BF16)
            o_ref[:, c0 + half:c0 + dk_dim] = ((x1 * sin + x2 * cos) * scale).astype(BF16)


def _ret_inproj(h, g, sh, sc, w, cos, sin, S):
    T, D = h.shape
    N = w.shape[1]
    tm = min(TOK_TILE, S)
    per_b = S // tm
    vec = pl.BlockSpec((1, 1, D), lambda i: (i // per_b, 0, 0))
    tab = pl.BlockSpec((tm, LANES), lambda i: (i, 0))
    return pl.pallas_call(
        functools.partial(_inproj_body, tn=512, dk_dim=D // RET_HEADS),
        out_shape=jax.ShapeDtypeStruct((T, N), BF16),
        grid=(T // tm,),
        in_specs=[pl.BlockSpec((tm, D), lambda i: (i, 0)),
                  pl.BlockSpec((1, D), lambda i: (0, 0)),
                  vec, vec,
                  pl.BlockSpec((D, N), lambda i: (0, 0)),
                  tab, tab],
        out_specs=pl.BlockSpec((tm, N), lambda i: (i, 0)),
        compiler_params=_cparams("parallel"),
        name="ret_inproj",
    )(h, g, sh, sc, w, cos, sin)


def _ret_body(q_ref, k_ref, v_ref, g_ref, di_ref, dq_ref, dk_ref, dc_ref,
              y_ref, state_ref, *, dk_dim):
    @pl.when(pl.program_id(1) == 0)
    def _():
        state_ref[...] = jnp.zeros_like(state_ref)

    dv_dim = 2 * dk_dim
    for hd in range(RET_HEADS):
        qk = slice(hd * dk_dim, (hd + 1) * dk_dim)
        vg = slice(hd * dv_dim, (hd + 1) * dv_dim)
        qb = q_ref[:, qk]
        kb = k_ref[:, qk]
        v = v_ref[:, vg]
        inner = (_dot_nt(qb, kb) * di_ref[hd]).astype(BF16)
        st = state_ref[hd]
        out = _dot(inner, v) + _dot(qb, st.astype(BF16)) * dq_ref[hd]
        kd = (kb.astype(F32) * dk_ref[hd]).astype(BF16)
        state_ref[hd] = st * dc_ref[hd] + _dot_tn(kd, v)

        mu = jnp.mean(out, axis=-1, keepdims=True)
        cen = out - mu
        var = jnp.mean(cen * cen, axis=-1, keepdims=True)
        o = cen * lax.rsqrt(var + EPS)
        y_ref[:, vg] = (_silu(g_ref[:, vg].astype(F32)) * o).astype(BF16)


def _retention(proj, S, D):
    T = proj.shape[0]
    B = T // S
    H = RET_HEADS
    dk = D // H
    dv = 2 * dk
    C = min(RET_CHUNK, S)
    n = S // C
    log_g = jnp.log1p(-(2.0 ** (-5.0 - jnp.arange(H, dtype=F32))))
    t = jnp.arange(C, dtype=F32)
    diff = t[:, None] - t[None, :]
    d_intra = jnp.where(diff >= 0, jnp.exp(log_g[:, None, None] * jnp.maximum(diff, 0.0)), 0.0)
    d_q = jnp.exp(log_g[:, None] * (t + 1.0))[:, :, None]
    d_k = jnp.exp(log_g[:, None] * (C - 1.0 - t))[:, :, None]
    d_c = jnp.exp(log_g * C)[:, None, None]

    assert 2 * D == H * dv
    row = lambda b, i: b * n + i
    const = lambda a: pl.BlockSpec(a.shape, lambda b, i: (0, 0, 0))
    return pl.pallas_call(
        functools.partial(_ret_body, dk_dim=dk),
        out_shape=jax.ShapeDtypeStruct((T, H * dv), BF16),
        grid=(B, n),
        in_specs=[pl.BlockSpec((C, D), lambda b, i: (row(b, i), 0)),
                  pl.BlockSpec((C, D), lambda b, i: (row(b, i), 1)),
                  pl.BlockSpec((C, H * dv), lambda b, i: (row(b, i), 1)),
                  pl.BlockSpec((C, H * dv), lambda b, i: (row(b, i), 2)),
                  const(d_intra), const(d_q), const(d_k), const(d_c)],
        out_specs=pl.BlockSpec((C, H * dv), lambda b, i: (row(b, i), 0)),
        scratch_shapes=[pltpu.VMEM((H, dk, dv), F32)],
        compiler_params=_cparams("parallel", "arbitrary"),
        name="retention",
    )(proj, proj, proj, proj, d_intra, d_q, d_k, d_c)


def _route(logits_t, bias):
    sc = jax.nn.sigmoid(logits_t)
    bi = sc + bias
    s_rows = [sc[e:e + 1, :] for e in range(N_EXPERTS)]
    b_rows = [bi[e:e + 1, :] for e in range(N_EXPERTS)]

    def top2sum(a, b, c, d):
        p, q = jnp.maximum(a, b), jnp.minimum(a, b)
        r, s = jnp.maximum(c, d), jnp.minimum(c, d)
        return jnp.maximum(p, r) + jnp.maximum(jnp.minimum(p, r), jnp.maximum(q, s))

    n = EXPERTS_PER_GROUP
    gs = [top2sum(*b_rows[n * g:n * g + n]) for g in range(N_GROUPS)]
    best, gi = gs[0], jnp.zeros_like(gs[0], dtype=jnp.int32)
    for g in range(1, N_GROUPS):
        upd = gs[g] > best
        gi = jnp.where(upd, g, gi)
        best = jnp.where(upd, gs[g], best)

    def pick(rows, j):
        out = rows[j]
        for g in range(1, N_GROUPS):
            out = jnp.where(gi == g, rows[n * g + j], out)
        return out

    vb = [pick(b_rows, j) for j in range(n)]
    vs = [pick(s_rows, j) for j in range(n)]

    def argmax_first(vals):
        best, idx = vals[0], jnp.zeros_like(gi)
        for j in range(1, n):
            upd = vals[j] > best
            idx = jnp.where(upd, j, idx)
            best = jnp.where(upd, vals[j], best)
        return idx

    i1 = argmax_first(vb)
    i2 = argmax_first([jnp.where(i1 == j, -jnp.inf, vb[j]) for j in range(n)])

    def take(vals, idx):
        out = vals[0]
        for j in range(1, n):
            out = jnp.where(idx == j, vals[j], out)
        return out

    w1, w2 = take(vs, i1), take(vs, i2)
    tot = w1 + w2
    w1, w2 = w1 / tot, w2 / tot
    return gi * n + i1, gi * n + i2, w1, w2


class _MoeLayout:
    def __init__(self, T, tm):
        self.tm = tm
        self.n_tiles = T // tm
        self.group = MOE_GROUP
        self.rt = -(-(2 * tm + N_EXPERTS * (MOE_GROUP - 1)) // MXU_DIM) * MXU_DIM
        self.ng = self.rt // MOE_GROUP
        assert self.ng < LANES
        self.tmx = MOE_ROW_TILE
        self.cap = -(-(T + self.n_tiles * MOE_GROUP) // self.tmx) * self.tmx
        self.dump = N_EXPERTS * self.cap
        self.rows = self.dump + self.rt
        pad = self.n_tiles * N_EXPERTS * (MOE_GROUP - 1)
        self.nt = (2 * T + pad) // self.tmx + N_EXPERTS


def _dispatch_meta(e1, e2, cum, tri, lay):
    E, G = N_EXPERTS, lay.group
    tm = e1.shape[1]
    eid = lax.broadcasted_iota(jnp.int32, (E, tm), 0)
    oh1, oh2 = eid == e1, eid == e2
    cnt = jnp.where(oh1 | oh2, 1.0, 0.0)
    pre = _dot(cnt.astype(BF16), tri)
    tot = jnp.sum(cnt, axis=1, keepdims=True)
    ptot = jnp.broadcast_to(jnp.ceil(tot * (1.0 / G)) * G, (E, LANES))
    below = jnp.where(lax.broadcasted_iota(jnp.int32, (E, E), 0) > lax.broadcasted_iota(jnp.int32, (E, E), 1),
                      1.0, 0.0)
    loff = jnp.dot(below, ptot, preferred_element_type=F32, precision=lax.Precision.HIGHEST)
    pos_e = loff[:, :1] + pre
    pos1 = jnp.sum(jnp.where(oh1, pos_e, 0.0), axis=0, keepdims=True)
    pos2 = jnp.sum(jnp.where(oh2, pos_e, 0.0), axis=0, keepdims=True)

    lane = lax.broadcasted_iota(jnp.int32, (E, LANES), 1)
    g_row = (lane * G).astype(F32)
    eg = jnp.sum(jnp.where(loff + ptot <= g_row, 1, 0), axis=0, keepdims=True)
    erow = lax.broadcasted_iota(jnp.int32, (E, LANES), 0)
    base = erow.astype(F32) * float(lay.cap) + cum - loff
    sel = jnp.sum(jnp.where(erow == eg, base, 0.0), axis=0, keepdims=True)
    dst = jnp.where(eg < E, g_row[:1] + sel, float(lay.dump) + g_row[:1])
    n_used = jnp.sum(ptot[:, :1], axis=0, keepdims=True) * (1.0 / G)
    table = jnp.where(lane[:1] == LANES - 1, n_used, dst).astype(jnp.int32)
    return pos1, pos2, table, cum + ptot


def _outproj_body(y_ref, w_ref, h_ref, g1_ref, gn_ref, sh_ref, sc_ref, wr_ref, br_ref, tri_ref,
                  ho_ref, xn_ref, rows_ref, cols_ref, tbl_ref, cum_ref, *, lay):
    @pl.when(pl.program_id(0) == 0)
    def _():
        cum_ref[...] = jnp.zeros_like(cum_ref)

    hn = h_ref[...] + g1_ref[0] * _dot(y_ref[...], w_ref[...])
    ho_ref[...] = hn
    xn = _norm_mod(hn, gn_ref[...], sh_ref[0], sc_ref[0])
    xb = xn.astype(BF16)
    xn_ref[...] = xb
    xl = (xn - xb.astype(F32)).astype(BF16)
    hl = _dot_nt(wr_ref[...], xb)
    logits_t = hl[:N_EXPERTS] + hl[N_EXPERTS:] + _dot_nt(wr_ref[:N_EXPERTS, :], xl)
    e1, e2, w1, w2 = _route(logits_t, br_ref[...])
    pos1, pos2, table, cum = _dispatch_meta(e1, e2, cum_ref[...], tri_ref[...], lay)
    cum_ref[...] = cum
    tbl_ref[0] = table
    tm = e1.shape[1]
    rows = jnp.concatenate([pos1, pos2, w1, w2, jnp.zeros((SUBLANES - 4, tm), F32)], axis=0)
    rows_ref[...] = rows
    cols_ref[...] = jnp.concatenate([rows, jnp.zeros((LANES - SUBLANES, tm), F32)], axis=0).T


def _outproj_route(y, w, h, g1, gn, sh, sc, wr_t, br, S, lay):
    T, D = h.shape
    K = y.shape[1]
    tm = lay.tm
    per_b = S // tm
    vec = pl.BlockSpec((1, 1, D), lambda i: (i // per_b, 0, 0))
    tok = lambda n: pl.BlockSpec((tm, n), lambda i: (i, 0))
    tri = jnp.triu(jnp.ones((tm, tm), BF16), k=1)
    return pl.pallas_call(
        functools.partial(_outproj_body, lay=lay),
        out_shape=(jax.ShapeDtypeStruct((T, D), F32),
                   jax.ShapeDtypeStruct((T, D), BF16),
                   jax.ShapeDtypeStruct((SUBLANES, T), F32),
                   jax.ShapeDtypeStruct((T, LANES), F32),
                   jax.ShapeDtypeStruct((lay.n_tiles, 1, LANES), jnp.int32),
                   jax.ShapeDtypeStruct((N_EXPERTS, LANES), F32)),
        grid=(T // tm,),
        in_specs=[tok(K),
                  pl.BlockSpec((K, D), lambda i: (0, 0)),
                  tok(D), vec,
                  pl.BlockSpec((1, D), lambda i: (0, 0)),
                  vec, vec,
                  pl.BlockSpec((2 * N_EXPERTS, D), lambda i: (0, 0)),
                  pl.BlockSpec((N_EXPERTS, 1), lambda i: (0, 0)),
                  pl.BlockSpec((tm, tm), lambda i: (0, 0))],
        out_specs=(tok(D), tok(D),
                   pl.BlockSpec((SUBLANES, tm), lambda i: (0, i)),
                   tok(LANES),
                   pl.BlockSpec((1, 1, LANES), lambda i: (i, 0, 0)),
                   pl.BlockSpec((N_EXPERTS, LANES), lambda i: (0, 0))),
        compiler_params=_cparams("arbitrary"),
        name="outproj_route",
    )(y, w, h, g1, gn, sh, sc, wr_t, br, tri)


def _group_copy(hbm_ref, buf_ref, tbl_ref, i, g, sem, lay, to_hbm):
    G = lay.group
    hbm = hbm_ref.at[pl.ds(pl.multiple_of(tbl_ref[i * LANES + g], G), G), :]
    vmem = buf_ref.at[pl.ds(pl.multiple_of(g * G, G), G), :]
    return pltpu.make_async_copy(vmem, hbm, sem) if to_hbm else pltpu.make_async_copy(hbm, vmem, sem)


def _dispatch_body(tbl_ref, x_ref, rows_ref, xs_ref, buf_ref, sem, *, lay):
    i = pl.program_id(0)
    pos1 = rows_ref[0:1, :].astype(jnp.int32)
    pos2 = rows_ref[1:2, :].astype(jnp.int32)
    r = lax.broadcasted_iota(jnp.int32, (lay.rt, lay.tm), 0)
    perm = jnp.where((r == pos1) | (r == pos2), 1.0, 0.0).astype(BF16)
    slot = i % 2
    buf_ref[slot] = _dot(perm, x_ref[...]).astype(BF16)

    def copies(tile, slot_):
        return [_group_copy(xs_ref, buf_ref.at[slot_], tbl_ref, tile, g, sem.at[slot_], lay, True)
                for g in range(lay.ng)]

    @pl.when(i > 0)
    def _():
        for c in copies(i - 1, 1 - slot):
            c.wait()

    for c in copies(i, slot):
        c.start()

    @pl.when(i == pl.num_programs(0) - 1)
    def _():
        for c in copies(i, slot):
            c.wait()


def _moe_dispatch(xn, rows, tbl, lay):
    T, D = xn.shape
    tm = lay.tm
    grid_spec = pltpu.PrefetchScalarGridSpec(
        num_scalar_prefetch=1,
        grid=(lay.n_tiles,),
        in_specs=[pl.BlockSpec((tm, D), lambda i, tbl: (i, 0)),
                  pl.BlockSpec((SUBLANES, tm), lambda i, tbl: (0, i))],
        out_specs=pl.BlockSpec(memory_space=pl.ANY),
        scratch_shapes=[pltpu.VMEM((2, lay.rt, D), BF16), pltpu.SemaphoreType.DMA((2,))],
    )
    return pl.pallas_call(
        functools.partial(_dispatch_body, lay=lay),
        out_shape=jax.ShapeDtypeStruct((lay.rows, D), BF16),
        grid_spec=grid_spec,
        compiler_params=_cparams("arbitrary"),
        name="moe_dispatch",
    )(tbl, xn, rows)


def _expert_body(te_ref, tb_ref, nv_ref, x_ref, wg_ref, wu_ref, wd_ref, y_ref, wgu_bf, wd_bf):
    n = pl.program_id(0)
    f = wg_ref.shape[2]

    @pl.when((n == 0) | (te_ref[n] != te_ref[jnp.maximum(n - 1, 0)]))
    def _():
        wgu_bf[:, :f] = wg_ref[0].astype(BF16)
        wgu_bf[:, f:] = wu_ref[0].astype(BF16)
        wd_bf[...] = wd_ref[0].astype(BF16)

    @pl.when(n < nv_ref[0])
    def _():
        hgu = _dot(x_ref[...], wgu_bf[...])
        hdn = (_silu(hgu[:, :f]) * hgu[:, f:]).astype(BF16)
        y_ref[...] = _dot(hdn, wd_bf[...]).astype(BF16)


def _moe_experts(xs, wg, wu, wd, layer, tile_e, tile_blk, n_valid, lay):
    R, D = xs.shape
    F = wg.shape[3]
    tmx = lay.tmx
    wspec = lambda shape: pl.BlockSpec((None, 1) + shape, lambda n, te, tb, nv: (layer, te[n], 0, 0))
    grid_spec = pltpu.PrefetchScalarGridSpec(
        num_scalar_prefetch=3,
        grid=(lay.nt,),
        in_specs=[pl.BlockSpec((tmx, D), lambda n, te, tb, nv: (tb[n], 0)),
                  wspec((D, F)), wspec((D, F)), wspec((F, D))],
        out_specs=pl.BlockSpec((tmx, D), lambda n, te, tb, nv: (tb[n], 0)),
        scratch_shapes=[pltpu.VMEM((D, 2 * F), BF16), pltpu.VMEM((F, D), BF16)],
    )
    return pl.pallas_call(
        _expert_body,
        out_shape=jax.ShapeDtypeStruct((R, D), BF16),
        grid_spec=grid_spec,
        compiler_params=_cparams("arbitrary"),
        name="moe_experts",
    )(tile_e, tile_blk, n_valid, xs, wg, wu, wd)


def _combine_body(tbl_ref, cols_ref, h_ref, g2_ref, *rest, lay, final):
    if final:
        gfin_ref, ys_ref, o_ref, buf_ref, sem = rest
    else:
        ys_ref, o_ref, buf_ref, sem = rest
    i = pl.program_id(0)
    slot = i % 2

    def fetch(tile, slot_, start):
        def one(g, _):
            c = _group_copy(ys_ref, buf_ref.at[slot_], tbl_ref, tile, g, sem.at[slot_], lay, False)
            c.start() if start else c.wait()
            return 0
        lax.fori_loop(0, tbl_ref[tile * LANES + LANES - 1], one, 0)

    @pl.when(i == 0)
    def _():
        buf_ref[...] = jnp.zeros_like(buf_ref)
        fetch(0, 0, True)

    @pl.when(i + 1 < pl.num_programs(0))
    def _():
        fetch(i + 1, 1 - slot, True)

    fetch(i, slot, False)

    cols = cols_ref[...]
    r = lax.broadcasted_iota(jnp.int32, (lay.tm, lay.rt), 1)
    mix = (jnp.where(r == cols[:, 0:1].astype(jnp.int32), cols[:, 2:3], 0.0)
           + jnp.where(r == cols[:, 1:2].astype(jnp.int32), cols[:, 3:4], 0.0)).astype(BF16)
    out = h_ref[...] + g2_ref[0] * _dot(mix, buf_ref[slot])
    o_ref[...] = _rms(out, gfin_ref[...]) if final else out


def _moe_combine(ys, cols, tbl, h, g2, S, lay, final_gain=None):
    T, D = h.shape
    tm = lay.tm
    per_b = S // tm
    final = final_gain is not None
    in_specs = [pl.BlockSpec((tm, LANES), lambda i, tbl: (i, 0)),
                pl.BlockSpec((tm, D), lambda i, tbl: (i, 0)),
                pl.BlockSpec((1, 1, D), lambda i, tbl: (i // per_b, 0, 0))]
    args = [cols, h, g2]
    if final:
        in_specs.append(pl.BlockSpec((1, D), lambda i, tbl: (0, 0)))
        args.append(final_gain)
    grid_spec = pltpu.PrefetchScalarGridSpec(
        num_scalar_prefetch=1,
        grid=(lay.n_tiles,),
        in_specs=in_specs + [pl.BlockSpec(memory_space=pl.ANY)],
        out_specs=pl.BlockSpec((tm, D), lambda i, tbl: (i, 0)),
        scratch_shapes=[pltpu.VMEM((2, lay.rt, D), BF16), pltpu.SemaphoreType.DMA((2,))],
    )
    return pl.pallas_call(
        functools.partial(_combine_body, lay=lay, final=final),
        out_shape=jax.ShapeDtypeStruct((T, D), F32),
        grid_spec=grid_spec,
        compiler_params=_cparams("arbitrary"),
        name="moe_combine",
    )(tbl, *args, ys)


def _expert_tiles(used, lay):
    tiles = jnp.ceil(used / lay.tmx).astype(jnp.int32)
    ends = jnp.cumsum(tiles)
    n_valid = ends[-1]
    n = jnp.minimum(jnp.arange(lay.nt, dtype=jnp.int32), n_valid - 1)
    e = jnp.sum((ends[None, :] <= n[:, None]).astype(jnp.int32), axis=1)
    blk = e * (lay.cap // lay.tmx) + n - (ends - tiles)[e]
    return e, blk, n_valid.reshape(1)


def _moe(xn, rows, cols, tbl, used, wg, wu, wd, layer, h, g2, S, lay, final_gain=None):
    tbl = tbl.reshape(-1)
    xs = _moe_dispatch(xn, rows, tbl, lay)
    ys = _moe_experts(xs, wg, wu, wd, layer, *_expert_tiles(used, lay), lay)
    return _moe_combine(ys, cols, tbl, h, g2, S, lay, final_gain)


def _kv_body(h_ref, g_ref, sh_ref, sc_ref, wa_ref, gkv_ref, wb_ref, cos_ref, sin_ref,
             k_ref, vt_ref):
    hn = _norm_mod(h_ref[...], g_ref[...], sh_ref[0], sc_ref[0]).astype(BF16)
    a = _dot(hn, wa_ref[...])
    c_kv = _rms(a[:, :KV_LORA], gkv_ref[...]).astype(BF16)
    kr = (a[:, KV_LORA:KV_LORA + LANES] * cos_ref[...]
          + a[:, KV_LORA + LANES:] * sin_ref[...]).astype(BF16)
    kv = _dot(c_kv, wb_ref[...])
    hk = MLA_HEADS * QK_NOPE
    w = QK_NOPE + LANES
    for hd in range(MLA_HEADS):
        k_ref[:, hd * w:hd * w + QK_NOPE] = kv[:, hd * QK_NOPE:(hd + 1) * QK_NOPE].astype(BF16)
        k_ref[:, hd * w + QK_NOPE:(hd + 1) * w] = kr
        vh = kv[:, hk + hd * V_HEAD:hk + (hd + 1) * V_HEAD]
        for g in range(vt_ref.shape[1]):
            vt_ref[hd, g] = vh[g * ATT_KEYS:(g + 1) * ATT_KEYS, :].T.astype(BF16)


def _mla_kv(h, g, sh, sc, wa, gkv, wb, cos, sin, S):
    T, D = h.shape
    tm = min(TOK_TILE, S)
    per_b = S // tm
    vec = pl.BlockSpec((1, 1, D), lambda i: (i // per_b, 0, 0))
    tok = lambda n: pl.BlockSpec((tm, n), lambda i: (i, 0))
    full = lambda a: pl.BlockSpec(a.shape, lambda i: (0, 0))
    kw = MLA_HEADS * (QK_NOPE + LANES)
    gk = tm // ATT_KEYS
    vt_shape = (MLA_HEADS, T // ATT_KEYS, V_HEAD, ATT_KEYS)
    return pl.pallas_call(
        _kv_body,
        out_shape=(jax.ShapeDtypeStruct((T, kw), BF16), jax.ShapeDtypeStruct(vt_shape, BF16)),
        grid=(T // tm,),
        in_specs=[tok(D), full(g), vec, vec, full(wa), full(gkv), full(wb), tok(LANES), tok(LANES)],
        out_specs=(tok(kw),
                   pl.BlockSpec((MLA_HEADS, gk, V_HEAD, ATT_KEYS), lambda i: (0, i, 0, 0))),
        compiler_params=_cparams("parallel"),
        name="mla_kv",
    )(h, g, sh, sc, wa, gkv, wb, cos, sin)


def _q_body(h_ref, g_ref, sh_ref, sc_ref, wa_ref, gq_ref, wb_ref, cos_ref, sin_ref, q_ref, *, scale):
    xn = _norm_mod(h_ref[...], g_ref[...], sh_ref[0], sc_ref[0]).astype(BF16)
    qa = _rms(_dot(xn, wa_ref[...]), gq_ref[...] * scale).astype(BF16)
    cos = cos_ref[...]
    sin = sin_ref[...]
    wi = QK_NOPE + 2 * LANES
    wo = QK_NOPE + LANES
    for hd in range(MLA_HEADS):
        qb = _dot(qa, wb_ref[:, hd * wi:(hd + 1) * wi])
        q_ref[:, hd * wo:hd * wo + QK_NOPE] = qb[:, :QK_NOPE].astype(BF16)
        rp = qb[:, QK_NOPE:QK_NOPE + LANES] * cos + qb[:, QK_NOPE + LANES:] * sin
        q_ref[:, hd * wo + QK_NOPE:(hd + 1) * wo] = rp.astype(BF16)


def _mla_q(h, g, sh, sc, wa, gq, wb, cos, sin, S):
    T, D = h.shape
    tm = min(TOK_TILE, S)
    per_b = S // tm
    vec = pl.BlockSpec((1, 1, D), lambda i: (i // per_b, 0, 0))
    tok = lambda n: pl.BlockSpec((tm, n), lambda i: (i, 0))
    full = lambda a: pl.BlockSpec(a.shape, lambda i: (0, 0))
    qw = MLA_HEADS * (QK_NOPE + LANES)
    return pl.pallas_call(
        functools.partial(_q_body, scale=(QK_NOPE + QK_ROPE) ** -0.5 * LOG2E),
        out_shape=jax.ShapeDtypeStruct((T, qw), BF16),
        grid=(T // tm,),
        in_specs=[tok(D), full(g), vec, vec, full(wa), full(gq), full(wb), tok(LANES), tok(LANES)],
        out_specs=tok(qw),
        compiler_params=_cparams("parallel"),
        name="mla_q",
    )(h, g, sh, sc, wa, gq, wb, cos, sin)


def _flash_body(q_ref, k_ref, vt_ref, bias_ref, o_ref, s_ref, smax_ref, m_ref, l_ref, acc_ref,
                *, tq, ks, hp):
    i = pl.program_id(2)
    last = (i * tq + tq - 1) // ks
    dq = QK_NOPE + LANES
    heads = range(hp)

    def scores(hd, slot, c, lo=0):
        kc = k_ref[pl.ds(pl.multiple_of(c * ks, ks), ks), hd * dq:(hd + 1) * dq]
        s = _dot_nt(kc, q_ref[lo:, hd * dq:(hd + 1) * dq])
        s_ref[hd, slot, :, lo:] = s
        smax_ref[hd, slot, :, lo:] = jnp.max(s, axis=0, keepdims=True)

    def update(hd, slot, c, lo=0, diag=None):
        if diag is None:
            smax = smax_ref[hd, slot, :, lo:]
        else:
            s_ref[hd, slot, :, diag:diag + ks] += bias_ref[...]
            smax = jnp.max(s_ref[hd, slot, :, lo:], axis=0, keepdims=True)
        m = m_ref[hd, :, lo:]
        m_new = jnp.maximum(m, smax)
        alpha = jnp.exp2(m - m_new)
        part = jnp.zeros((SUBLANES, tq - lo), F32)
        ps = []
        for r in range(ks // ATT_SLAB):
            p = jnp.exp2(s_ref[hd, slot, r * ATT_SLAB:(r + 1) * ATT_SLAB, lo:] - m_new)
            for r8 in range(ATT_SLAB // SUBLANES):
                part = part + p[r8 * SUBLANES:(r8 + 1) * SUBLANES, :]
            ps.append(p.astype(BF16))
        m_ref[hd, :, lo:] = m_new
        l_ref[hd, :, lo:] = alpha * l_ref[hd, :, lo:] + jnp.sum(part, axis=0, keepdims=True)
        acc_ref[hd, :, lo:] = (alpha * acc_ref[hd, :, lo:]
                               + _dot(vt_ref[hd, c], jnp.concatenate(ps, axis=0)))

    m_ref[...] = jnp.full_like(m_ref, -jnp.inf)
    l_ref[...] = jnp.zeros_like(l_ref)
    acc_ref[...] = jnp.zeros_like(acc_ref)
    for hd in heads:
        scores(hd, 0, 0)

    def pair(g, _):
        for hd in heads:
            scores(hd, 1, 2 * g + 1)
        for hd in heads:
            update(hd, 0, 2 * g)
        for hd in heads:
            scores(hd, 0, 2 * g + 2)
        for hd in heads:
            update(hd, 1, 2 * g + 1)
        return 0

    assert tq == 2 * ks
    lax.fori_loop(0, i, pair, 0)
    for hd in heads:
        scores(hd, 1, last, lo=ks)
    for hd in heads:
        update(hd, 0, last - 1, diag=0)
    for hd in heads:
        update(hd, 1, last, lo=ks, diag=ks)

    for hd in heads:
        o_ref[:, hd * V_HEAD:(hd + 1) * V_HEAD] = (acc_ref[hd] / l_ref[hd]).T.astype(o_ref.dtype)


def _flash(q, k, vt, S):
    T = q.shape[0]
    B = T // S
    H = MLA_HEADS
    hp = ATT_HEADS
    tq = min(ATT_TILE, S)
    ks = ATT_KEYS
    nq = S // tq
    dq = QK_NOPE + LANES
    chunk = jnp.arange(ks, dtype=jnp.int32) // CHUNK
    bias = jnp.where(chunk[:, None] <= chunk[None, :], 0.0, -jnp.inf).astype(F32)
    return pl.pallas_call(
        functools.partial(_flash_body, tq=tq, ks=ks, hp=hp),
        out_shape=jax.ShapeDtypeStruct((T, H * V_HEAD), BF16),
        grid=(B, H // hp, nq),
        in_specs=[pl.BlockSpec((tq, hp * dq), lambda b, h, i: (b * nq + i, h)),
                  pl.BlockSpec((S, hp * dq), lambda b, h, i: (b, h)),
                  pl.BlockSpec((hp, S // ks, V_HEAD, ks), lambda b, h, i: (h, b, 0, 0)),
                  pl.BlockSpec((ks, ks), lambda b, h, i: (0, 0))],
        out_specs=pl.BlockSpec((tq, hp * V_HEAD), lambda b, h, i: (b * nq + i, h)),
        scratch_shapes=[pltpu.VMEM((hp, 2, ks, tq), F32), pltpu.VMEM((hp, 2, 1, tq), F32),
                        pltpu.VMEM((hp, 1, tq), F32), pltpu.VMEM((hp, 1, tq), F32),
                        pltpu.VMEM((hp, V_HEAD, tq), F32)],
        compiler_params=_cparams("parallel", "parallel", "arbitrary"),
        name="flash_attn",
    )(q, k, vt, bias)


def _rope_swap(w):
    half = w.shape[-1] // 2
    return jnp.concatenate([w[..., half:], w[..., :half]], axis=-1)


def _pad_lanes(w):
    return jnp.pad(w, [(0, 0)] * (w.ndim - 1) + [(0, LANES - w.shape[-1])])


def kernel(x, c, positions, w_mod, b_mod, norm_mix, norm_ffn, ret_w_in, ret_w_out, w_mod_kv, b_mod_kv, norm_kv, mla_w_kv_a, mla_kv_norm, mla_w_kv_b, mla_w_q_a, mla_q_norm, mla_w_q_b, mla_w_o, router_w, router_b, moe_w_gate, moe_w_up, moe_w_down, final_norm):
    B, S, D = x.shape
    T = B * S
    depth = w_mod.shape[0]
    n_a = ret_w_in.shape[0]

    c8 = jnp.pad(c, ((0, 8 - B), (0, 0)))
    mod = _mod_vectors(c8, w_mod, b_mod)[:, :B]
    mod = mod.reshape(depth, B, 6, 1, D)
    kv_mod = _mod_vectors(c8, w_mod_kv[None], b_mod_kv[None])[0, :B].reshape(B, 2, 1, D)

    pos_col = positions.reshape(T, 1)
    inv_ret = (ROPE_THETA ** (-jnp.arange(LANES, dtype=F32) / LANES)).reshape(1, LANES)
    cos_r, sin_r = _rope_tables(pos_col, inv_ret)
    step = LANES // (QK_ROPE // 2)
    c_m, s_m = cos_r[:, ::step], sin_r[:, ::step]
    cos_m = _pad_lanes(jnp.concatenate([c_m, c_m], axis=-1))
    sin_m = _pad_lanes(jnp.concatenate([-s_m, s_m], axis=-1))

    wr_hi = router_w.T.astype(BF16)
    wr_t = jnp.concatenate([wr_hi, (router_w.T - wr_hi.astype(F32)).astype(BF16)], axis=0)
    br = router_b.reshape(N_EXPERTS, 1)

    h = x.reshape(T, D)
    lay = _MoeLayout(T, min(TOK_TILE, S))
    k_full = v_full = None
    for layer in range(depth):
        sh1, sc1, g1, sh2, sc2, g2 = (mod[layer, :, i] for i in range(6))
        gmix = norm_mix[layer].reshape(1, D)
        if layer < n_a:
            proj = _ret_inproj(h, gmix, sh1, sc1, ret_w_in[layer].astype(BF16), cos_r, sin_r, S)
            mix = _retention(proj, S, D)
            w_o = ret_w_out[layer].astype(BF16)
        else:
            if layer == n_a:
                wa = mla_w_kv_a
                wa_r = wa[:, KV_LORA:]
                wa_p = jnp.concatenate([wa[:, :KV_LORA], _pad_lanes(wa_r), _pad_lanes(_rope_swap(wa_r))],
                                       axis=-1).astype(BF16)
                wb = mla_w_kv_b.reshape(KV_LORA, MLA_HEADS, QK_NOPE + V_HEAD)
                wb_p = jnp.concatenate([wb[..., :QK_NOPE].reshape(KV_LORA, -1),
                                        wb[..., QK_NOPE:].reshape(KV_LORA, -1)], axis=-1).astype(BF16)
                k_full, v_full = _mla_kv(h, norm_kv.reshape(1, D), kv_mod[:, 0], kv_mod[:, 1], wa_p,
                                         mla_kv_norm.reshape(1, KV_LORA), wb_p, cos_m, sin_m, S)
            j = layer - n_a
            wq = mla_w_q_b[j].reshape(Q_LORA, MLA_HEADS, QK_NOPE + QK_ROPE)
            wq_r = wq[..., QK_NOPE:]
            wq_p = jnp.concatenate([wq[..., :QK_NOPE], _pad_lanes(wq_r), _pad_lanes(_rope_swap(wq_r))],
                                   axis=-1).reshape(Q_LORA, -1).astype(BF16)
            q_full = _mla_q(h, gmix, sh1, sc1, mla_w_q_a[j].astype(BF16),
                            mla_q_norm[j].reshape(1, Q_LORA), wq_p, cos_m, sin_m, S)
            mix = _flash(q_full, k_full, v_full, S)
            w_o = mla_w_o[j].astype(BF16)
        h, xn, rows, cols, tbl, used = _outproj_route(mix, w_o, h, g1, norm_ffn[layer].reshape(1, D),
                                                      sh2, sc2, wr_t, br, S, lay)
        h = _moe(xn, rows, cols, tbl, used[:, 0], moe_w_gate, moe_w_up, moe_w_down, layer,
                 h, g2, S, lay, final_norm.reshape(1, D) if layer == depth - 1 else None)
    return h.reshape(B, S, D)
```

```python
import functools

import jax
import jax.numpy as jnp
from jax import lax
from jax.experimental import pallas as pl
from jax.experimental.pallas import tpu as pltpu

F32 = jnp.float32
BF16 = jnp.bfloat16

CHUNK = 64
RET_HEADS = 4
MLA_HEADS = 8
QK_NOPE = 128
QK_ROPE = 64
V_HEAD = 128
Q_LORA = 256
KV_LORA = 128
N_EXPERTS = 16
N_GROUPS = 4
EXPERTS_PER_GROUP = N_EXPERTS // N_GROUPS
ROPE_THETA = 10000.0
EPS = 1e-6

LANES = 128
VMEM_LIMIT = 56 * 1024 * 1024
LOG2E = 1.4426950408889634

RET_CHUNK = 256
TOK_TILE = 512
ATT_TILE = 1024
ATT_KEYS = 512
ATT_HEADS = 2
ATT_SLAB = 32
SUBLANES = 8
MXU_DIM = 256
MOE_GROUP = 16
MOE_ROW_TILE = 1024


def _cparams(*sem):
    return pltpu.CompilerParams(dimension_semantics=sem, vmem_limit_bytes=VMEM_LIMIT)


def _silu(x):
    return x * jax.nn.sigmoid(x)


def _rms(x, g):
    return x * lax.rsqrt(jnp.mean(x * x, axis=-1, keepdims=True) + EPS) * g


def _norm_mod(h, g, shift, scale):
    return _rms(h, g) * (1.0 + scale) + shift


def _dot(a, b):
    return jnp.dot(a, b, preferred_element_type=F32)


def _dot_nt(a, b, **kw):
    return lax.dot_general(a, b, (((1,), (1,)), ((), ())), preferred_element_type=F32, **kw)


def _dot_tn(a, b):
    return lax.dot_general(a, b, (((0,), (0,)), ((), ())), preferred_element_type=F32)


def _mod_body(c_ref, w_ref, b_ref, o_ref):
    ca = _silu(c_ref[...])
    o_ref[0] = jnp.dot(ca, w_ref[0], preferred_element_type=F32,
                       precision=lax.Precision.HIGHEST) + b_ref[0]


def _mod_vectors(c8, w, b):
    L, D, N = w.shape
    tn = D
    assert N % tn == 0
    return pl.pallas_call(
        _mod_body,
        out_shape=jax.ShapeDtypeStruct((L, 8, N), F32),
        grid=(L, N // tn),
        in_specs=[pl.BlockSpec((8, D), lambda l, j: (0, 0)),
                  pl.BlockSpec((1, D, tn), lambda l, j: (l, 0, j)),
                  pl.BlockSpec((1, 1, tn), lambda l, j: (l, 0, j))],
        out_specs=pl.BlockSpec((1, 8, tn), lambda l, j: (l, 0, j)),
        compiler_params=_cparams("parallel", "parallel"),
        name="mod_vectors",
    )(c8, w, b.reshape(L, 1, N))


def _rope_body(pos_ref, inv_ref, cm_ref, sm_ref, cos_ref, sin_ref):
    ang = pos_ref[...].astype(F32) * inv_ref[...]
    cos_ref[...] = jnp.cos(ang) * cm_ref[...]
    sin_ref[...] = jnp.sin(ang) * sm_ref[...]


def _rope_tables(pos_col, inv, cm, sm):
    T = pos_col.shape[0]
    tm = min(T, 1024)
    row = pl.BlockSpec((1, LANES), lambda i: (0, 0))
    return pl.pallas_call(
        _rope_body,
        out_shape=(jax.ShapeDtypeStruct((T, LANES), F32),) * 2,
        grid=(T // tm,),
        in_specs=[pl.BlockSpec((tm, 1), lambda i: (i, 0)), row, row, row],
        out_specs=(pl.BlockSpec((tm, LANES), lambda i: (i, 0)),) * 2,
        compiler_params=_cparams("parallel"),
        name="rope_tables",
    )(pos_col, inv, cm, sm)


def _inproj_body(h_ref, g_ref, sh_ref, sc_ref, w_ref, cos_ref, sin_ref, o_ref, *, tn, dk_dim):
    xn = _norm_mod(h_ref[...], g_ref[...], sh_ref[0], sc_ref[0]).astype(BF16)
    d_model = h_ref.shape[1]
    half = dk_dim // 2
    for j in range(w_ref.shape[1] // tn):
        acc = _dot(xn, w_ref[:, j * tn:(j + 1) * tn])
        if j * tn >= 2 * d_model:
            o_ref[:, j * tn:(j + 1) * tn] = acc.astype(BF16)
            continue
        cos = cos_ref[...]
        sin = sin_ref[...]
        scale = 1.0 if j * tn < d_model else dk_dim ** -0.5
        for hd in range(tn // dk_dim):
            x1 = acc[:, hd * dk_dim:hd * dk_dim + half]
            x2 = acc[:, hd * dk_dim + half:(hd + 1) * dk_dim]
            c0 = j * tn + hd * dk_dim
            o_ref[:, c0:c0 + half] = ((x1 * cos - x2 * sin) * scale).astype(BF16)
            o_ref[:, c0 + half:c0 + dk_dim] = ((x1 * sin + x2 * cos) * scale).astype(BF16)


def _ret_inproj(h, g, sh, sc, w, cos, sin, S):
    T, D = h.shape
    N = w.shape[1]
    tm = min(TOK_TILE, S)
    per_b = S // tm
    vec = pl.BlockSpec((1, 1, D), lambda i: (i // per_b, 0, 0))
    tab = pl.BlockSpec((tm, LANES), lambda i: (i, 0))
    return pl.pallas_call(
        functools.partial(_inproj_body, tn=512, dk_dim=D // RET_HEADS),
        out_shape=jax.ShapeDtypeStruct((T, N), BF16),
        grid=(T // tm,),
        in_specs=[pl.BlockSpec((tm, D), lambda i: (i, 0)),
                  pl.BlockSpec((1, D), lambda i: (0, 0)),
                  vec, vec,
                  pl.BlockSpec((D, N), lambda i: (0, 0)),
                  tab, tab],
        out_specs=pl.BlockSpec((tm, N), lambda i: (i, 0)),
        compiler_params=_cparams("parallel"),
        name="ret_inproj",
    )(h, g, sh, sc, w, cos, sin)


def _ret_body(q_ref, k_ref, v_ref, g_ref, di_ref, dq_ref, dk_ref, dc_ref,
              y_ref, state_ref, *, dk_dim):
    @pl.when(pl.program_id(1) == 0)
    def _():
        state_ref[...] = jnp.zeros_like(state_ref)

    dv_dim = 2 * dk_dim
    for hd in range(RET_HEADS):
        qk = slice(hd * dk_dim, (hd + 1) * dk_dim)
        vg = slice(hd * dv_dim, (hd + 1) * dv_dim)
        qb = q_ref[:, qk]
        kb = k_ref[:, qk]
        v = v_ref[:, vg]
        inner = (_dot_nt(qb, kb) * di_ref[hd]).astype(BF16)
        st = state_ref[hd]
        out = _dot(inner, v) + _dot(qb, st.astype(BF16)) * dq_ref[hd]
        kd = (kb.astype(F32) * dk_ref[hd]).astype(BF16)
        state_ref[hd] = st * dc_ref[hd] + _dot_tn(kd, v)

        mu = jnp.mean(out, axis=-1, keepdims=True)
        cen = out - mu
        var = jnp.mean(cen * cen, axis=-1, keepdims=True)
        o = cen * lax.rsqrt(var + EPS)
        y_ref[:, vg] = (_silu(g_ref[:, vg].astype(F32)) * o).astype(BF16)


def _retention(proj, S, D):
    T = proj.shape[0]
    B = T // S
    H = RET_HEADS
    dk = D // H
    dv = 2 * dk
    C = min(RET_CHUNK, S)
    n = S // C
    log_g = jnp.log1p(-(2.0 ** (-5.0 - jnp.arange(H, dtype=F32))))
    t = jnp.arange(C, dtype=F32)
    diff = t[:, None] - t[None, :]
    d_intra = jnp.where(diff >= 0, jnp.exp(log_g[:, None, None] * jnp.maximum(diff, 0.0)), 0.0)
    d_q = jnp.exp(log_g[:, None] * (t + 1.0))[:, :, None]
    d_k = jnp.exp(log_g[:, None] * (C - 1.0 - t))[:, :, None]
    d_c = jnp.exp(log_g * C)[:, None, None]

    assert 2 * D == H * dv
    row = lambda b, i: b * n + i
    const = lambda a: pl.BlockSpec(a.shape, lambda b, i: (0, 0, 0))
    return pl.pallas_call(
        functools.partial(_ret_body, dk_dim=dk),
        out_shape=jax.ShapeDtypeStruct((T, H * dv), BF16),
        grid=(B, n),
        in_specs=[pl.BlockSpec((C, D), lambda b, i: (row(b, i), 0)),
                  pl.BlockSpec((C, D), lambda b, i: (row(b, i), 1)),
                  pl.BlockSpec((C, H * dv), lambda b, i: (row(b, i), 1)),
                  pl.BlockSpec((C, H * dv), lambda b, i: (row(b, i), 2)),
                  const(d_intra), const(d_q), const(d_k), const(d_c)],
        out_specs=pl.BlockSpec((C, H * dv), lambda b, i: (row(b, i), 0)),
        scratch_shapes=[pltpu.VMEM((H, dk, dv), F32)],
        compiler_params=_cparams("parallel", "arbitrary"),
        name="retention",
    )(proj, proj, proj, proj, d_intra, d_q, d_k, d_c)


def _route(logits_t, bias):
    sc = jax.nn.sigmoid(logits_t)
    bi = sc + bias
    s_rows = [sc[e:e + 1, :] for e in range(N_EXPERTS)]
    b_rows = [bi[e:e + 1, :] for e in range(N_EXPERTS)]

    def top2sum(a, b, c, d):
        p, q = jnp.maximum(a, b), jnp.minimum(a, b)
        r, s = jnp.maximum(c, d), jnp.minimum(c, d)
        return jnp.maximum(p, r) + jnp.maximum(jnp.minimum(p, r), jnp.maximum(q, s))

    n = EXPERTS_PER_GROUP
    gs = [top2sum(*b_rows[n * g:n * g + n]) for g in range(N_GROUPS)]
    best, gi = gs[0], jnp.zeros_like(gs[0], dtype=jnp.int32)
    for g in range(1, N_GROUPS):
        upd = gs[g] > best
        gi = jnp.where(upd, g, gi)
        best = jnp.where(upd, gs[g], best)

    def pick(rows, j):
        out = rows[j]
        for g in range(1, N_GROUPS):
            out = jnp.where(gi == g, rows[n * g + j], out)
        return out

    vb = [pick(b_rows, j) for j in range(n)]
    vs = [pick(s_rows, j) for j in range(n)]

    def argmax_first(vals):
        best, idx = vals[0], jnp.zeros_like(gi)
        for j in range(1, n):
            upd = vals[j] > best
            idx = jnp.where(upd, j, idx)
            best = jnp.where(upd, vals[j], best)
        return idx

    i1 = argmax_first(vb)
    i2 = argmax_first([jnp.where(i1 == j, -jnp.inf, vb[j]) for j in range(n)])

    def take(vals, idx):
        out = vals[0]
        for j in range(1, n):
            out = jnp.where(idx == j, vals[j], out)
        return out

    w1, w2 = take(vs, i1), take(vs, i2)
    tot = w1 + w2
    w1, w2 = w1 / tot, w2 / tot
    return gi * n + i1, gi * n + i2, w1, w2


class _MoeLayout:
    def __init__(self, T, tm):
        self.tm = tm
        self.n_tiles = T // tm
        self.group = MOE_GROUP
        self.rt = -(-(2 * tm + N_EXPERTS * (MOE_GROUP - 1)) // MXU_DIM) * MXU_DIM
        self.ng = self.rt // MOE_GROUP
        assert self.ng < LANES
        self.tmx = MOE_ROW_TILE
        self.cap = -(-(T + self.n_tiles * MOE_GROUP) // self.tmx) * self.tmx
        self.dump = N_EXPERTS * self.cap
        self.rows = self.dump + self.rt
        pad = self.n_tiles * N_EXPERTS * (MOE_GROUP - 1)
        self.nt = (2 * T + pad) // self.tmx + N_EXPERTS


def _dispatch_meta(e1, e2, cum, tri, lay):
    E, G = N_EXPERTS, lay.group
    tm = e1.shape[1]
    eid = lax.broadcasted_iota(jnp.int32, (E, tm), 0)
    oh1, oh2 = eid == e1, eid == e2
    cnt = jnp.where(oh1 | oh2, 1.0, 0.0)
    pre = _dot(cnt.astype(BF16), tri)
    tot = jnp.sum(cnt, axis=1, keepdims=True)
    ptot = jnp.broadcast_to(jnp.ceil(tot * (1.0 / G)) * G, (E, LANES))
    below = jnp.where(lax.broadcasted_iota(jnp.int32, (E, E), 0) > lax.broadcasted_iota(jnp.int32, (E, E), 1),
                      1.0, 0.0)
    loff = jnp.dot(below, ptot, preferred_element_type=F32, precision=lax.Precision.HIGHEST)
    pos_e = loff[:, :1] + pre
    pos1 = jnp.sum(jnp.where(oh1, pos_e, 0.0), axis=0, keepdims=True)
    pos2 = jnp.sum(jnp.where(oh2, pos_e, 0.0), axis=0, keepdims=True)

    lane = lax.broadcasted_iota(jnp.int32, (E, LANES), 1)
    g_row = (lane * G).astype(F32)
    eg = jnp.sum(jnp.where(loff + ptot <= g_row, 1, 0), axis=0, keepdims=True)
    erow = lax.broadcasted_iota(jnp.int32, (E, LANES), 0)
    base = erow.astype(F32) * float(lay.cap) + cum - loff
    sel = jnp.sum(jnp.where(erow == eg, base, 0.0), axis=0, keepdims=True)
    dst = jnp.where(eg < E, g_row[:1] + sel, float(lay.dump) + g_row[:1])
    n_used = jnp.sum(ptot[:, :1], axis=0, keepdims=True) * (1.0 / G)
    table = jnp.where(lane[:1] == LANES - 1, n_used, dst).astype(jnp.int32)
    return pos1, pos2, table, cum + ptot


def _outproj_body(y_ref, w_ref, h_ref, g1_ref, gn_ref, sh_ref, sc_ref, wr_ref, br_ref, tri_ref,
                  ho_ref, xn_ref, rows_ref, cols_ref, tbl_ref, cum_ref, *, lay):
    @pl.when(pl.program_id(0) == 0)
    def _():
        cum_ref[...] = jnp.zeros_like(cum_ref)

    hn = h_ref[...] + g1_ref[0] * _dot(y_ref[...], w_ref[...])
    ho_ref[...] = hn
    xn = _norm_mod(hn, gn_ref[...], sh_ref[0], sc_ref[0])
    xb = xn.astype(BF16)
    xn_ref[...] = xb
    xl = (xn - xb.astype(F32)).astype(BF16)
    hl = _dot_nt(wr_ref[...], xb)
    logits_t = hl[:N_EXPERTS] + hl[N_EXPERTS:] + _dot_nt(wr_ref[:N_EXPERTS, :], xl)
    e1, e2, w1, w2 = _route(logits_t, br_ref[...])
    pos1, pos2, table, cum = _dispatch_meta(e1, e2, cum_ref[...], tri_ref[...], lay)
    cum_ref[...] = cum
    tbl_ref[0] = table
    tm = e1.shape[1]
    rows = jnp.concatenate([pos1, pos2, w1, w2, jnp.zeros((SUBLANES - 4, tm), F32)], axis=0)
    rows_ref[...] = rows
    cols_ref[...] = jnp.concatenate([rows, jnp.zeros((LANES - SUBLANES, tm), F32)], axis=0).T


def _outproj_route(y, w, h, g1, gn, sh, sc, wr_t, br, S, lay):
    T, D = h.shape
    K = y.shape[1]
    tm = lay.tm
    per_b = S // tm
    vec = pl.BlockSpec((1, 1, D), lambda i: (i // per_b, 0, 0))
    tok = lambda n: pl.BlockSpec((tm, n), lambda i: (i, 0))
    tri = jnp.triu(jnp.ones((tm, tm), BF16), k=1)
    return pl.pallas_call(
        functools.partial(_outproj_body, lay=lay),
        out_shape=(jax.ShapeDtypeStruct((T, D), F32),
                   jax.ShapeDtypeStruct((T, D), BF16),
                   jax.ShapeDtypeStruct((SUBLANES, T), F32),
                   jax.ShapeDtypeStruct((T, LANES), F32),
                   jax.ShapeDtypeStruct((lay.n_tiles, 1, LANES), jnp.int32),
                   jax.ShapeDtypeStruct((N_EXPERTS, LANES), F32)),
        grid=(T // tm,),
        in_specs=[tok(K),
                  pl.BlockSpec((K, D), lambda i: (0, 0)),
                  tok(D), vec,
                  pl.BlockSpec((1, D), lambda i: (0, 0)),
                  vec, vec,
                  pl.BlockSpec((2 * N_EXPERTS, D), lambda i: (0, 0)),
                  pl.BlockSpec((N_EXPERTS, 1), lambda i: (0, 0)),
                  pl.BlockSpec((tm, tm), lambda i: (0, 0))],
        out_specs=(tok(D), tok(D),
                   pl.BlockSpec((SUBLANES, tm), lambda i: (0, i)),
                   tok(LANES),
                   pl.BlockSpec((1, 1, LANES), lambda i: (i, 0, 0)),
                   pl.BlockSpec((N_EXPERTS, LANES), lambda i: (0, 0))),
        compiler_params=_cparams("arbitrary"),
        name="outproj_route",
    )(y, w, h, g1, gn, sh, sc, wr_t, br, tri)


def _group_copy(hbm_ref, buf_ref, tbl_ref, i, g, sem, lay, to_hbm):
    G = lay.group
    hbm = hbm_ref.at[pl.ds(pl.multiple_of(tbl_ref[i * LANES + g], G), G), :]
    vmem = buf_ref.at[pl.ds(pl.multiple_of(g * G, G), G), :]
    return pltpu.make_async_copy(vmem, hbm, sem) if to_hbm else pltpu.make_async_copy(hbm, vmem, sem)


def _dispatch_body(tbl_ref, x_ref, rows_ref, xs_ref, buf_ref, sem, *, lay):
    i = pl.program_id(0)
    pos1 = rows_ref[0:1, :].astype(jnp.int32)
    pos2 = rows_ref[1:2, :].astype(jnp.int32)
    r = lax.broadcasted_iota(jnp.int32, (lay.rt, lay.tm), 0)
    perm = jnp.where((r == pos1) | (r == pos2), 1.0, 0.0).astype(BF16)
    slot = i % 2
    buf_ref[slot] = _dot(perm, x_ref[...]).astype(BF16)

    def copies(tile, slot_):
        return [_group_copy(xs_ref, buf_ref.at[slot_], tbl_ref, tile, g, sem.at[slot_], lay, True)
                for g in range(lay.ng)]

    @pl.when(i > 0)
    def _():
        for c in copies(i - 1, 1 - slot):
            c.wait()

    for c in copies(i, slot):
        c.start()

    @pl.when(i == pl.num_programs(0) - 1)
    def _():
        for c in copies(i, slot):
            c.wait()


def _moe_dispatch(xn, rows, tbl, lay):
    T, D = xn.shape
    tm = lay.tm
    grid_spec = pltpu.PrefetchScalarGridSpec(
        num_scalar_prefetch=1,
        grid=(lay.n_tiles,),
        in_specs=[pl.BlockSpec((tm, D), lambda i, tbl: (i, 0)),
                  pl.BlockSpec((SUBLANES, tm), lambda i, tbl: (0, i))],
        out_specs=pl.BlockSpec(memory_space=pl.ANY),
        scratch_shapes=[pltpu.VMEM((2, lay.rt, D), BF16), pltpu.SemaphoreType.DMA((2,))],
    )
    return pl.pallas_call(
        functools.partial(_dispatch_body, lay=lay),
        out_shape=jax.ShapeDtypeStruct((lay.rows, D), BF16),
        grid_spec=grid_spec,
        compiler_params=_cparams("arbitrary"),
        name="moe_dispatch",
    )(tbl, xn, rows)


def _expert_body(te_ref, tb_ref, nv_ref, x_ref, wg_ref, wu_ref, wd_ref, y_ref, wgu_bf, wd_bf):
    n = pl.program_id(0)
    f = wg_ref.shape[2]

    @pl.when((n == 0) | (te_ref[n] != te_ref[jnp.maximum(n - 1, 0)]))
    def _():
        wgu_bf[:, :f] = wg_ref[0].astype(BF16)
        wgu_bf[:, f:] = wu_ref[0].astype(BF16)
        wd_bf[...] = wd_ref[0].astype(BF16)

    @pl.when(n < nv_ref[0])
    def _():
        hgu = _dot(x_ref[...], wgu_bf[...])
        hdn = (_silu(hgu[:, :f]) * hgu[:, f:]).astype(BF16)
        y_ref[...] = _dot(hdn, wd_bf[...]).astype(BF16)


def _moe_experts(xs, wg, wu, wd, layer, tile_e, tile_blk, n_valid, lay):
    R, D = xs.shape
    F = wg.shape[3]
    tmx = lay.tmx
    wspec = lambda shape: pl.BlockSpec((None, 1) + shape, lambda n, te, tb, nv: (layer, te[n], 0, 0))
    grid_spec = pltpu.PrefetchScalarGridSpec(
        num_scalar_prefetch=3,
        grid=(lay.nt,),
        in_specs=[pl.BlockSpec((tmx, D), lambda n, te, tb, nv: (tb[n], 0)),
                  wspec((D, F)), wspec((D, F)), wspec((F, D))],
        out_specs=pl.BlockSpec((tmx, D), lambda n, te, tb, nv: (tb[n], 0)),
        scratch_shapes=[pltpu.VMEM((D, 2 * F), BF16), pltpu.VMEM((F, D), BF16)],
    )
    return pl.pallas_call(
        _expert_body,
        out_shape=jax.ShapeDtypeStruct((R, D), BF16),
        grid_spec=grid_spec,
        compiler_params=_cparams("arbitrary"),
        name="moe_experts",
    )(tile_e, tile_blk, n_valid, xs, wg, wu, wd)


def _combine_body(tbl_ref, cols_ref, h_ref, g2_ref, *rest, lay, final):
    if final:
        gfin_ref, ys_ref, o_ref, buf_ref, sem = rest
    else:
        ys_ref, o_ref, buf_ref, sem = rest
    i = pl.program_id(0)
    slot = i % 2

    def fetch(tile, slot_, start):
        def one(g, _):
            c = _group_copy(ys_ref, buf_ref.at[slot_], tbl_ref, tile, g, sem.at[slot_], lay, False)
            c.start() if start else c.wait()
            return 0
        lax.fori_loop(0, tbl_ref[tile * LANES + LANES - 1], one, 0)

    @pl.when(i == 0)
    def _():
        buf_ref[...] = jnp.zeros_like(buf_ref)
        fetch(0, 0, True)

    @pl.when(i + 1 < pl.num_programs(0))
    def _():
        fetch(i + 1, 1 - slot, True)

    fetch(i, slot, False)

    cols = cols_ref[...]
    r = lax.broadcasted_iota(jnp.int32, (lay.tm, lay.rt), 1)
    mix = (jnp.where(r == cols[:, 0:1].astype(jnp.int32), cols[:, 2:3], 0.0)
           + jnp.where(r == cols[:, 1:2].astype(jnp.int32), cols[:, 3:4], 0.0)).astype(BF16)
    out = h_ref[...] + g2_ref[0] * _dot(mix, buf_ref[slot])
    o_ref[...] = _rms(out, gfin_ref[...]) if final else out


def _moe_combine(ys, cols, tbl, h, g2, S, lay, final_gain=None):
    T, D = h.shape
    tm = lay.tm
    per_b = S // tm
    final = final_gain is not None
    in_specs = [pl.BlockSpec((tm, LANES), lambda i, tbl: (i, 0)),
                pl.BlockSpec((tm, D), lambda i, tbl: (i, 0)),
                pl.BlockSpec((1, 1, D), lambda i, tbl: (i // per_b, 0, 0))]
    args = [cols, h, g2]
    if final:
        in_specs.append(pl.BlockSpec((1, D), lambda i, tbl: (0, 0)))
        args.append(final_gain)
    grid_spec = pltpu.PrefetchScalarGridSpec(
        num_scalar_prefetch=1,
        grid=(lay.n_tiles,),
        in_specs=in_specs + [pl.BlockSpec(memory_space=pl.ANY)],
        out_specs=pl.BlockSpec((tm, D), lambda i, tbl: (i, 0)),
        scratch_shapes=[pltpu.VMEM((2, lay.rt, D), BF16), pltpu.SemaphoreType.DMA((2,))],
    )
    return pl.pallas_call(
        functools.partial(_combine_body, lay=lay, final=final),
        out_shape=jax.ShapeDtypeStruct((T, D), F32),
        grid_spec=grid_spec,
        compiler_params=_cparams("arbitrary"),
        name="moe_combine",
    )(tbl, *args, ys)


def _expert_tiles(used, lay):
    tiles = jnp.ceil(used / lay.tmx).astype(jnp.int32)
    ends = jnp.cumsum(tiles)
    n_valid = ends[-1]
    n = jnp.minimum(jnp.arange(lay.nt, dtype=jnp.int32), n_valid - 1)
    e = jnp.sum((ends[None, :] <= n[:, None]).astype(jnp.int32), axis=1)
    blk = e * (lay.cap // lay.tmx) + n - (ends - tiles)[e]
    return e, blk, n_valid.reshape(1)


def _moe(xn, rows, cols, tbl, used, wg, wu, wd, layer, h, g2, S, lay, final_gain=None):
    tbl = tbl.reshape(-1)
    xs = _moe_dispatch(xn, rows, tbl, lay)
    ys = _moe_experts(xs, wg, wu, wd, layer, *_expert_tiles(used, lay), lay)
    return _moe_combine(ys, cols, tbl, h, g2, S, lay, final_gain)


def _kv_body(h_ref, g_ref, sh_ref, sc_ref, wa_ref, gkv_ref, wb_ref, cos_ref, sin_ref,
             k_ref, vt_ref):
    hn = _norm_mod(h_ref[...], g_ref[...], sh_ref[0], sc_ref[0]).astype(BF16)
    a = _dot(hn, wa_ref[...])
    c_kv = _rms(a[:, :KV_LORA], gkv_ref[...]).astype(BF16)
    kr = (a[:, KV_LORA:KV_LORA + LANES] * cos_ref[...]
          + a[:, KV_LORA + LANES:] * sin_ref[...]).astype(BF16)
    kv = _dot(c_kv, wb_ref[...])
    hk = MLA_HEADS * QK_NOPE
    w = QK_NOPE + LANES
    for hd in range(MLA_HEADS):
        k_ref[:, hd * w:hd * w + QK_NOPE] = kv[:, hd * QK_NOPE:(hd + 1) * QK_NOPE].astype(BF16)
        k_ref[:, hd * w + QK_NOPE:(hd + 1) * w] = kr
        vh = kv[:, hk + hd * V_HEAD:hk + (hd + 1) * V_HEAD]
        for g in range(vt_ref.shape[1]):
            vt_ref[hd, g] = vh[g * ATT_KEYS:(g + 1) * ATT_KEYS, :].T.astype(BF16)


def _mla_kv(h, g, sh, sc, wa, gkv, wb, cos, sin, S):
    T, D = h.shape
    tm = min(TOK_TILE, S)
    per_b = S // tm
    vec = pl.BlockSpec((1, 1, D), lambda i: (i // per_b, 0, 0))
    tok = lambda n: pl.BlockSpec((tm, n), lambda i: (i, 0))
    full = lambda a: pl.BlockSpec(a.shape, lambda i: (0, 0))
    kw = MLA_HEADS * (QK_NOPE + LANES)
    gk = tm // ATT_KEYS
    vt_shape = (MLA_HEADS, T // ATT_KEYS, V_HEAD, ATT_KEYS)
    return pl.pallas_call(
        _kv_body,
        out_shape=(jax.ShapeDtypeStruct((T, kw), BF16), jax.ShapeDtypeStruct(vt_shape, BF16)),
        grid=(T // tm,),
        in_specs=[tok(D), full(g), vec, vec, full(wa), full(gkv), full(wb), tok(LANES), tok(LANES)],
        out_specs=(tok(kw),
                   pl.BlockSpec((MLA_HEADS, gk, V_HEAD, ATT_KEYS), lambda i: (0, i, 0, 0))),
        compiler_params=_cparams("parallel"),
        name="mla_kv",
    )(h, g, sh, sc, wa, gkv, wb, cos, sin)


def _q_body(h_ref, g_ref, sh_ref, sc_ref, wa_ref, gq_ref, wb_ref, cos_ref, sin_ref, q_ref, *, scale):
    xn = _norm_mod(h_ref[...], g_ref[...], sh_ref[0], sc_ref[0]).astype(BF16)
    qa = _rms(_dot(xn, wa_ref[...]), gq_ref[...] * scale).astype(BF16)
    cos = cos_ref[...]
    sin = sin_ref[...]
    wi = QK_NOPE + 2 * LANES
    wo = QK_NOPE + LANES
    for hd in range(MLA_HEADS):
        qb = _dot(qa, wb_ref[:, hd * wi:(hd + 1) * wi])
        q_ref[:, hd * wo:hd * wo + QK_NOPE] = qb[:, :QK_NOPE].astype(BF16)
        rp = qb[:, QK_NOPE:QK_NOPE + LANES] * cos + qb[:, QK_NOPE + LANES:] * sin
        q_ref[:, hd * wo + QK_NOPE:(hd + 1) * wo] = rp.astype(BF16)


def _mla_q(h, g, sh, sc, wa, gq, wb, cos, sin, S):
    T, D = h.shape
    tm = min(TOK_TILE, S)
    per_b = S // tm
    vec = pl.BlockSpec((1, 1, D), lambda i: (i // per_b, 0, 0))
    tok = lambda n: pl.BlockSpec((tm, n), lambda i: (i, 0))
    full = lambda a: pl.BlockSpec(a.shape, lambda i: (0, 0))
    qw = MLA_HEADS * (QK_NOPE + LANES)
    return pl.pallas_call(
        functools.partial(_q_body, scale=(QK_NOPE + QK_ROPE) ** -0.5 * LOG2E),
        out_shape=jax.ShapeDtypeStruct((T, qw), BF16),
        grid=(T // tm,),
        in_specs=[tok(D), full(g), vec, vec, full(wa), full(gq), full(wb), tok(LANES), tok(LANES)],
        out_specs=tok(qw),
        compiler_params=_cparams("parallel"),
        name="mla_q",
    )(h, g, sh, sc, wa, gq, wb, cos, sin)


def _flash_body(q_ref, k_ref, vt_ref, bias_ref, o_ref, s_ref, smax_ref, m_ref, l_ref, acc_ref,
                *, tq, ks, hp):
    i = pl.program_id(2)
    last = (i * tq + tq - 1) // ks
    dq = QK_NOPE + LANES
    heads = range(hp)

    def scores(hd, slot, c, lo=0):
        kc = k_ref[pl.ds(pl.multiple_of(c * ks, ks), ks), hd * dq:(hd + 1) * dq]
        s = _dot_nt(kc, q_ref[lo:, hd * dq:(hd + 1) * dq])
        s_ref[hd, slot, :, lo:] = s
        smax_ref[hd, slot, :, lo:] = jnp.max(s, axis=0, keepdims=True)

    def update(hd, slot, c, lo=0, diag=None):
        if diag is None:
            smax = smax_ref[hd, slot, :, lo:]
        else:
            s_ref[hd, slot, :, diag:diag + ks] += bias_ref[...]
            smax = jnp.max(s_ref[hd, slot, :, lo:], axis=0, keepdims=True)
        m = m_ref[hd, :, lo:]
        m_new = jnp.maximum(m, smax)
        alpha = jnp.exp2(m - m_new)
        part = jnp.zeros((SUBLANES, tq - lo), F32)
        ps = []
        for r in range(ks // ATT_SLAB):
            p = jnp.exp2(s_ref[hd, slot, r * ATT_SLAB:(r + 1) * ATT_SLAB, lo:] - m_new)
            for r8 in range(ATT_SLAB // SUBLANES):
                part = part + p[r8 * SUBLANES:(r8 + 1) * SUBLANES, :]
            ps.append(p.astype(BF16))
        m_ref[hd, :, lo:] = m_new
        l_ref[hd, :, lo:] = alpha * l_ref[hd, :, lo:] + jnp.sum(part, axis=0, keepdims=True)
        acc_ref[hd, :, lo:] = (alpha * acc_ref[hd, :, lo:]
                               + _dot(vt_ref[hd, c], jnp.concatenate(ps, axis=0)))

    m_ref[...] = jnp.full_like(m_ref, -jnp.inf)
    l_ref[...] = jnp.zeros_like(l_ref)
    acc_ref[...] = jnp.zeros_like(acc_ref)
    for hd in heads:
        scores(hd, 0, 0)

    def pair(g, _):
        for hd in heads:
            scores(hd, 1, 2 * g + 1)
        for hd in heads:
            update(hd, 0, 2 * g)
        for hd in heads:
            scores(hd, 0, 2 * g + 2)
        for hd in heads:
            update(hd, 1, 2 * g + 1)
        return 0

    assert tq == 2 * ks
    lax.fori_loop(0, i, pair, 0)
    for hd in heads:
        scores(hd, 1, last, lo=ks)
    for hd in heads:
        update(hd, 0, last - 1, diag=0)
    for hd in heads:
        update(hd, 1, last, lo=ks, diag=ks)

    for hd in heads:
        o_ref[:, hd * V_HEAD:(hd + 1) * V_HEAD] = (acc_ref[hd] / l_ref[hd]).T.astype(o_ref.dtype)


def _flash(q, k, vt, S):
    T = q.shape[0]
    B = T // S
    H = MLA_HEADS
    hp = ATT_HEADS
    tq = min(ATT_TILE, S)
    ks = ATT_KEYS
    nq = S // tq
    dq = QK_NOPE + LANES
    chunk = jnp.arange(ks, dtype=jnp.int32) // CHUNK
    bias = jnp.where(chunk[:, None] <= chunk[None, :], 0.0, -jnp.inf).astype(F32)
    return pl.pallas_call(
        functools.partial(_flash_body, tq=tq, ks=ks, hp=hp),
        out_shape=jax.ShapeDtypeStruct((T, H * V_HEAD), BF16),
        grid=(B, H // hp, nq),
        in_specs=[pl.BlockSpec((tq, hp * dq), lambda b, h, i: (b * nq + i, h)),
                  pl.BlockSpec((S, hp * dq), lambda b, h, i: (b, h)),
                  pl.BlockSpec((hp, S // ks, V_HEAD, ks), lambda b, h, i: (h, b, 0, 0)),
                  pl.BlockSpec((ks, ks), lambda b, h, i: (0, 0))],
        out_specs=pl.BlockSpec((tq, hp * V_HEAD), lambda b, h, i: (b * nq + i, h)),
        scratch_shapes=[pltpu.VMEM((hp, 2, ks, tq), F32), pltpu.VMEM((hp, 2, 1, tq), F32),
                        pltpu.VMEM((hp, 1, tq), F32), pltpu.VMEM((hp, 1, tq), F32),
                        pltpu.VMEM((hp, V_HEAD, tq), F32)],
        compiler_params=_cparams("parallel", "parallel", "arbitrary"),
        name="flash_attn",
    )(q, k, vt, bias)


def _rope_swap(w):
    half = w.shape[-1] // 2
    return jnp.concatenate([w[..., half:], w[..., :half]], axis=-1)


def _pad_lanes(w):
    return jnp.pad(w, [(0, 0)] * (w.ndim - 1) + [(0, LANES - w.shape[-1])])


def kernel(x, c, positions, w_mod, b_mod, norm_mix, norm_ffn, ret_w_in, ret_w_out, w_mod_kv, b_mod_kv, norm_kv, mla_w_kv_a, mla_kv_norm, mla_w_kv_b, mla_w_q_a, mla_q_norm, mla_w_q_b, mla_w_o, router_w, router_b, moe_w_gate, moe_w_up, moe_w_down, final_norm):
    B, S, D = x.shape
    T = B * S
    depth = w_mod.shape[0]
    n_a = ret_w_in.shape[0]

    c8 = jnp.pad(c, ((0, 8 - B), (0, 0)))
    mod = _mod_vectors(c8, w_mod, b_mod)[:, :B]
    mod = mod.reshape(depth, B, 6, 1, D)
    kv_mod = _mod_vectors(c8, w_mod_kv[None], b_mod_kv[None])[0, :B].reshape(B, 2, 1, D)

    pos_col = positions.reshape(T, 1)
    ones = jnp.ones((1, LANES), F32)
    inv_ret = (ROPE_THETA ** (-jnp.arange(LANES, dtype=F32) / LANES)).reshape(1, LANES)
    cos_r, sin_r = _rope_tables(pos_col, inv_ret, ones, ones)
    hr = QK_ROPE // 2
    inv_m = ROPE_THETA ** (-jnp.arange(hr, dtype=F32) / hr)
    inv_m = _pad_lanes(jnp.concatenate([inv_m, inv_m])[None])
    cm = _pad_lanes(jnp.ones((1, QK_ROPE), F32))
    sm = _pad_lanes(jnp.concatenate([-jnp.ones((1, hr), F32), jnp.ones((1, hr), F32)], axis=-1))
    cos_m, sin_m = _rope_tables(pos_col, inv_m, cm, sm)

    wr_hi = router_w.T.astype(BF16)
    wr_t = jnp.concatenate([wr_hi, (router_w.T - wr_hi.astype(F32)).astype(BF16)], axis=0)
    br = router_b.reshape(N_EXPERTS, 1)

    h = x.reshape(T, D)
    lay = _MoeLayout(T, min(TOK_TILE, S))
    k_full = v_full = None
    for layer in range(depth):
        sh1, sc1, g1, sh2, sc2, g2 = (mod[layer, :, i] for i in range(6))
        gmix = norm_mix[layer].reshape(1, D)
        if layer < n_a:
            proj = _ret_inproj(h, gmix, sh1, sc1, ret_w_in[layer].astype(BF16), cos_r, sin_r, S)
            mix = _retention(proj, S, D)
            w_o = ret_w_out[layer].astype(BF16)
        else:
            if layer == n_a:
                wa = mla_w_kv_a
                wa_r = wa[:, KV_LORA:]
                wa_p = jnp.concatenate([wa[:, :KV_LORA], _pad_lanes(wa_r), _pad_lanes(_rope_swap(wa_r))],
                                       axis=-1).astype(BF16)
                wb = mla_w_kv_b.reshape(KV_LORA, MLA_HEADS, QK_NOPE + V_HEAD)
                wb_p = jnp.concatenate([wb[..., :QK_NOPE].reshape(KV_LORA, -1),
                                        wb[..., QK_NOPE:].reshape(KV_LORA, -1)], axis=-1).astype(BF16)
                k_full, v_full = _mla_kv(h, norm_kv.reshape(1, D), kv_mod[:, 0], kv_mod[:, 1], wa_p,
                                         mla_kv_norm.reshape(1, KV_LORA), wb_p, cos_m, sin_m, S)
            j = layer - n_a
            wq = mla_w_q_b[j].reshape(Q_LORA, MLA_HEADS, QK_NOPE + QK_ROPE)
            wq_r = wq[..., QK_NOPE:]
            wq_p = jnp.concatenate([wq[..., :QK_NOPE], _pad_lanes(wq_r), _pad_lanes(_rope_swap(wq_r))],
                                   axis=-1).reshape(Q_LORA, -1).astype(BF16)
            q_full = _mla_q(h, gmix, sh1, sc1, mla_w_q_a[j].astype(BF16),
                            mla_q_norm[j].reshape(1, Q_LORA), wq_p, cos_m, sin_m, S)
            mix = _flash(q_full, k_full, v_full, S)
            w_o = mla_w_o[j].astype(BF16)
        h, xn, rows, cols, tbl, used = _outproj_route(mix, w_o, h, g1, norm_ffn[layer].reshape(1, D),
                                                      sh2, sc2, wr_t, br, S, lay)
        h = _moe(xn, rows, cols, tbl, used[:, 0], moe_w_gate, moe_w_up, moe_w_down, layer,
                 h, g2, S, lay, final_norm.reshape(1, D) if layer == depth - 1 else None)
    return h.reshape(B, S, D)
```
